```python
import jax, jax.numpy as jnp
from jax import lax
import numpy as np

D_MODEL = 1024
BATCH = 4
SEQ = 4096
DEPTH = 4
DEC_BATCH = 32
DEC_SEQ = 1
PAST_LEN = 8192
PAGE_SIZE = 128

BRANCH_WIDTH = 512
N_BRANCH = 3

RW_HEADS = 8
RW_HEAD_DIM = 64
RW_WIDTH = RW_HEADS * RW_HEAD_DIM
RW_DECAY_RANK = 64
RW_ICLR_RANK = 64
RW_GATE_RANK = 128
RW_COLS = 3 * RW_WIDTH + RW_DECAY_RANK + RW_ICLR_RANK + RW_GATE_RANK
RW_SPLITS = (RW_WIDTH, 2 * RW_WIDTH, 3 * RW_WIDTH, 3 * RW_WIDTH + RW_DECAY_RANK, 3 * RW_WIDTH + RW_DECAY_RANK + RW_ICLR_RANK)
RW_GN_EPS = 64e-5

NSA_HEADS = 8
NSA_KV_HEADS = 2
NSA_HEAD_DIM = 64
NSA_GROUP = NSA_HEADS // NSA_KV_HEADS
NSA_WIDTH = NSA_HEADS * NSA_HEAD_DIM
NSA_KV_WIDTH = NSA_KV_HEADS * NSA_HEAD_DIM
NSA_COLS = NSA_WIDTH + 6 * NSA_KV_WIDTH + 3 * NSA_HEADS
CMP_BLOCK = 32
SEL_BLOCK = 64
TOP_N = 8
WINDOW = 512
Q_BLOCK = 128
NEG_INF = -1e30
FORCE_BONUS = 1e9

POOL_GROUPS = 4
POOL_GROUP_DIM = 128
POOL_WIDTH = POOL_GROUPS * POOL_GROUP_DIM
POOL_WINDOWS = (2, 4, 8, 16)
POOL_HIST = max(POOL_WINDOWS) - 1

RW_END = RW_COLS
NSA_END = RW_END + NSA_COLS
POOL_END = NSA_END + POOL_WIDTH
IN_COLS = POOL_END + N_BRANCH * D_MODEL

MOE_GROUPS = 4
EXPERTS_PER_GROUP = 4
N_EXPERTS = MOE_GROUPS * EXPERTS_PER_GROUP
TOP_K = 2
EXPERT_HIDDEN = 256
MOE_BLOCK = 128

LN_EPS = 1e-5
DEEPNORM_ALPHA = (2 * DEPTH) ** 0.25
DEEPNORM_BETA = (8 * DEPTH) ** -0.25

kernel_name = 'hybrid_rwkv7_nsa_pool_hmoe_step'


def layer_norm(x, g, b):
    xf = x.astype(jnp.float32)
    mu = jnp.mean(xf, axis=-1, keepdims=True)
    var = jnp.mean(jnp.square(xf - mu), axis=-1, keepdims=True)
    return ((xf - mu) * lax.rsqrt(var + LN_EPS) * g + b).astype(x.dtype)


def masked_softmax(s, mask):
    s = jnp.where(mask, s, NEG_INF)
    m = jnp.max(s, axis=-1, keepdims=True)
    e = jnp.where(mask, jnp.exp(s - m), 0.0)
    return e / jnp.maximum(jnp.sum(e, axis=-1, keepdims=True), 1e-30)


def alibi_slopes(n):
    return jnp.exp2(-8.0 * jnp.arange(1, n + 1, dtype=jnp.float32) / n)


def rwkv7_time_mix(p, prev_row, s0, mu, w0, w_up, a0, a_up, g_up, k_k, k_a, r_k, gn_g, gn_b):
    B, T, _ = p.shape
    pf = p.astype(jnp.float32)
    prev = jnp.concatenate([prev_row[:, None].astype(jnp.float32), pf[:, :-1]], axis=1)
    xs = pf + (prev - pf) * mu
    r, xk, v, wd, ad, gd = jnp.split(xs, RW_SPLITS, axis=-1)
    w_log = -jax.nn.softplus(-(w0 + jnp.tanh(wd) @ w_up)) - 0.5
    decay = jnp.exp(-jnp.exp(w_log))
    a = jax.nn.sigmoid(a0 + ad @ a_up)
    g = jax.nn.sigmoid(gd) @ g_up
    heads = lambda z: z.reshape(B, T, RW_HEADS, RW_HEAD_DIM)
    kk = heads(xk * k_k)
    kk = kk / jnp.maximum(jnp.sqrt(jnp.sum(kk * kk, axis=-1, keepdims=True)), 1e-12)
    kmod = heads(xk * (1.0 + (a - 1.0) * k_a))
    r, v, decay, a = heads(r), heads(v), heads(decay), heads(a)
    seq = tuple(jnp.moveaxis(z, 1, 0) for z in (r, decay, kmod, v, kk, kk * a))

    def step(S, inp):
        r_t, w_t, k_t, v_t, kk_t, kka_t = inp
        sk = jnp.einsum('bhij,bhj->bhi', S, kk_t)
        S = S * w_t[:, :, None, :] - sk[..., None] * kka_t[:, :, None, :] + v_t[..., None] * k_t[:, :, None, :]
        return S, jnp.einsum('bhij,bhj->bhi', S, r_t)

    s_last, y = lax.scan(step, s0.astype(jnp.float32), seq)
    y = jnp.moveaxis(y, 0, 1)
    ym = jnp.mean(y, axis=-1, keepdims=True)
    yv = jnp.mean(jnp.square(y - ym), axis=-1, keepdims=True)
    y = ((y - ym) * lax.rsqrt(yv + RW_GN_EPS)).reshape(B, T, RW_WIDTH) * gn_g + gn_b
    bonus = (jnp.sum(r * kmod * r_k, axis=-1, keepdims=True) * v).reshape(B, T, RW_WIDTH)
    out = (y + bonus) * g
    return out.astype(p.dtype), s_last, p[:, -1]


def pool_mix(p, hist, start_pos, pool_w, pool_scale):
    B, T, _ = p.shape
    z = jnp.concatenate([hist.astype(jnp.float32), p.astype(jnp.float32)], axis=1)
    cs = jnp.concatenate([jnp.zeros((B, 1, POOL_WIDTH), jnp.float32), jnp.cumsum(z, axis=1)], axis=1)
    pos = start_pos + jnp.arange(T)
    outs = []
    for gi, w in enumerate(POOL_WINDOWS):
        sl = slice(gi * POOL_GROUP_DIM, (gi + 1) * POOL_GROUP_DIM)
        s = cs[:, POOL_HIST + 1:POOL_HIST + 1 + T, sl] - cs[:, POOL_HIST + 1 - w:POOL_HIST + 1 - w + T, sl]
        cnt = jnp.minimum(pos + 1, w).astype(jnp.float32)
        outs.append(s / cnt[None, :, None] - z[:, POOL_HIST:, sl])
    d = jnp.stack(outs, axis=2)
    y = jnp.einsum('btgc,gcd->btgd', d, pool_w).reshape(B, T, POOL_WIDTH) * pool_scale
    return y.astype(p.dtype), z[:, -POOL_HIST:].astype(p.dtype)


def nsa_mix(q, kv_full, kv_win, gates, q_start, w_buf_len, cmp_wk, cmp_wv):
    B, Tq = q.shape[0], q.shape[1]
    G, R, HD = NSA_KV_HEADS, NSA_GROUP, NSA_HEAD_DIM
    Tk = kv_full.shape[1]
    tk_pad = -(-Tk // SEL_BLOCK) * SEL_BLOCK
    n_cmp = tk_pad // CMP_BLOCK
    n_sel = tk_pad // SEL_BLOCK
    k_top = min(TOP_N, n_sel)
    kvp = jnp.pad(kv_full, ((0, 0), (0, tk_pad - Tk), (0, 0), (0, 0), (0, 0)))
    cmp_rows = kvp[:, :, :2].reshape(B, n_cmp, CMP_BLOCK, 2, G, HD)
    k_cmp = jnp.einsum('bnjgd,jd->bngd', cmp_rows[:, :, :, 0], cmp_wk)
    v_cmp = jnp.einsum('bnjgd,jd->bngd', cmp_rows[:, :, :, 1], cmp_wv).astype(jnp.float32)
    cmp_end = jnp.arange(n_cmp) * CMP_BLOCK + (CMP_BLOCK - 1)
    sel_rows = kvp[:, :, 2:].reshape(B, n_sel, SEL_BLOCK, 2, G, HD)
    k_blocks = jnp.transpose(sel_rows[:, :, :, 0], (0, 3, 1, 2, 4))
    v_blocks = jnp.transpose(sel_rows[:, :, :, 1], (0, 3, 1, 2, 4))
    win = jnp.pad(kv_win, ((0, 0), (WINDOW, 0), (0, 0), (0, 0), (0, 0)))
    qb = min(Tq, Q_BLOCK)
    n_qb = Tq // qb
    slopes = alibi_slopes(NSA_HEADS).reshape(G, R)[None, :, :, None, None]
    scale = NSA_HEAD_DIM ** -0.5
    gather = jax.vmap(jax.vmap(lambda blocks, ids: blocks[ids]))
    blk_id = jnp.arange(n_sel)

    def block(i):
        q0 = i * qb
        qblk = lax.dynamic_slice_in_dim(q, q0, qb, axis=1).reshape(B, qb, G, R, HD)
        gblk = jax.nn.sigmoid(lax.dynamic_slice_in_dim(gates, q0, qb, axis=1).astype(jnp.float32)).reshape(B, qb, G, R, 3)
        t = q_start + q0 + jnp.arange(qb)
        s_c = jnp.einsum('bqgrd,bngd->bgrqn', qblk, k_cmp).astype(jnp.float32) * scale
        dist_c = t[:, None] - cmp_end[None, :]
        s_c = s_c - slopes * dist_c.astype(jnp.float32)
        p_c = masked_softmax(s_c, dist_c >= 0)
        o_c = jnp.einsum('bgrqn,bngd->bqgrd', p_c, v_cmp)
        imp = jnp.sum(p_c, axis=2).reshape(B, G, qb, n_sel, SEL_BLOCK // CMP_BLOCK).sum(-1)
        cur = t // SEL_BLOCK
        valid = blk_id[None, :] * SEL_BLOCK <= t[:, None]
        forced = (blk_id[None, :] == 0) | (blk_id[None, :] == cur[:, None]) | (blk_id[None, :] == cur[:, None] - 1)
        score = jnp.where(valid, imp + FORCE_BONUS * forced.astype(jnp.float32), NEG_INF)
        _, idx = lax.top_k(score, k_top)
        k_g = gather(k_blocks, idx).reshape(B, G, qb, k_top * SEL_BLOCK, HD)
        v_g = gather(v_blocks, idx).reshape(B, G, qb, k_top * SEL_BLOCK, HD).astype(jnp.float32)
        pos_s = (idx[..., None] * SEL_BLOCK + jnp.arange(SEL_BLOCK)).reshape(B, G, qb, k_top * SEL_BLOCK)
        dist_s = (t[None, None, :, None] - pos_s)[:, :, None]
        s_s = jnp.einsum('bqgrd,bgqnd->bgrqn', qblk, k_g).astype(jnp.float32) * scale
        s_s = s_s - slopes * dist_s.astype(jnp.float32)
        p_s = masked_softmax(s_s, dist_s >= 0)
        o_s = jnp.einsum('bgrqn,bgqnd->bqgrd', p_s, v_g)
        wblk = lax.dynamic_slice_in_dim(win, w_buf_len + q0, WINDOW + qb, axis=1)
        pos_w = q_start + q0 - WINDOW + jnp.arange(WINDOW + qb)
        dist_w = t[:, None] - pos_w[None, :]
        s_w = jnp.einsum('bqgrd,bkgd->bgrqk', qblk, wblk[:, :, 0]).astype(jnp.float32) * scale
        s_w = s_w - slopes * dist_w.astype(jnp.float32)
        m_w = (pos_w[None, :] >= 0) & (dist_w >= 0) & (dist_w <= WINDOW)
        p_w = masked_softmax(s_w, m_w)
        o_w = jnp.einsum('bgrqk,bkgd->bqgrd', p_w, wblk[:, :, 1].astype(jnp.float32))
        o = gblk[..., 0:1] * o_c + gblk[..., 1:2] * o_s + gblk[..., 2:3] * o_w
        return o.reshape(B, qb, NSA_WIDTH)

    out = lax.map(block, jnp.arange(n_qb))
    return jnp.moveaxis(out, 0, 1).reshape(B, Tq, NSA_WIDTH).astype(q.dtype)


def grouped_experts(xf, eid, wt, w_gate, w_up, w_down):
    n_tok, d = xf.shape
    n_assign = n_tok * TOP_K
    blk = MOE_BLOCK if n_assign >= N_EXPERTS * MOE_BLOCK else 8
    n_blk = -(-(n_assign + N_EXPERTS * (blk - 1)) // blk)
    e = eid.reshape(-1)
    tok = jnp.repeat(jnp.arange(n_tok, dtype=jnp.int32), TOP_K)
    w = wt.reshape(-1)
    order = jnp.argsort(e)
    e_s, tok_s, w_s = e[order], tok[order], w[order]
    counts = jnp.bincount(e, length=N_EXPERTS)
    padded = (counts + blk - 1) // blk * blk
    pad_end = jnp.cumsum(padded)
    pad_start = pad_end - padded
    start = jnp.cumsum(counts) - counts
    dest = pad_start[e_s] + jnp.arange(n_assign) - start[e_s]
    tok_buf = jnp.zeros((n_blk * blk,), jnp.int32).at[dest].set(tok_s)
    w_buf = jnp.zeros((n_blk * blk,), w.dtype).at[dest].set(w_s)
    blk_e = jnp.minimum(jnp.searchsorted(pad_end, jnp.arange(n_blk) * blk, side='right'), N_EXPERTS - 1)
    xb = xf[tok_buf].reshape(n_blk, blk, d)

    def expert_block(args):
        xj, ej = args
        h = jax.nn.silu(xj @ w_gate[ej]) * (xj @ w_up[ej])
        return h @ w_down[ej]

    yb = lax.map(expert_block, (xb, blk_e)).reshape(n_blk * blk, d)
    return jnp.zeros((n_tok, d), yb.dtype).at[tok_buf].add(yb * w_buf[:, None].astype(yb.dtype))


def hmoe_ffn(u, router_g, bias_g, router_e, bias_e, w_gate, w_up, w_down):
    B, T, D = u.shape
    xf = u.reshape(B * T, D)
    lg = (xf @ router_g).astype(jnp.float32) + bias_g
    gsel = jnp.argmax(lg, axis=-1)
    wg = jnp.take_along_axis(jax.nn.softmax(lg, axis=-1), gsel[:, None], axis=-1)
    le = ((xf @ router_e).astype(jnp.float32) + bias_e).reshape(B * T, MOE_GROUPS, EXPERTS_PER_GROUP)
    le = jnp.take_along_axis(le, gsel[:, None, None], axis=1)[:, 0]
    top_v, top_i = lax.top_k(le, TOP_K)
    eid = (gsel[:, None] * EXPERTS_PER_GROUP + top_i).astype(jnp.int32)
    wt = wg * jax.nn.softmax(top_v, axis=-1)
    return grouped_experts(xf, eid, wt, w_gate, w_up, w_down).reshape(B, T, D).astype(u.dtype)


def trunk_layer(x, c, lw, past):
    (ada_w, ada_b, w_in, rw_mu, rw_w0, rw_w_up, rw_a0, rw_a_up, rw_g_up, rw_k_k, rw_k_a, rw_r_k,
     rw_gn_g, rw_gn_b, nsa_cmp_wk, nsa_cmp_wv, pool_w, pool_scale, w_branch, w_out, ln1_g, ln1_b,
     moe_router_g, moe_bias_g, moe_router_e, moe_bias_e, moe_w_gate, moe_w_up, moe_w_down, ln2_g, ln2_b) = lw
    nsa_past, win_buf, rw_state, rw_prev, pool_hist, q_start = past
    B, T, _ = x.shape
    mod = jax.nn.silu(c) @ ada_w + ada_b
    sh1, sc1, g1, sh2, sc2, g2 = jnp.split(mod[:, None, :], 6, axis=-1)
    u = x * (1.0 + sc1) + sh1
    proj = u @ w_in
    p_rw, p_nsa, p_pool, p_gate = jnp.split(proj, (RW_END, NSA_END, POOL_END), axis=-1)
    o_rw, rw_state_new, rw_prev_new = rwkv7_time_mix(p_rw, rw_prev, rw_state, rw_mu, rw_w0, rw_w_up, rw_a0, rw_a_up,
                                                     rw_g_up, rw_k_k, rw_k_a, rw_r_k, rw_gn_g, rw_gn_b)
    q, kv_new, nsa_g = jnp.split(p_nsa, (NSA_WIDTH, NSA_WIDTH + 6 * NSA_KV_WIDTH), axis=-1)
    kv_new = kv_new.reshape(B, T, 6, NSA_KV_HEADS, NSA_HEAD_DIM)
    kv_full = jnp.concatenate([nsa_past.astype(kv_new.dtype), kv_new[:, :, :4]], axis=1)
    kv_win = jnp.concatenate([win_buf.astype(kv_new.dtype), kv_new[:, :, 4:]], axis=1)
    o_nsa = nsa_mix(q.reshape(B, T, NSA_HEADS, NSA_HEAD_DIM), kv_full, kv_win, nsa_g.reshape(B, T, NSA_HEADS, 3),
                    q_start, win_buf.shape[1], nsa_cmp_wk, nsa_cmp_wv)
    o_pool, pool_hist_new = pool_mix(p_pool, pool_hist, q_start, pool_w, pool_scale)
    branches = jnp.stack([o_rw, o_nsa, o_pool], axis=2)
    up = jnp.einsum('btnc,ncd->btnd', branches, w_branch)
    gate = jax.nn.sigmoid(p_gate.reshape(B, T, N_BRANCH, D_MODEL))
    mixed = jnp.sum(gate * up, axis=2) @ w_out
    x = layer_norm(DEEPNORM_ALPHA * x + (1.0 + g1) * mixed, ln1_g, ln1_b)
    u2 = x * (1.0 + sc2) + sh2
    y = hmoe_ffn(u2, moe_router_g, moe_bias_g, moe_router_e, moe_bias_e, moe_w_gate, moe_w_up, moe_w_down)
    x = layer_norm(DEEPNORM_ALPHA * x + (1.0 + g2) * y, ln2_g, ln2_b)
    keep = min(WINDOW, kv_win.shape[1])
    return x, (kv_new[:, :, :4], kv_win[:, kv_win.shape[1] - keep:], rw_state_new, rw_prev_new, pool_hist_new)


def setup_inputs(seed: int = 0) -> dict:
    key = jax.random.key(seed)
    ks = iter(jax.random.split(key, 64))

    def nrm(shape, scale=1.0):
        return jax.random.normal(next(ks), shape, jnp.float32) * scale

    def uni(shape, lo, hi):
        return jax.random.uniform(next(ks), shape, jnp.float32, lo, hi)

    L, D = DEPTH, D_MODEL
    beta = DEEPNORM_BETA
    n_pages = PAST_LEN // PAGE_SIZE
    n_pool = (DEC_BATCH * n_pages * 5) // 4
    win_len = min(WINDOW, PAST_LEN)
    x_prompt = nrm((BATCH, SEQ, D))
    x_sample = nrm((DEC_BATCH, DEC_SEQ, D))
    cache_nsa_kv = nrm((L, n_pool, PAGE_SIZE, 4, NSA_KV_HEADS, NSA_HEAD_DIM))
    cache_win_kv = nrm((L, DEC_BATCH, win_len, 2, NSA_KV_HEADS, NSA_HEAD_DIM))
    state_rwkv = nrm((L, DEC_BATCH, RW_HEADS, RW_HEAD_DIM, RW_HEAD_DIM), 0.5)
    state_rwkv_shift = nrm((L, DEC_BATCH, RW_COLS))
    state_pool = nrm((L, DEC_BATCH, POOL_HIST, POOL_WIDTH))
    page_table = jax.random.permutation(next(ks), n_pool)[:DEC_BATCH * n_pages].reshape(DEC_BATCH, n_pages).astype(jnp.int32)
    return {
        'x_prompt': x_prompt,
        'x_sample': x_sample,
        'cache_nsa_kv': cache_nsa_kv,
        'cache_win_kv': cache_win_kv,
        'state_rwkv': state_rwkv,
        'state_rwkv_shift': state_rwkv_shift,
        'state_pool': state_pool,
        'page_table': page_table,
        'c_prompt': nrm((BATCH, D)),
        'c_sample': nrm((DEC_BATCH, D)),
        'ada_w': nrm((L, D, 6 * D), 0.1 * D ** -0.5),
        'ada_b': nrm((L, 6 * D), 0.01),
        'w_in': nrm((L, D, IN_COLS), D ** -0.5),
        'rw_mu': uni((L, RW_COLS), 0.0, 1.0),
        'rw_w0': uni((L, RW_WIDTH), -6.0, -1.0),
        'rw_w_up': nrm((L, RW_DECAY_RANK, RW_WIDTH), 0.5 * RW_DECAY_RANK ** -0.5),
        'rw_a0': nrm((L, RW_WIDTH), 0.1),
        'rw_a_up': nrm((L, RW_ICLR_RANK, RW_WIDTH), 0.5 * RW_ICLR_RANK ** -0.5),
        'rw_g_up': nrm((L, RW_GATE_RANK, RW_WIDTH), RW_GATE_RANK ** -0.5),
        'rw_k_k': 0.85 + nrm((L, RW_WIDTH), 0.05),
        'rw_k_a': 1.0 + nrm((L, RW_WIDTH), 0.05),
        'rw_r_k': nrm((L, RW_HEADS, RW_HEAD_DIM), 0.1),
        'rw_gn_g': 1.0 + nrm((L, RW_WIDTH), 0.1),
        'rw_gn_b': nrm((L, RW_WIDTH), 0.01),
        'nsa_cmp_wk': (1.0 + nrm((L, CMP_BLOCK, NSA_HEAD_DIM), 0.1)) / CMP_BLOCK,
        'nsa_cmp_wv': (1.0 + nrm((L, CMP_BLOCK, NSA_HEAD_DIM), 0.1)) / CMP_BLOCK,
        'pool_w': nrm((L, POOL_GROUPS, POOL_GROUP_DIM, POOL_GROUP_DIM), POOL_GROUP_DIM ** -0.5),
        'pool_scale': 1.0 + nrm((L, POOL_WIDTH), 0.1),
        'w_branch': nrm((L, N_BRANCH, BRANCH_WIDTH, D), beta * BRANCH_WIDTH ** -0.5),
        'w_out': nrm((L, D, D), beta * D ** -0.5),
        'ln1_g': 1.0 + nrm((L, D), 0.1),
        'ln1_b': nrm((L, D), 0.01),
        'moe_router_g': nrm((L, D, MOE_GROUPS), D ** -0.5),
        'moe_bias_g': nrm((L, MOE_GROUPS), 0.01),
        'moe_router_e': nrm((L, D, N_EXPERTS), D ** -0.5),
        'moe_bias_e': nrm((L, N_EXPERTS), 0.01),
        'moe_w_gate': nrm((L, N_EXPERTS, D, EXPERT_HIDDEN), D ** -0.5),
        'moe_w_up': nrm((L, N_EXPERTS, D, EXPERT_HIDDEN), D ** -0.5),
        'moe_w_down': nrm((L, N_EXPERTS, EXPERT_HIDDEN, D), beta * EXPERT_HIDDEN ** -0.5),
        'ln2_g': 1.0 + nrm((L, D), 0.1),
        'ln2_b': nrm((L, D), 0.01),
    }


def reference(x_prompt, x_sample, cache_nsa_kv, cache_win_kv, state_rwkv, state_rwkv_shift, state_pool, page_table,
              c_prompt, c_sample, ada_w, ada_b, w_in, rw_mu, rw_w0, rw_w_up, rw_a0, rw_a_up, rw_g_up, rw_k_k, rw_k_a,
              rw_r_k, rw_gn_g, rw_gn_b, nsa_cmp_wk, nsa_cmp_wv, pool_w, pool_scale, w_branch, w_out, ln1_g, ln1_b,
              moe_router_g, moe_bias_g, moe_router_e, moe_bias_e, moe_w_gate, moe_w_up, moe_w_down, ln2_g, ln2_b):
    n_pages = page_table.shape[1]
    past_len = n_pages * PAGE_SIZE
    bp, dt = x_prompt.shape[0], x_prompt.dtype
    bs = x_sample.shape[0]
    yp, ys = x_prompt, x_sample
    new_p, new_s = [], []
    for l in range(DEPTH):
        lw = tuple(w[l] for w in (ada_w, ada_b, w_in, rw_mu, rw_w0, rw_w_up, rw_a0, rw_a_up, rw_g_up, rw_k_k, rw_k_a,
                                  rw_r_k, rw_gn_g, rw_gn_b, nsa_cmp_wk, nsa_cmp_wv, pool_w, pool_scale, w_branch, w_out,
                                  ln1_g, ln1_b, moe_router_g, moe_bias_g, moe_router_e, moe_bias_e, moe_w_gate, moe_w_up,
                                  moe_w_down, ln2_g, ln2_b))
        past_p = (jnp.zeros((bp, 0, 4, NSA_KV_HEADS, NSA_HEAD_DIM), dt),
                  jnp.zeros((bp, 0, 2, NSA_KV_HEADS, NSA_HEAD_DIM), dt),
                  jnp.zeros((bp, RW_HEADS, RW_HEAD_DIM, RW_HEAD_DIM), jnp.float32),
                  jnp.zeros((bp, RW_COLS), dt),
                  jnp.zeros((bp, POOL_HIST, POOL_WIDTH), dt),
                  0)
        yp, st_p = trunk_layer(yp, c_prompt, lw, past_p)
        nsa_past = cache_nsa_kv[l][page_table].reshape(bs, past_len, 4, NSA_KV_HEADS, NSA_HEAD_DIM)
        past_s = (nsa_past, cache_win_kv[l], state_rwkv[l], state_rwkv_shift[l], state_pool[l], past_len)
        ys, st_s = trunk_layer(ys, c_sample, lw, past_s)
        new_p.append(st_p)
        new_s.append(st_s)
    return (yp, ys,
            jnp.stack([s[0] for s in new_p]), jnp.stack([s[0] for s in new_s]),
            jnp.stack([s[1] for s in new_p]), jnp.stack([s[1] for s in new_s]),
            jnp.stack([s[2] for s in new_p]), jnp.stack([s[2] for s in new_s]),
            jnp.stack([s[3] for s in new_p]), jnp.stack([s[3] for s in new_s]),
            jnp.stack([s[4] for s in new_p]), jnp.stack([s[4] for s in new_s]))
```

```python
import functools

import jax
import jax.numpy as jnp
from jax import lax
from jax.experimental import pallas as pl
from jax.experimental.pallas import tpu as pltpu

F32 = jnp.float32
BF16 = jnp.bfloat16

D_MODEL = 1024
RW_HEADS = 8
RW_HEAD_DIM = 64
RW_WIDTH = 512
RW_COLS = 1792
RW_GN_EPS = 64e-5
RW_CHUNK = 64

NSA_HEADS = 8
NSA_KV_HEADS = 2
NSA_HEAD_DIM = 64
NSA_GROUP = 4
CMP_BLOCK = 32
SEL_BLOCK = 64
TOP_N = 8
WINDOW = 512
NSA_TQ = 128
NEG_INF = -1e30
BELOW_NEG_INF = -3e38
FORCE_BONUS = 1e9
NSA_SLOPES = tuple(2.0 ** (-8.0 * (h + 1) / NSA_HEADS) for h in range(NSA_HEADS))

POOL_WINDOWS = (2, 4, 8, 16)
POOL_HIST = 15
POOL_WIDTH = 512

RW_END = 1792
NSA_Q_END = RW_END + 512
NSA_KV_END = NSA_Q_END + 768
NSA_END = NSA_KV_END + 24
POOL_END = NSA_END + 512

MOE_GROUPS = 4
EXPERTS_PER_GROUP = 4
EXPERT_HIDDEN = 256
ROUTE_LANE0 = 4

LN_EPS = 1e-5
LANES = 128
HALF = 64


def _cparams(sem, vmem_mb=48):
    return pltpu.CompilerParams(dimension_semantics=sem, vmem_limit_bytes=vmem_mb * 1024 * 1024)


def _dot(a, b):
    return jnp.dot(a, b, preferred_element_type=F32)


def _dot_nt(a, b):
    return lax.dot_general(a, b, (((1,), (1,)), ((), ())), preferred_element_type=F32)


def _dot_tn(a, b):
    return lax.dot_general(a, b, (((0,), (0,)), ((), ())), preferred_element_type=F32)


def _sigmoid(x):
    return 1.0 / (1.0 + jnp.exp(-x))


def _softplus(x):
    return jnp.maximum(x, 0.0) + jnp.log(1.0 + jnp.exp(-jnp.abs(x)))


def _layer_norm(h, g, b):
    mu = jnp.mean(h, axis=-1, keepdims=True)
    d = h - mu
    var = jnp.mean(d * d, axis=-1, keepdims=True)
    return d * lax.rsqrt(var + LN_EPS) * g + b


def _half_sum(x, lo):
    s_lo = jnp.sum(jnp.where(lo, x, 0.0), axis=1, keepdims=True)
    s_hi = jnp.sum(jnp.where(lo, 0.0, x), axis=1, keepdims=True)
    return jnp.where(lo, s_lo, s_hi)


def _ada_kernel(c_ref, w_ref, b_ref, o_ref):
    c = c_ref[...]
    s = (c * _sigmoid(c)).astype(BF16)
    o_ref[...] = _dot(s, w_ref[...].astype(BF16)) + b_ref[...]


def _ada_mod(c_all, ada_w, ada_b):
    L, D, D6 = ada_w.shape
    nb = c_all.shape[0]
    return pl.pallas_call(
        _ada_kernel,
        grid=(L, D6 // D),
        in_specs=[pl.BlockSpec((nb, D), lambda l, j: (0, 0)),
                  pl.BlockSpec((None, D, D), lambda l, j: (l, 0, j)),
                  pl.BlockSpec((None, 1, D), lambda l, j: (l, 0, j))],
        out_specs=pl.BlockSpec((None, nb, D), lambda l, j: (l, 0, j)),
        out_shape=jax.ShapeDtypeStruct((L, nb, D6), F32),
        compiler_params=_cparams(("parallel", "parallel")),
        name="ada_mod",
    )(c_all, ada_w, ada_b.reshape(L, 1, D6))


def _mod_spec(mod, rows_per_group, tm):
    r = mod.shape[1]
    return pl.BlockSpec((None, r, mod.shape[2]), lambda i: ((i * tm) // rows_per_group, 0, 0))


def _proj_kernel(x_ref, sc_ref, sh_ref, w_ref, o_ref):
    u = (x_ref[...] * (1.0 + sc_ref[...]) + sh_ref[...]).astype(BF16)
    o_ref[...] = _dot(u, w_ref[...])


def _proj(x, sc, sh, w, rows_per_group, tm):
    n, d = x.shape
    nc = w.shape[1]
    return pl.pallas_call(
        _proj_kernel,
        grid=(n // tm,),
        in_specs=[pl.BlockSpec((tm, d), lambda i: (i, 0)),
                  _mod_spec(sc, rows_per_group, tm), _mod_spec(sh, rows_per_group, tm),
                  pl.BlockSpec((d, nc), lambda i: (0, 0))],
        out_specs=pl.BlockSpec((tm, nc), lambda i: (i, 0)),
        out_shape=jax.ShapeDtypeStruct((n, nc), F32),
        compiler_params=_cparams(("parallel",)),
        name="proj",
    )(x, sc, sh, w)


def _proj_nsa_kernel(x_ref, sc_ref, sh_ref, w_ref, wk_ref, wv_ref, q_ref, kv_ref, gate_ref, kvt_ref=None, cmp_ref=None):
    u = (x_ref[...] * (1.0 + sc_ref[...]) + sh_ref[...]).astype(BF16)
    res = _dot(u, w_ref[...])
    q_ref[...] = res[:, :1024].astype(BF16)
    kv = res[:, 1024:1792]
    kv_ref[...] = kv
    gate_ref[...] = _sigmoid(res[:, 1792:1920])
    if kvt_ref is None:
        return
    for j in range(6):
        kvt_ref[j] = kv[:, j * LANES:(j + 1) * LANES].astype(BF16)
    nb = kv.shape[0] // CMP_BLOCK
    kc = jnp.sum(kv[:, 0:LANES].reshape(nb, CMP_BLOCK, LANES) * wk_ref[...][None], axis=1)
    vc = jnp.sum(kv[:, LANES:2 * LANES].reshape(nb, CMP_BLOCK, LANES) * wv_ref[...][None], axis=1)
    cmp_ref[:, 0:LANES] = kc
    cmp_ref[:, LANES:2 * LANES] = vc


def _proj_nsa(x, sc, sh, w, wk, wv, n_seq, t, tm):
    n, d = x.shape
    out_specs = [pl.BlockSpec((tm, 1024), lambda i: (i, 0)),
                 pl.BlockSpec((tm, 768), lambda i: (i, 0)),
                 pl.BlockSpec((tm, LANES), lambda i: (i, 0))]
    out_shape = [jax.ShapeDtypeStruct((n, 1024), BF16),
                 jax.ShapeDtypeStruct((n, 768), F32),
                 jax.ShapeDtypeStruct((n, LANES), F32)]
    if t > 1:
        tps = t // tm
        nb = tm // CMP_BLOCK
        out_specs += [pl.BlockSpec((None, 6, tm, LANES), lambda i: (i // tps, 0, i % tps, 0)),
                      pl.BlockSpec((nb, 2 * LANES), lambda i: (i, 0))]
        out_shape += [jax.ShapeDtypeStruct((n_seq, 6, t, LANES), BF16),
                      jax.ShapeDtypeStruct((n // CMP_BLOCK, 2 * LANES), F32)]
    rows_per_group = t if sc.shape[1] == 1 else n
    return pl.pallas_call(
        _proj_nsa_kernel,
        grid=(n // tm,),
        in_specs=[pl.BlockSpec((tm, d), lambda i: (i, 0)),
                  _mod_spec(sc, rows_per_group, tm), _mod_spec(sh, rows_per_group, tm),
                  pl.BlockSpec((d, 1920), lambda i: (0, 0)),
                  pl.BlockSpec((CMP_BLOCK, LANES), lambda i: (0, 0)),
                  pl.BlockSpec((CMP_BLOCK, LANES), lambda i: (0, 0))],
        out_specs=out_specs,
        out_shape=out_shape,
        compiler_params=_cparams(("parallel",)),
        name="proj_nsa",
    )(x, sc, sh, w, wk, wv)


def _rwkv_prep_body(p, prev, mu_ref, w0_ref, a0_ref, wwa_ref, gup_ref, kkw_ref, kaw_ref, rk_ref, outs):
    r_ref, lw_ref, k_ref, v_ref, kk_ref, kka_ref, g_ref, bonus_ref = outs
    xs = p + (prev - p) * mu_ref[...]
    r = xs[:, 0:512]
    xk = xs[:, 512:1024]
    v = xs[:, 1024:1536]
    t12 = xs[:, 1536:1664]
    gd = xs[:, 1664:1792]
    lane = lax.broadcasted_iota(jnp.int32, (1, LANES), 1)
    lo = lane < HALF
    z = jnp.where(lo, jnp.tanh(t12), t12).astype(BF16)
    dwa = _dot(z, wwa_ref[...])
    w_log = -_softplus(-(w0_ref[...] + dwa[:, :512])) - 0.5
    a = _sigmoid(a0_ref[...] + dwa[:, 512:])
    g_ref[...] = _dot(_sigmoid(gd).astype(BF16), gup_ref[...])
    kmod = xk * (1.0 + (a - 1.0) * kaw_ref[...])
    kkr = xk * kkw_ref[...]
    rkr = r * kmod * rk_ref[...]
    for m in range(4):
        sl = slice(m * LANES, (m + 1) * LANES)
        x = kkr[:, sl]
        nrm = jnp.sqrt(_half_sum(x * x, lo))
        kk = x / jnp.maximum(nrm, 1e-12)
        kk_ref[:, sl] = kk
        kka_ref[:, sl] = kk * a[:, sl]
        bonus_ref[:, sl] = _half_sum(rkr[:, sl], lo) * v[:, sl]
    r_ref[...] = r
    lw_ref[...] = -jnp.exp(w_log)
    k_ref[...] = kmod
    v_ref[...] = v


def _rwkv_prep_seq_kernel(p_ref, pprev_ref, prow_ref, *rest, seq_tiles):
    i = pl.program_id(0)
    p = p_ref[...]
    first = (i % seq_tiles) == 0
    prev_row = jnp.where(first, prow_ref[...], pprev_ref[7:8, :])
    rolled = pltpu.roll(p, 1, 0)
    rowid = lax.broadcasted_iota(jnp.int32, (p.shape[0], 1), 0)
    prev = jnp.where(rowid == 0, prev_row, rolled)
    _rwkv_prep_body(p, prev, *rest[:8], rest[8:])


def _rwkv_prep_step_kernel(p_ref, prev_ref, *rest):
    _rwkv_prep_body(p_ref[...], prev_ref[...], *rest[:8], rest[8:])


def _rwkv_prep(p_rw, prev_rows, wts, t, tm):
    n = p_rw.shape[0]
    vec = lambda c: pl.BlockSpec((1, c), lambda i: (0, 0))
    w_specs = [vec(RW_COLS), vec(512), vec(512), pl.BlockSpec((LANES, 1024), lambda i: (0, 0)),
               pl.BlockSpec((LANES, 512), lambda i: (0, 0)), vec(512), vec(512), vec(512)]
    out_specs = [pl.BlockSpec((tm, 512), lambda i: (i, 0))] * 8
    out_shape = [jax.ShapeDtypeStruct((n, 512), F32)] * 8
    if t == 1:
        kern = _rwkv_prep_step_kernel
        in_specs = [pl.BlockSpec((tm, RW_COLS), lambda i: (i, 0)), pl.BlockSpec((tm, RW_COLS), lambda i: (i, 0))]
        args = (p_rw, prev_rows)
    else:
        tps = t // tm
        kern = functools.partial(_rwkv_prep_seq_kernel, seq_tiles=tps)
        in_specs = [pl.BlockSpec((tm, RW_COLS), lambda i: (i, 0)),
                    pl.BlockSpec((8, RW_COLS), lambda i: (jnp.maximum(i * (tm // 8) - 1, 0), 0)),
                    pl.BlockSpec((None, 1, RW_COLS), lambda i: (i // tps, 0, 0))]
        args = (p_rw, p_rw, prev_rows[:, None, :])
    return pl.pallas_call(
        kern, grid=(n // tm,), in_specs=in_specs + w_specs, out_specs=out_specs, out_shape=out_shape,
        compiler_params=_cparams(("parallel",)), name="rwkv_prep",
    )(*args, *wts)


def _split3(x):
    x1 = x.astype(BF16)
    r1 = x - x1.astype(F32)
    x2 = r1.astype(BF16)
    x3 = (r1 - x2.astype(F32)).astype(BF16)
    return x1, x2, x3


def _rwkv_chunk_kernel(r_ref, lw_ref, k_ref, v_ref, kk_ref, kka_ref, g_ref, bonus_ref, s0_ref, gng_ref, gnb_ref,
                       o_ref, s_ref):
    c = pl.program_id(1)
    C = RW_CHUNK

    @pl.when(c == 0)
    def _():
        s_ref[...] = s0_ref[...]

    lw = lw_ref[...]
    row = lax.broadcasted_iota(jnp.int32, (C, C), 0)
    col = lax.broadcasted_iota(jnp.int32, (C, C), 1)
    tri = (row >= col).astype(BF16)
    l1, l2, l3 = _split3(lw)
    cl = _dot(tri, l1) + _dot(tri, l2) + _dot(tri, l3)
    cl_last = cl[C - 1:C, :]
    g_in = jnp.exp(cl)
    g_ex = jnp.exp(cl - lw)
    g_inv = jnp.exp(-cl)
    g_rem = jnp.exp(cl_last - cl)
    gc = jnp.exp(cl_last)
    k = k_ref[...]
    kka = kka_ref[...]
    qk_all = kk_ref[...] * g_ex
    r_all = r_ref[...] * g_in
    kt_all = k * g_inv
    at_all = kka * g_inv
    kd_all = k * g_rem
    ad_all = kka * g_rem
    v_all = v_ref[...]

    lane = lax.broadcasted_iota(jnp.int32, (1, LANES), 1)
    lo = lane < HALF

    def st(x):
        return jnp.concatenate([jnp.where(lo, x, 0.0), jnp.where(lo, 0.0, x)], axis=0)

    r2 = lax.broadcasted_iota(jnp.int32, (4 * C, 4 * C), 0)
    c2 = lax.broadcasted_iota(jnp.int32, (4 * C, 4 * C), 1)
    rt = r2 % C
    ct = c2 % C
    tmask = (ct < rt) | ((r2 >= 2 * C) & (ct == rt))
    ri = lax.broadcasted_iota(jnp.int32, (2 * C, 2 * C), 0)
    ci = lax.broadcasted_iota(jnp.int32, (2 * C, 2 * C), 1)
    eye = (ri == ci).astype(F32)

    for m in range(4):
        sl = slice(m * LANES, (m + 1) * LANES)
        qk_st = st(qk_all[:, sl])
        r_st = st(r_all[:, sl])
        qr = jnp.concatenate([qk_st, r_st], axis=0).astype(BF16)
        ak = jnp.concatenate([st(at_all[:, sl]), st(kt_all[:, sl])], axis=0).astype(BF16)
        xm = jnp.where(tmask, _dot_nt(qr, ak), 0.0)
        v_st = st(v_all[:, sl])
        av = _dot(xm[:, 2 * C:].astype(BF16), v_st.astype(BF16))
        aak_v = av[:2 * C]
        ark_v = av[2 * C:]
        p = xm[:2 * C, :2 * C]
        tm_ = eye - p
        for _ in range(5):
            pb = p.astype(BF16)
            p = _dot(pb, pb)
            tm_ = _dot(tm_.astype(BF16), (eye + p).astype(BF16))
        wu = _dot(tm_.astype(BF16), jnp.concatenate([qk_st, aak_v], axis=1).astype(BF16))
        ry = _dot(xm[2 * C:, :2 * C].astype(BF16), wu.astype(BF16))
        wm = wu[:, :LANES]
        u = wu[:, LANES:]
        rq = r_st - ry[:, :LANES]
        y0 = ark_v - ry[:, LANES:]
        ad_st = st(ad_all[:, sl])
        kd_st = st(kd_all[:, sl])
        gt = eye * gc[:, sl] - _dot_tn(wm.astype(BF16), ad_st.astype(BF16))
        ht = _dot_tn(jnp.concatenate([v_st, u], axis=0).astype(BF16),
                     jnp.concatenate([kd_st, -ad_st], axis=0).astype(BF16))
        s = s_ref[m]
        sb = s.astype(BF16)
        y_st = _dot_nt(rq.astype(BF16), sb) + y0
        s_ref[m] = _dot(sb, gt.astype(BF16)) + ht
        y = y_st[:C] + y_st[C:]
        mu = _half_sum(y, lo) * (1.0 / RW_HEAD_DIM)
        d = y - mu
        var = _half_sum(d * d, lo) * (1.0 / RW_HEAD_DIM)
        yn = d * lax.rsqrt(var + RW_GN_EPS) * gng_ref[:, sl] + gnb_ref[:, sl]
        o_ref[:, sl] = ((yn + bonus_ref[:, sl]) * g_ref[:, sl]).astype(BF16)


def _rwkv_chunk(seqs, s0_bd, gn_g, gn_b, n_seq, t):
    C = RW_CHUNK
    nch = t // C
    row_spec = pl.BlockSpec((C, 512), lambda b, c: (b * nch + c, 0))
    st_spec = pl.BlockSpec((None, 4, LANES, LANES), lambda b, c: (b, 0, 0, 0))
    vec = pl.BlockSpec((1, 512), lambda b, c: (0, 0))
    return pl.pallas_call(
        _rwkv_chunk_kernel,
        grid=(n_seq, nch),
        in_specs=[row_spec] * 8 + [st_spec, vec, vec],
        out_specs=[row_spec, st_spec],
        out_shape=[jax.ShapeDtypeStruct((n_seq * t, 512), BF16),
                   jax.ShapeDtypeStruct((n_seq, 4, LANES, LANES), F32)],
        compiler_params=_cparams(("parallel", "arbitrary")),
        name="rwkv_chunk",
    )(*seqs, s0_bd, gn_g, gn_b)


def _state_to_bd(s):
    b = s.shape[0]
    s = s.reshape(b, 4, 2, 64, 64).astype(F32)
    bd = jnp.einsum('bmhij,hk->bmhikj', s, jnp.eye(2, dtype=F32))
    return bd.reshape(b, 4, LANES, LANES)


def _state_from_bd(bd):
    b = bd.shape[0]
    bd = bd.reshape(b, 4, 2, 64, 2, 64)
    return jnp.stack([bd[:, :, 0, :, 0, :], bd[:, :, 1, :, 1, :]], axis=2).reshape(b, 8, 64, 64)


def _nsa_prompt_kernel(q_ref, kvt_ref, cmp_ref, gate_ref, o_ref, m_ref, l_ref, acc_ref, *, k_top):
    qi = pl.program_id(1)
    tq = NSA_TQ
    q0 = qi * tq
    lane = lax.broadcasted_iota(jnp.int32, (1, LANES), 1)
    lo = lane < HALF
    tok = q0 + lax.broadcasted_iota(jnp.int32, (tq, 1), 0)
    cmpv = cmp_ref[...]
    kc = cmpv[:, :LANES].astype(BF16)
    vc = cmpv[:, LANES:].astype(BF16)
    gates = gate_ref[...]
    blk_c = 2 * (lane % HALF) + lane // HALF
    dist_c = tok - (blk_c * CMP_BLOCK + (CMP_BLOCK - 1))
    mask_c = dist_c >= 0
    dist_cf = dist_c.astype(F32)
    cur = tok // SEL_BLOCK
    valid = (lane * SEL_BLOCK <= tok) & lo
    forced = (lane == 0) | (lane == cur) | (lane == cur - 1)
    blk_row = lax.broadcasted_iota(jnp.int32, (LANES, LANES), 0)
    key_blk = lax.broadcasted_iota(jnp.int32, (LANES, LANES), 1) // SEL_BLOCK

    def attend(qg, g, k_idx, v_idx, j_lo, j_hi, mask_fn):
        m_ref[...] = jnp.full(m_ref.shape, NEG_INF, F32)
        l_ref[...] = jnp.zeros(l_ref.shape, F32)
        acc_ref[...] = jnp.zeros(acc_ref.shape, F32)

        def body(j, carry):
            start = pl.multiple_of(j * tq, tq)
            kt = kvt_ref[k_idx, pl.ds(start, tq), :]
            vt = kvt_ref[v_idx, pl.ds(start, tq), :]
            s = _dot_nt(qg, kt)
            dist = tok - (j * tq + lane)
            mask = mask_fn(j, dist)
            dist_f = dist.astype(F32)
            es = []
            for r in range(NSA_GROUP):
                rows = slice(r * tq, (r + 1) * tq)
                sr = jnp.where(mask, s[rows] - NSA_SLOPES[NSA_GROUP * g + r] * dist_f, NEG_INF)
                m_old = m_ref[rows]
                m_new = jnp.maximum(m_old, jnp.max(sr, axis=1, keepdims=True))
                alpha = jnp.exp(m_old - m_new)
                e = jnp.where(mask, jnp.exp(sr - m_new), 0.0)
                l_ref[rows] = alpha * l_ref[rows] + jnp.sum(e, axis=1, keepdims=True)
                acc_ref[rows] = alpha * acc_ref[rows]
                m_ref[rows] = m_new
                es.append(e.astype(BF16))
            acc_ref[...] += _dot(jnp.concatenate(es, axis=0), vt)
            return carry

        lax.fori_loop(j_lo, j_hi, body, 0)
        return acc_ref[...] / jnp.maximum(l_ref[...], 1e-30)

    for g in range(NSA_KV_HEADS):
        qg = jnp.concatenate([q_ref[:, (NSA_GROUP * g + r) * LANES:(NSA_GROUP * g + r + 1) * LANES]
                              for r in range(NSA_GROUP)], axis=0)
        s = _dot_nt(qg, kc)
        ps = []
        imp = jnp.zeros((tq, LANES), F32)
        for r in range(NSA_GROUP):
            sr = jnp.where(mask_c, s[r * tq:(r + 1) * tq] - NSA_SLOPES[NSA_GROUP * g + r] * dist_cf, NEG_INF)
            mx = jnp.max(sr, axis=1, keepdims=True)
            e = jnp.where(mask_c, jnp.exp(sr - mx), 0.0)
            p = e / jnp.maximum(jnp.sum(e, axis=1, keepdims=True), 1e-30)
            ps.append(p.astype(BF16))
            imp = imp + p
        o_c = _dot(jnp.concatenate(ps, axis=0), vc)
        imp_sel = imp + pltpu.roll(imp, HALF, 1)
        score = jnp.where(valid, imp_sel + jnp.where(forced, FORCE_BONUS, 0.0), NEG_INF)
        score = jnp.where(lo, score, BELOW_NEG_INF)
        sel = jnp.zeros((tq, LANES), F32)
        for _ in range(k_top):
            mx = jnp.max(score, axis=1, keepdims=True)
            idx = jnp.min(jnp.where(score == mx, lane, 4 * LANES), axis=1, keepdims=True)
            hit = lane == idx
            sel = jnp.where(hit, 1.0, sel)
            score = jnp.where(hit, BELOW_NEG_INF, score)
        sel_b = sel.astype(BF16)

        def sel_mask(j, dist):
            expand = (blk_row == 2 * j + key_blk).astype(BF16)
            return (_dot(sel_b, expand) > 0.5) & (dist >= 0)

        def win_mask(j, dist):
            return (dist >= 0) & (dist <= WINDOW)

        o_s = attend(qg, g, 2, 3, 0, qi + 1, sel_mask)
        o_w = attend(qg, g, 4, 5, jnp.maximum(qi - WINDOW // tq, 0), qi + 1, win_mask)
        in_group = lo if g == 0 else jnp.logical_not(lo)
        for r in range(NSA_GROUP):
            h = NSA_GROUP * g + r
            rows = slice(r * tq, (r + 1) * tq)
            o = (gates[:, 3 * h:3 * h + 1] * o_c[rows] + gates[:, 3 * h + 1:3 * h + 2] * o_s[rows]
                 + gates[:, 3 * h + 2:3 * h + 3] * o_w[rows])
            o_ref[:, h * LANES:(h + 1) * LANES] = jnp.where(in_group, o, 0.0).astype(BF16)


def _nsa_prompt(q_pad, kvt, cmp_perm, gates, n_seq, t):
    tq = NSA_TQ
    nq = t // tq
    k_top = min(TOP_N, t // SEL_BLOCK)
    return pl.pallas_call(
        functools.partial(_nsa_prompt_kernel, k_top=k_top),
        grid=(n_seq, nq),
        in_specs=[pl.BlockSpec((tq, 1024), lambda b, i: (b * nq + i, 0)),
                  pl.BlockSpec((None, 6, t, LANES), lambda b, i: (b, 0, 0, 0)),
                  pl.BlockSpec((None, LANES, 2 * LANES), lambda b, i: (b, 0, 0)),
                  pl.BlockSpec((tq, LANES), lambda b, i: (b * nq + i, 0))],
        out_specs=pl.BlockSpec((tq, 1024), lambda b, i: (b * nq + i, 0)),
        out_shape=jax.ShapeDtypeStruct((n_seq * t, 1024), BF16),
        scratch_shapes=[pltpu.VMEM((NSA_GROUP * tq, 1), F32), pltpu.VMEM((NSA_GROUP * tq, 1), F32),
                        pltpu.VMEM((NSA_GROUP * tq, LANES), F32)],
        compiler_params=_cparams(("parallel", "parallel")),
        name="nsa_prompt",
    )(q_pad, kvt, cmp_perm, gates)


def _permute_cmp(cmp, n_seq, t):
    nc = t // CMP_BLOCK
    c = cmp.reshape(n_seq, nc // 2, 2, 2 * LANES)
    c = jnp.pad(c, ((0, 0), (0, HALF - nc // 2), (0, 0), (0, 0)))
    return jnp.transpose(c, (0, 2, 1, 3)).reshape(n_seq, LANES, 2 * LANES)


def _compress_pool_kernel(x_ref, wk_ref, wv_ref, o_ref):
    x = x_ref[...]
    pb = x.shape[0]
    nb = x.shape[1] // CMP_BLOCK
    kc = jnp.sum(x[:, :, 0:LANES].reshape(pb, nb, CMP_BLOCK, LANES) * wk_ref[...][None, None], axis=2)
    vc = jnp.sum(x[:, :, LANES:2 * LANES].reshape(pb, nb, CMP_BLOCK, LANES) * wv_ref[...][None, None], axis=2)
    o_ref[:, 0:nb, :] = kc
    o_ref[:, nb:2 * nb, :] = vc


def _compress_pool(cache, wk, wv, pb):
    L, n_pool, page, _ = cache.shape
    nb = page // CMP_BLOCK
    return pl.pallas_call(
        _compress_pool_kernel,
        grid=(L, n_pool // pb),
        in_specs=[pl.BlockSpec((None, pb, page, 2 * LANES), lambda l, i: (l, i, 0, 0)),
                  pl.BlockSpec((None, CMP_BLOCK, LANES), lambda l, i: (l, 0, 0)),
                  pl.BlockSpec((None, CMP_BLOCK, LANES), lambda l, i: (l, 0, 0))],
        out_specs=pl.BlockSpec((None, pb, 2 * nb, LANES), lambda l, i: (l, i, 0, 0)),
        out_shape=jax.ShapeDtypeStruct((L, n_pool, 2 * nb, LANES), F32),
        compiler_params=_cparams(("parallel", "parallel")),
        name="compress_pool",
    )(cache, wk, wv)


def _nsa_dec_cmp_kernel(pt_ref, pool_ref, q_ref, gate_ref, slope_ref, win_ref, kvn_ref, part_ref, idx_ref, gath_ref,
                        *, n_pages, past_len, k_top):
    b = pl.program_id(0)
    t = past_len
    for p in range(n_pages):
        tile = pool_ref[pt_ref[b, p]]
        for r in range(8):
            gath_ref[r, p:p + 1, :] = tile[r:r + 1, :]
    x = [gath_ref[r] for r in range(8)]
    q = q_ref[...]
    slope = slope_ref[:, 0:1]
    gates = gate_ref[...]
    lane = lax.broadcasted_iota(jnp.int32, (1, LANES), 1)
    lane_p = lax.broadcasted_iota(jnp.int32, (1, 2 * n_pages), 1)
    page_of = lane_p % n_pages
    ss, masks = [], []
    for pair in range(2):
        kmat = jnp.concatenate([x[2 * pair], x[2 * pair + 1]], axis=0).astype(BF16)
        blk = page_of * 4 + 2 * pair + lane_p // n_pages
        dist = t - (blk * CMP_BLOCK + (CMP_BLOCK - 1))
        mask = dist >= 0
        s = _dot_nt(q, kmat) - slope * dist.astype(F32)
        ss.append(jnp.where(mask, s, NEG_INF))
        masks.append(mask)
    mx = jnp.maximum(jnp.max(ss[0], axis=1, keepdims=True), jnp.max(ss[1], axis=1, keepdims=True))
    es = [jnp.where(masks[i], jnp.exp(ss[i] - mx), 0.0) for i in range(2)]
    den = jnp.maximum(jnp.sum(es[0], axis=1, keepdims=True) + jnp.sum(es[1], axis=1, keepdims=True), 1e-30)
    ps = [e / den for e in es]
    o_c = jnp.zeros((NSA_HEADS, LANES), F32)
    for pair in range(2):
        vmat = jnp.concatenate([x[4 + 2 * pair], x[5 + 2 * pair]], axis=0).astype(BF16)
        o_c = o_c + _dot(ps[pair].astype(BF16), vmat)
    n_sel_blk = lane_p // n_pages + 2 * page_of
    cur = t // SEL_BLOCK
    forced = (n_sel_blk == 0) | (n_sel_blk == cur) | (n_sel_blk == cur - 1)
    valid = n_sel_blk * SEL_BLOCK <= t
    idx_rows = []
    for g in range(NSA_KV_HEADS):
        imp_a = jnp.sum(ps[0][NSA_GROUP * g:NSA_GROUP * (g + 1)], axis=0, keepdims=True)
        imp_b = jnp.sum(ps[1][NSA_GROUP * g:NSA_GROUP * (g + 1)], axis=0, keepdims=True)
        ev = imp_a + pltpu.roll(imp_a, n_pages, 1)
        od = imp_b + pltpu.roll(imp_b, n_pages, 1)
        imp = jnp.where(lane_p < n_pages, ev, od)
        score = jnp.where(valid, imp + jnp.where(forced, FORCE_BONUS, 0.0), NEG_INF)
        chosen = jnp.where(lane == 0, cur, 0)
        for it in range(1, k_top):
            mxs = jnp.max(score, axis=1, keepdims=True)
            pick = jnp.min(jnp.where(score == mxs, n_sel_blk, 1 << 30), axis=1, keepdims=True)
            score = jnp.where(n_sel_blk == pick, BELOW_NEG_INF, score)
            chosen = jnp.where(lane == it, pick, chosen)
        idx_rows.append(chosen)
    rowi = lax.broadcasted_iota(jnp.int32, (8, LANES), 0)
    idx_ref[...] = jnp.where(rowi == 0, idx_rows[0], jnp.where(rowi == 1, idx_rows[1], 0))
    w = win_ref[...]
    wl = w.shape[0]
    kw = w[:, :LANES].astype(BF16)
    vw = w[:, LANES:].astype(BF16)
    col = lax.broadcasted_iota(jnp.int32, (1, wl), 1)
    dist_w = wl - col
    mask_w = (t - dist_w >= 0) & (dist_w <= WINDOW)
    s_w = jnp.where(mask_w, _dot_nt(q, kw) - slope * dist_w.astype(F32), NEG_INF)
    kvn = kvn_ref[...]
    qf = q.astype(F32)
    s_n = jnp.sum(qf * kvn[4:5, :].astype(BF16).astype(F32), axis=1, keepdims=True)
    mw = jnp.maximum(jnp.max(s_w, axis=1, keepdims=True), s_n)
    e_w = jnp.where(mask_w, jnp.exp(s_w - mw), 0.0)
    e_n = jnp.exp(s_n - mw)
    den_w = jnp.maximum(jnp.sum(e_w, axis=1, keepdims=True) + e_n, 1e-30)
    o_w = _dot((e_w / den_w).astype(BF16), vw) + (e_n / den_w).astype(BF16).astype(F32) * kvn[5:6, :].astype(BF16).astype(F32)
    part_ref[...] = gates[:, 0:1] * o_c + gates[:, 2:3] * o_w


def _nsa_dec_sel_kernel(pt_ref, sel_ref, blk_ref, q_ref, gate_ref, slope_ref, kvn_ref, part_ref, o_ref,
                        m_ref, l_ref, acc_ref, *, past_len, k_top):
    b = pl.program_id(0)
    j = pl.program_id(1)
    g = j // k_top
    t = past_len

    @pl.when(j == 0)
    def _():
        m_ref[...] = jnp.full(m_ref.shape, NEG_INF, F32)
        l_ref[...] = jnp.zeros(l_ref.shape, F32)
        acc_ref[...] = jnp.zeros(acc_ref.shape, F32)

    n = sel_ref[b, j]
    q = q_ref[...]
    slope = slope_ref[:, 0:1]
    x = blk_ref[...]
    col = lax.broadcasted_iota(jnp.int32, (1, SEL_BLOCK), 1)
    pos = n * SEL_BLOCK + col
    mask = (pos < t) & (pos <= t)
    rowh = lax.broadcasted_iota(jnp.int32, (NSA_HEADS, 1), 0)
    in_g = (rowh // NSA_GROUP) == g
    s = jnp.where(mask, _dot_nt(q, x[:, :LANES].astype(BF16)) - slope * (t - pos).astype(F32), NEG_INF)
    is_cur = n * SEL_BLOCK + SEL_BLOCK > t
    kvn = kvn_ref[...]
    s_n = jnp.sum(q.astype(F32) * kvn[2:3, :].astype(BF16).astype(F32), axis=1, keepdims=True)
    s_n = jnp.where(is_cur, s_n, NEG_INF)
    m_old = m_ref[...]
    m_new = jnp.maximum(m_old, jnp.maximum(jnp.max(s, axis=1, keepdims=True), s_n))
    alpha = jnp.exp(m_old - m_new)
    e = jnp.where(mask, jnp.exp(s - m_new), 0.0)
    e_n = jnp.where(is_cur, jnp.exp(s_n - m_new), 0.0)
    l_new = alpha * l_ref[...] + jnp.sum(e, axis=1, keepdims=True) + e_n
    acc_new = (alpha * acc_ref[...] + _dot(e.astype(BF16), x[:, LANES:].astype(BF16))
               + e_n.astype(BF16).astype(F32) * kvn[3:4, :].astype(BF16).astype(F32))
    m_ref[...] = jnp.where(in_g, m_new, m_old)
    l_ref[...] = jnp.where(in_g, l_new, l_ref[...])
    acc_ref[...] = jnp.where(in_g, acc_new, acc_ref[...])

    @pl.when(j == pl.num_programs(1) - 1)
    def _():
        lane = lax.broadcasted_iota(jnp.int32, (1, LANES), 1)
        o_s = acc_ref[...] / jnp.maximum(l_ref[...], 1e-30)
        o = part_ref[...] + gate_ref[:, 1:2] * o_s
        in_half = (lane // HALF) == (rowh // NSA_GROUP)
        o_ref[...] = jnp.where(in_half, o, 0.0).astype(BF16)


def _nsa_decode(page_table, cmp_pool_l, cache_l, win_l, q_pad, gates, kv_new, slopes, past_len):
    bs, n_pages = page_table.shape
    n_pool, page = cache_l.shape[0], cache_l.shape[1]
    k_top = min(TOP_N, past_len // SEL_BLOCK + 1)
    q3 = q_pad.reshape(bs, NSA_HEADS, LANES)
    g3 = jnp.pad(gates[:, :3 * NSA_HEADS].reshape(bs, NSA_HEADS, 3), ((0, 0), (0, 0), (0, LANES - 3)))
    kvn = kv_new.reshape(bs, 6, LANES)
    wl = win_l.shape[1]
    head_spec = lambda *_: None
    part, idx = pl.pallas_call(
        functools.partial(_nsa_dec_cmp_kernel, n_pages=n_pages, past_len=past_len, k_top=k_top),
        grid_spec=pltpu.PrefetchScalarGridSpec(
            num_scalar_prefetch=1, grid=(bs,),
            in_specs=[pl.BlockSpec((n_pool, 8, LANES), lambda b, pt: (0, 0, 0)),
                      pl.BlockSpec((None, NSA_HEADS, LANES), lambda b, pt: (b, 0, 0)),
                      pl.BlockSpec((None, NSA_HEADS, LANES), lambda b, pt: (b, 0, 0)),
                      pl.BlockSpec((NSA_HEADS, LANES), lambda b, pt: (0, 0)),
                      pl.BlockSpec((None, wl, 2 * LANES), lambda b, pt: (b, 0, 0)),
                      pl.BlockSpec((None, 6, LANES), lambda b, pt: (b, 0, 0))],
            out_specs=[pl.BlockSpec((None, NSA_HEADS, LANES), lambda b, pt: (b, 0, 0)),
                       pl.BlockSpec((None, 8, LANES), lambda b, pt: (b, 0, 0))],
            scratch_shapes=[pltpu.VMEM((8, n_pages, LANES), F32)]),
        out_shape=[jax.ShapeDtypeStruct((bs, NSA_HEADS, LANES), F32),
                   jax.ShapeDtypeStruct((bs, 8, LANES), jnp.int32)],
        compiler_params=_cparams(("arbitrary",)),
        name="nsa_dec_cmp",
    )(page_table, cmp_pool_l, q3, g3, slopes, win_l, kvn)
    sel = idx[:, :NSA_KV_HEADS, :k_top].reshape(bs, NSA_KV_HEADS * k_top)
    halves = page // SEL_BLOCK
    cache_h = cache_l.reshape(n_pool * halves, SEL_BLOCK, 4 * LANES)

    def blk_map(b, j, pt, sl):
        n = sl[b, j]
        pg = pt[b, jnp.minimum(n // halves, n_pages - 1)]
        return (pg * halves + n % halves, 0, 1)

    out = pl.pallas_call(
        functools.partial(_nsa_dec_sel_kernel, past_len=past_len, k_top=k_top),
        grid_spec=pltpu.PrefetchScalarGridSpec(
            num_scalar_prefetch=2, grid=(bs, NSA_KV_HEADS * k_top),
            in_specs=[pl.BlockSpec((None, SEL_BLOCK, 2 * LANES), blk_map),
                      pl.BlockSpec((None, NSA_HEADS, LANES), lambda b, j, pt, sl: (b, 0, 0)),
                      pl.BlockSpec((None, NSA_HEADS, LANES), lambda b, j, pt, sl: (b, 0, 0)),
                      pl.BlockSpec((NSA_HEADS, LANES), lambda b, j, pt, sl: (0, 0)),
                      pl.BlockSpec((None, 6, LANES), lambda b, j, pt, sl: (b, 0, 0)),
                      pl.BlockSpec((None, NSA_HEADS, LANES), lambda b, j, pt, sl: (b, 0, 0))],
            out_specs=pl.BlockSpec((None, NSA_HEADS, LANES), lambda b, j, pt, sl: (b, 0, 0)),
            scratch_shapes=[pltpu.VMEM((NSA_HEADS, 1), F32), pltpu.VMEM((NSA_HEADS, 1), F32),
                            pltpu.VMEM((NSA_HEADS, LANES), F32)]),
        out_shape=jax.ShapeDtypeStruct((bs, NSA_HEADS, LANES), BF16),
        compiler_params=_cparams(("arbitrary", "arbitrary")),
        name="nsa_dec_sel",
    )(page_table, sel, cache_h, q3, g3, slopes, kvn, part)
    return out.reshape(bs, NSA_HEADS * LANES)


def _pool_tail(sums, cur, cnts, pw_ref, scale_ref, o_ref):
    for gi in range(len(POOL_WINDOWS)):
        sl = slice(gi * LANES, (gi + 1) * LANES)
        d = sums[gi] / cnts[gi] - cur[:, sl]
        y = _dot(d.astype(BF16), pw_ref[gi])
        o_ref[:, sl] = (y * scale_ref[:, sl]).astype(BF16)


def _pool_seq_kernel(p_ref, pprev_ref, hist_ref, pw_ref, scale_ref, o_ref, zz_ref, *, seq_tiles, tm, start_pos):
    i = pl.program_id(0)
    first = (i % seq_tiles) == 0
    cur = p_ref[...]
    zz_ref[0:16, :] = jnp.where(first, hist_ref[...], pprev_ref[...])
    zz_ref[16:16 + tm, :] = cur
    pos = start_pos + (i % seq_tiles) * tm + lax.broadcasted_iota(jnp.int32, (tm, 1), 0)
    sums, cnts = [], []
    for gi, w in enumerate(POOL_WINDOWS):
        sl = slice(gi * LANES, (gi + 1) * LANES)
        s = cur[:, sl]
        for k in range(1, w):
            s = s + zz_ref[16 - k:16 - k + tm, sl]
        sums.append(s)
        cnts.append(jnp.minimum(pos + 1, w).astype(F32))
    _pool_tail(sums, cur, cnts, pw_ref, scale_ref, o_ref)


def _pool_step_kernel(z_ref, pw_ref, scale_ref, o_ref, *, start_pos):
    z = z_ref[...]
    cur = z[:, 15, :]
    sums, cnts = [], []
    for gi, w in enumerate(POOL_WINDOWS):
        sl = slice(gi * LANES, (gi + 1) * LANES)
        sums.append(jnp.sum(z[:, 16 - w:16, sl], axis=1))
        cnts.append(float(min(start_pos + 1, w)))
    _pool_tail(sums, cur, cnts, pw_ref, scale_ref, o_ref)


def _pool_seq(p_pool, hist16, pool_w, pool_scale, t, tm, start_pos):
    n = p_pool.shape[0]
    tps = t // tm
    return pl.pallas_call(
        functools.partial(_pool_seq_kernel, seq_tiles=tps, tm=tm, start_pos=start_pos),
        grid=(n // tm,),
        in_specs=[pl.BlockSpec((tm, 512), lambda i: (i, 0)),
                  pl.BlockSpec((16, 512), lambda i: (jnp.maximum(i * (tm // 16) - 1, 0), 0)),
                  pl.BlockSpec((None, 16, 512), lambda i: (i // tps, 0, 0)),
                  pl.BlockSpec((4, LANES, LANES), lambda i: (0, 0, 0)),
                  pl.BlockSpec((1, 512), lambda i: (0, 0))],
        out_specs=pl.BlockSpec((tm, 512), lambda i: (i, 0)),
        out_shape=jax.ShapeDtypeStruct((n, 512), BF16),
        scratch_shapes=[pltpu.VMEM((tm + 16, 512), F32)],
        compiler_params=_cparams(("parallel",)),
        name="pool_seq",
    )(p_pool, p_pool, hist16, pool_w, pool_scale)


def _pool_step(z16, pool_w, pool_scale, start_pos):
    bs = z16.shape[0]
    return pl.pallas_call(
        functools.partial(_pool_step_kernel, start_pos=start_pos),
        grid=(1,),
        in_specs=[pl.BlockSpec((bs, 16, 512), lambda i: (0, 0, 0)),
                  pl.BlockSpec((4, LANES, LANES), lambda i: (0, 0, 0)),
                  pl.BlockSpec((1, 512), lambda i: (0, 0))],
        out_specs=pl.BlockSpec((bs, 512), lambda i: (0, 0)),
        out_shape=jax.ShapeDtypeStruct((bs, 512), BF16),
        compiler_params=_cparams(("arbitrary",)),
        name="pool_step",
    )(z16, pool_w, pool_scale)


def _merge_kernel(x_ref, sc1_ref, sh1_ref, g1_ref, sc2_ref, sh2_ref, orw_ref, onsa_ref, opool_ref,
                  wg_ref, wbr_ref, wbn_ref, wbp_ref, wout_ref, lng_ref, lnb_ref, wr_ref, br_ref,
                  x1_ref, u2_ref, route_ref, *, alpha):
    x = x_ref[...]
    d = x.shape[1]
    u = (x * (1.0 + sc1_ref[...]) + sh1_ref[...]).astype(BF16)
    mixed = jnp.zeros(x.shape, F32)
    for bi, (o_ref, wb_ref) in enumerate(((orw_ref, wbr_ref), (onsa_ref, wbn_ref), (opool_ref, wbp_ref))):
        gate = _sigmoid(_dot(u, wg_ref[:, bi * d:(bi + 1) * d]))
        mixed = mixed + gate * _dot(o_ref[...], wb_ref[...])
    m = _dot(mixed.astype(BF16), wout_ref[...])
    x1 = _layer_norm(alpha * x + (1.0 + g1_ref[...]) * m, lng_ref[...], lnb_ref[...])
    x1_ref[...] = x1
    u2 = x1 * (1.0 + sc2_ref[...]) + sh2_ref[...]
    u2b = u2.astype(BF16)
    u2_ref[...] = u2b
    lg = _dot(u2b, wr_ref[...]) + br_ref[...]
    lane = lax.broadcasted_iota(jnp.int32, (1, LANES), 1)
    is_g = lane < MOE_GROUPS
    mg = jnp.max(jnp.where(is_g, lg, BELOW_NEG_INF), axis=1, keepdims=True)
    gsel = jnp.min(jnp.where(is_g & (lg == mg), lane, LANES), axis=1, keepdims=True)
    wgrp = 1.0 / jnp.sum(jnp.where(is_g, jnp.exp(lg - mg), 0.0), axis=1, keepdims=True)
    e_lane = lane - ROUTE_LANE0
    in_grp = (e_lane >= 0) & (e_lane < MOE_GROUPS * EXPERTS_PER_GROUP) & ((e_lane // EXPERTS_PER_GROUP) == gsel)
    v1 = jnp.max(jnp.where(in_grp, lg, BELOW_NEG_INF), axis=1, keepdims=True)
    i1 = jnp.min(jnp.where(in_grp & (lg == v1), lane, LANES), axis=1, keepdims=True)
    rest = in_grp & (lane != i1)
    v2 = jnp.max(jnp.where(rest, lg, BELOW_NEG_INF), axis=1, keepdims=True)
    i2 = jnp.min(jnp.where(rest & (lg == v2), lane, LANES), axis=1, keepdims=True)
    e2 = jnp.exp(v2 - v1)
    w1 = wgrp / (1.0 + e2)
    w2 = wgrp * e2 / (1.0 + e2)
    for g in range(MOE_GROUPS):
        src = lane + (ROUTE_LANE0 + EXPERTS_PER_GROUP * g)
        route_ref[g] = jnp.where(lane < EXPERTS_PER_GROUP,
                                 jnp.where(src == i1, w1, jnp.where(src == i2, w2, 0.0)), 0.0)


def _merge(x, mods, o_rw, o_nsa, o_pool, wts, rows_per_group, tm, alpha):
    n, d = x.shape
    full = lambda a: pl.BlockSpec(a.shape, lambda i: (0,) * a.ndim)
    row = lambda c: pl.BlockSpec((tm, c), lambda i: (i, 0))
    return pl.pallas_call(
        functools.partial(_merge_kernel, alpha=alpha),
        grid=(n // tm,),
        in_specs=[row(d)] + [_mod_spec(m, rows_per_group, tm) for m in mods]
                 + [row(512), row(1024), row(512)] + [full(w) for w in wts],
        out_specs=[row(d), row(d), pl.BlockSpec((MOE_GROUPS, tm, LANES), lambda i: (0, i, 0))],
        out_shape=[jax.ShapeDtypeStruct((n, d), F32), jax.ShapeDtypeStruct((n, d), BF16),
                   jax.ShapeDtypeStruct((MOE_GROUPS, n, LANES), F32)],
        compiler_params=_cparams(("parallel",), 56),
        name="merge",
    )(x, *mods, o_rw, o_nsa, o_pool, *wts)


def _moe_kernel(x1_ref, u2_ref, route_ref, g2_ref, wg_ref, wu_ref, wd_ref, lng_ref, lnb_ref, o_ref, acc_ref, *, alpha):
    g = pl.program_id(1)

    @pl.when(g == 0)
    def _():
        acc_ref[...] = jnp.zeros(acc_ref.shape, F32)

    u = u2_ref[...]
    hg = _dot(u, wg_ref[...])
    h = (hg * _sigmoid(hg) * _dot(u, wu_ref[...])).astype(BF16)
    route = route_ref[...]
    y = acc_ref[...]
    for e in range(EXPERTS_PER_GROUP):
        w = route[:, e:e + 1]
        ye = _dot(h[:, e * EXPERT_HIDDEN:(e + 1) * EXPERT_HIDDEN], wd_ref[e * EXPERT_HIDDEN:(e + 1) * EXPERT_HIDDEN, :])
        y = y + jnp.where(w != 0.0, w * ye, 0.0)
    acc_ref[...] = y

    @pl.when(g == pl.num_programs(1) - 1)
    def _():
        h2 = alpha * x1_ref[...] + (1.0 + g2_ref[...]) * acc_ref[...]
        o_ref[...] = _layer_norm(h2, lng_ref[...], lnb_ref[...])


def _moe(x1, u2, route, g2, w_gate, w_up, w_down, ln_g, ln_b, rows_per_group, tm, alpha):
    n, d = x1.shape
    hid = EXPERTS_PER_GROUP * EXPERT_HIDDEN
    r = g2.shape[1]
    return pl.pallas_call(
        functools.partial(_moe_kernel, alpha=alpha),
        grid=(n // tm, MOE_GROUPS),
        in_specs=[pl.BlockSpec((tm, d), lambda i, g: (i, 0)),
                  pl.BlockSpec((tm, d), lambda i, g: (i, 0)),
                  pl.BlockSpec((None, tm, LANES), lambda i, g: (g, i, 0)),
                  pl.BlockSpec((None, r, d), lambda i, g: ((i * tm) // rows_per_group, 0, 0)),
                  pl.BlockSpec((None, d, hid), lambda i, g: (g, 0, 0)),
                  pl.BlockSpec((None, d, hid), lambda i, g: (g, 0, 0)),
                  pl.BlockSpec((None, hid, d), lambda i, g: (g, 0, 0)),
                  pl.BlockSpec((1, d), lambda i, g: (0, 0)),
                  pl.BlockSpec((1, d), lambda i, g: (0, 0))],
        out_specs=pl.BlockSpec((tm, d), lambda i, g: (i, 0)),
        out_shape=jax.ShapeDtypeStruct((n, d), F32),
        scratch_shapes=[pltpu.VMEM((tm, d), F32)],
        compiler_params=_cparams(("parallel", "arbitrary")),
        name="moe",
    )(x1, u2, route, g2, w_gate, w_up, w_down, ln_g, ln_b)


def _pad_heads(w, axis):
    shp = w.shape
    w = w.reshape(shp[:axis] + (NSA_KV_HEADS, NSA_GROUP, 1, NSA_HEAD_DIM) + shp[axis + 1:])
    sel = jnp.eye(NSA_KV_HEADS, dtype=w.dtype).reshape((1,) * axis + (NSA_KV_HEADS, 1, NSA_KV_HEADS, 1) + (1,) * (len(shp) - axis - 1))
    w = w * sel
    return w.reshape(shp[:axis] + (NSA_HEADS * LANES,) + shp[axis + 1:])


def _prep_weights(w_in, rw_w_up, rw_a_up, w_branch, moe_router_g, moe_bias_g, moe_router_e, moe_bias_e,
                  moe_w_gate, moe_w_up, moe_w_down, nsa_cmp_wk, nsa_cmp_wv):
    L, D, _ = w_in.shape
    w = {}
    w['rw'] = w_in[:, :, :RW_END].astype(BF16)
    wq = _pad_heads(w_in[:, :, RW_END:NSA_Q_END] * (NSA_HEAD_DIM ** -0.5), 2)
    wg = jnp.pad(w_in[:, :, NSA_KV_END:NSA_END], ((0, 0), (0, 0), (0, LANES - 3 * NSA_HEADS)))
    w['nsa'] = jnp.concatenate([wq, w_in[:, :, NSA_Q_END:NSA_KV_END], wg], axis=-1).astype(BF16)
    w['pool'] = w_in[:, :, NSA_END:POOL_END].astype(BF16)
    w['gate'] = w_in[:, :, POOL_END:].astype(BF16)
    z = jnp.zeros((L, HALF, RW_WIDTH), F32)
    w['wwa'] = jnp.concatenate([jnp.concatenate([rw_w_up, z], axis=2),
                                jnp.concatenate([z, rw_a_up], axis=2)], axis=1).astype(BF16)
    w['b_rw'] = w_branch[:, 0].astype(BF16)
    w['b_nsa'] = _pad_heads(w_branch[:, 1], 1).astype(BF16)
    w['b_pool'] = w_branch[:, 2].astype(BF16)
    pad_r = LANES - MOE_GROUPS - MOE_GROUPS * EXPERTS_PER_GROUP
    w['router'] = jnp.pad(jnp.concatenate([moe_router_g, moe_router_e], axis=2), ((0, 0), (0, 0), (0, pad_r))).astype(BF16)
    w['router_b'] = jnp.pad(jnp.concatenate([moe_bias_g, moe_bias_e], axis=1), ((0, 0), (0, pad_r)))[:, None, :]
    hid = EXPERTS_PER_GROUP * EXPERT_HIDDEN
    grp = lambda a: jnp.transpose(a.reshape(L, MOE_GROUPS, EXPERTS_PER_GROUP, D, EXPERT_HIDDEN),
                                  (0, 1, 3, 2, 4)).reshape(L, MOE_GROUPS, D, hid).astype(BF16)
    w['moe_gate'] = grp(moe_w_gate)
    w['moe_up'] = grp(moe_w_up)
    w['moe_down'] = moe_w_down.reshape(L, MOE_GROUPS, hid, D).astype(BF16)
    w['cmp_wk'] = jnp.tile(nsa_cmp_wk, (1, 1, NSA_KV_HEADS))
    w['cmp_wv'] = jnp.tile(nsa_cmp_wv, (1, 1, NSA_KV_HEADS))
    return w


def kernel(x_prompt, x_sample, cache_nsa_kv, cache_win_kv, state_rwkv, state_rwkv_shift, state_pool, page_table, c_prompt, c_sample, ada_w, ada_b, w_in, rw_mu, rw_w0, rw_w_up, rw_a0, rw_a_up, rw_g_up, rw_k_k, rw_k_a, rw_r_k, rw_gn_g, rw_gn_b, nsa_cmp_wk, nsa_cmp_wv, pool_w, pool_scale, w_branch, w_out, ln1_g, ln1_b, moe_router_g, moe_bias_g, moe_router_e, moe_bias_e, moe_w_gate, moe_w_up, moe_w_down, ln2_g, ln2_b):
    bp, t, d = x_prompt.shape
    bs, ts, _ = x_sample.shape
    L = ada_w.shape[0]
    n_pool, page = cache_nsa_kv.shape[1], cache_nsa_kv.shape[2]
    n_pages = page_table.shape[1]
    past_len = n_pages * page
    assert ts == 1 and d == D_MODEL
    assert t % 256 == 0 and t // CMP_BLOCK <= LANES and past_len % SEL_BLOCK == 0
    assert cache_win_kv.shape[2] == WINDOW and bs % 8 == 0 and 2 * n_pages == LANES
    alpha = (2 * L) ** 0.25
    tm = 256
    chunk_pad = RW_CHUNK

    w = _prep_weights(w_in, rw_w_up, rw_a_up, w_branch, moe_router_g, moe_bias_g, moe_router_e, moe_bias_e,
                      moe_w_gate, moe_w_up, moe_w_down, nsa_cmp_wk, nsa_cmp_wv)
    w_out_b = w_out.astype(BF16)
    pool_w_b = pool_w.astype(BF16)
    gup_b = rw_g_up.astype(BF16)

    nb = bp + bs
    nb_pad = -(-nb // 8) * 8
    c_all = jnp.pad(jnp.concatenate([c_prompt, c_sample], axis=0), ((0, nb_pad - nb), (0, 0)))
    mods = _ada_mod(c_all, ada_w, ada_b)

    cache4 = cache_nsa_kv.reshape(L, n_pool, page, 4 * LANES)
    pb = 8 if n_pool % 8 == 0 else 1
    cmp_pool = _compress_pool(cache4, w['cmp_wk'], w['cmp_wv'], pb)
    slopes = jnp.broadcast_to(jnp.asarray(NSA_SLOPES, F32)[:, None], (NSA_HEADS, LANES))
    win_cache = cache_win_kv.reshape(L, bs, WINDOW, 2 * LANES)

    xp = x_prompt.reshape(bp * t, d)
    xs = x_sample.reshape(bs, d)
    zeros_prev = jnp.zeros((bp, RW_COLS), F32)
    zeros_hist = jnp.zeros((bp, 16, POOL_WIDTH), F32)
    zeros_state = jnp.zeros((bp, 4, LANES, LANES), F32)
    outs = {k: [] for k in ('nsa_p', 'nsa_s', 'win_p', 'win_s', 'rw_p', 'rw_s', 'sh_p', 'sh_s', 'pool_p', 'pool_s')}

    for l in range(L):
        vec = lambda a: a[l].reshape(1, -1)
        rw_wts = (vec(rw_mu), vec(rw_w0), vec(rw_a0), w['wwa'][l], gup_b[l], vec(rw_k_k), vec(rw_k_a), vec(rw_r_k))
        merge_wts = (w['gate'][l], w['b_rw'][l], w['b_nsa'][l], w['b_pool'][l], w_out_b[l], vec(ln1_g), vec(ln1_b),
                     w['router'][l], w['router_b'][l])
        mod_p = [m[:, None, :] for m in jnp.split(mods[l, :bp], 6, axis=-1)]
        mod_s = [m[None] for m in jnp.split(mods[l, bp:bp + bs], 6, axis=-1)]

        p_rw = _proj(xp, mod_p[1], mod_p[0], w['rw'][l], t, tm)
        q_pad, kv, gates, kvt, cmp = _proj_nsa(xp, mod_p[1], mod_p[0], w['nsa'][l], w['cmp_wk'][l], w['cmp_wv'][l], bp, t, tm)
        p_pool = _proj(xp, mod_p[1], mod_p[0], w['pool'][l], t, tm)
        seqs = _rwkv_prep(p_rw, zeros_prev, rw_wts, t, tm)
        o_rw, s_bd = _rwkv_chunk(seqs, zeros_state, vec(rw_gn_g), vec(rw_gn_b), bp, t)
        o_nsa = _nsa_prompt(q_pad, kvt, _permute_cmp(cmp, bp, t), gates, bp, t)
        o_pool = _pool_seq(p_pool, zeros_hist, pool_w_b[l], vec(pool_scale), t, tm, 0)
        x1, u2, route = _merge(xp, (mod_p[1], mod_p[0], mod_p[2], mod_p[4], mod_p[3]), o_rw, o_nsa, o_pool,
                               merge_wts, t, tm, alpha)
        xp = _moe(x1, u2, route, mod_p[5], w['moe_gate'][l], w['moe_up'][l], w['moe_down'][l],
                  vec(ln2_g), vec(ln2_b), t, tm, alpha)
        kv3 = kv.reshape(bp, t, 6, NSA_KV_HEADS, NSA_HEAD_DIM)
        outs['nsa_p'].append(kv3[:, :, :4])
        outs['win_p'].append(kv3[:, t - min(WINDOW, t):, 4:])
        outs['rw_p'].append(_state_from_bd(s_bd))
        outs['sh_p'].append(p_rw.reshape(bp, t, RW_COLS)[:, -1])
        outs['pool_p'].append(p_pool.reshape(bp, t, POOL_WIDTH)[:, t - POOL_HIST:])

        p_rw = _proj(xs, mod_s[1], mod_s[0], w['rw'][l], bs, bs)
        q_pad, kv, gates = _proj_nsa(xs, mod_s[1], mod_s[0], w['nsa'][l], w['cmp_wk'][l], w['cmp_wv'][l], bs, 1, bs)
        p_pool = _proj(xs, mod_s[1], mod_s[0], w['pool'][l], bs, bs)
        seqs = _rwkv_prep(p_rw, state_rwkv_shift[l], rw_wts, 1, bs)
        seqs = [jnp.pad(a[:, None, :], ((0, 0), (0, chunk_pad - 1), (0, 0))).reshape(bs * chunk_pad, 512) for a in seqs]
        o_rw, s_bd = _rwkv_chunk(seqs, _state_to_bd(state_rwkv[l]), vec(rw_gn_g), vec(rw_gn_b), bs, chunk_pad)
        o_rw = o_rw.reshape(bs, chunk_pad, 512)[:, 0]
        o_nsa = _nsa_decode(page_table, cmp_pool[l], cache4[l], win_cache[l], q_pad, gates, kv, slopes, past_len)
        z16 = jnp.concatenate([state_pool[l].astype(F32), p_pool[:, None, :]], axis=1)
        o_pool = _pool_step(z16, pool_w_b[l], vec(pool_scale), past_len)
        x1, u2, route = _merge(xs, (mod_s[1], mod_s[0], mod_s[2], mod_s[4], mod_s[3]), o_rw, o_nsa, o_pool,
                               merge_wts, bs, bs, alpha)
        xs = _moe(x1, u2, route, mod_s[5], w['moe_gate'][l], w['moe_up'][l], w['moe_down'][l],
                  vec(ln2_g), vec(ln2_b), bs, bs, alpha)
        kv3 = kv.reshape(bs, 1, 6, NSA_KV_HEADS, NSA_HEAD_DIM)
        outs['nsa_s'].append(kv3[:, :, :4])
        outs['win_s'].append(jnp.concatenate([cache_win_kv[l].astype(F32), kv3[:, :, 4:]], axis=1)[:, -WINDOW:])
        outs['rw_s'].append(_state_from_bd(s_bd))
        outs['sh_s'].append(p_rw)
        outs['pool_s'].append(z16[:, 1:])

    st = lambda k: jnp.stack(outs[k])
    return (xp.reshape(bp, t, d), xs.reshape(bs, 1, d), st('nsa_p'), st('nsa_s'), st('win_p'), st('win_s'),
            st('rw_p'), st('rw_s'), st('sh_p'), st('sh_s'), st('pool_p'), st('pool_s'))
```

```python
import functools

import jax
import jax.numpy as jnp
from jax import lax
from jax.experimental import pallas as pl
from jax.experimental.pallas import tpu as pltpu

F32 = jnp.float32
BF16 = jnp.bfloat16

D_MODEL = 1024
RW_HEADS = 8
RW_HEAD_DIM = 64
RW_WIDTH = 512
RW_COLS = 1792
RW_GN_EPS = 64e-5
RW_CHUNK = 64

NSA_HEADS = 8
NSA_KV_HEADS = 2
NSA_HEAD_DIM = 64
NSA_GROUP = 4
CMP_BLOCK = 32
SEL_BLOCK = 64
TOP_N = 8
WINDOW = 512
NSA_TQ = 128
NEG_INF = -1e30
BELOW_NEG_INF = -3e38
FORCE_BONUS = 1e9
NSA_SLOPES = tuple(2.0 ** (-8.0 * (h + 1) / NSA_HEADS) for h in range(NSA_HEADS))

POOL_WINDOWS = (2, 4, 8, 16)
POOL_HIST = 15
POOL_WIDTH = 512

RW_END = 1792
NSA_Q_END = RW_END + 512
NSA_KV_END = NSA_Q_END + 768
NSA_END = NSA_KV_END + 24
POOL_END = NSA_END + 512

MOE_GROUPS = 4
EXPERTS_PER_GROUP = 4
EXPERT_HIDDEN = 256
ROUTE_LANE0 = 4

LN_EPS = 1e-5
LANES = 128
HALF = 64


def _cparams(sem, vmem_mb=48):
    return pltpu.CompilerParams(dimension_semantics=sem, vmem_limit_bytes=vmem_mb * 1024 * 1024)


def _dot(a, b):
    return jnp.dot(a, b, preferred_element_type=F32)


def _dot_nt(a, b):
    return lax.dot_general(a, b, (((1,), (1,)), ((), ())), preferred_element_type=F32)


def _dot_tn(a, b):
    return lax.dot_general(a, b, (((0,), (0,)), ((), ())), preferred_element_type=F32)


def _sigmoid(x):
    return 1.0 / (1.0 + jnp.exp(-x))


def _softplus(x):
    return jnp.maximum(x, 0.0) + jnp.log(1.0 + jnp.exp(-jnp.abs(x)))


def _layer_norm(h, g, b):
    mu = jnp.mean(h, axis=-1, keepdims=True)
    d = h - mu
    var = jnp.mean(d * d, axis=-1, keepdims=True)
    return d * lax.rsqrt(var + LN_EPS) * g + b


def _half_sum(x, lo):
    s_lo = jnp.sum(jnp.where(lo, x, 0.0), axis=1, keepdims=True)
    s_hi = jnp.sum(jnp.where(lo, 0.0, x), axis=1, keepdims=True)
    return jnp.where(lo, s_lo, s_hi)


def _ada_kernel(c_ref, w_ref, b_ref, o_ref):
    c = c_ref[...]
    s = (c * _sigmoid(c)).astype(BF16)
    o_ref[...] = _dot(s, w_ref[...].astype(BF16)) + b_ref[...]


def _ada_mod(c_all, ada_w, ada_b):
    L, D, D6 = ada_w.shape
    nb = c_all.shape[0]
    return pl.pallas_call(
        _ada_kernel,
        grid=(L, D6 // D),
        in_specs=[pl.BlockSpec((nb, D), lambda l, j: (0, 0)),
                  pl.BlockSpec((None, D, D), lambda l, j: (l, 0, j)),
                  pl.BlockSpec((None, 1, D), lambda l, j: (l, 0, j))],
        out_specs=pl.BlockSpec((None, nb, D), lambda l, j: (l, 0, j)),
        out_shape=jax.ShapeDtypeStruct((L, nb, D6), F32),
        compiler_params=_cparams(("parallel", "parallel")),
        name="ada_mod",
    )(c_all, ada_w, ada_b.reshape(L, 1, D6))


def _mod_spec(mod, rows_per_group, tm):
    r = mod.shape[1]
    return pl.BlockSpec((None, r, mod.shape[2]), lambda i: ((i * tm) // rows_per_group, 0, 0))


def _proj_kernel(x_ref, sc_ref, sh_ref, w_ref, o_ref):
    u = (x_ref[...] * (1.0 + sc_ref[...]) + sh_ref[...]).astype(BF16)
    o_ref[...] = _dot(u, w_ref[...])


def _proj(x, sc, sh, w, rows_per_group, tm):
    n, d = x.shape
    nc = w.shape[1]
    return pl.pallas_call(
        _proj_kernel,
        grid=(n // tm,),
        in_specs=[pl.BlockSpec((tm, d), lambda i: (i, 0)),
                  _mod_spec(sc, rows_per_group, tm), _mod_spec(sh, rows_per_group, tm),
                  pl.BlockSpec((d, nc), lambda i: (0, 0))],
        out_specs=pl.BlockSpec((tm, nc), lambda i: (i, 0)),
        out_shape=jax.ShapeDtypeStruct((n, nc), F32),
        compiler_params=_cparams(("parallel",)),
        name="proj",
    )(x, sc, sh, w)


def _proj_nsa_kernel(x_ref, sc_ref, sh_ref, w_ref, wk_ref, wv_ref, q_ref, kv_ref, gate_ref, kvt_ref=None, cmp_ref=None):
    u = (x_ref[...] * (1.0 + sc_ref[...]) + sh_ref[...]).astype(BF16)
    res = _dot(u, w_ref[...])
    q_ref[...] = res[:, :1024].astype(BF16)
    kv = res[:, 1024:1792]
    kv_ref[...] = kv
    gate_ref[...] = _sigmoid(res[:, 1792:1920])
    if kvt_ref is None:
        return
    for j in range(6):
        kvt_ref[j] = kv[:, j * LANES:(j + 1) * LANES].astype(BF16)
    nb = kv.shape[0] // CMP_BLOCK
    kc = jnp.sum(kv[:, 0:LANES].reshape(nb, CMP_BLOCK, LANES) * wk_ref[...][None], axis=1)
    vc = jnp.sum(kv[:, LANES:2 * LANES].reshape(nb, CMP_BLOCK, LANES) * wv_ref[...][None], axis=1)
    cmp_ref[:, 0:LANES] = kc
    cmp_ref[:, LANES:2 * LANES] = vc


def _proj_nsa(x, sc, sh, w, wk, wv, n_seq, t, tm):
    n, d = x.shape
    out_specs = [pl.BlockSpec((tm, 1024), lambda i: (i, 0)),
                 pl.BlockSpec((tm, 768), lambda i: (i, 0)),
                 pl.BlockSpec((tm, LANES), lambda i: (i, 0))]
    out_shape = [jax.ShapeDtypeStruct((n, 1024), BF16),
                 jax.ShapeDtypeStruct((n, 768), F32),
                 jax.ShapeDtypeStruct((n, LANES), F32)]
    if t > 1:
        tps = t // tm
        nb = tm // CMP_BLOCK
        out_specs += [pl.BlockSpec((None, 6, tm, LANES), lambda i: (i // tps, 0, i % tps, 0)),
                      pl.BlockSpec((nb, 2 * LANES), lambda i: (i, 0))]
        out_shape += [jax.ShapeDtypeStruct((n_seq, 6, t, LANES), BF16),
                      jax.ShapeDtypeStruct((n // CMP_BLOCK, 2 * LANES), F32)]
    rows_per_group = t if sc.shape[1] == 1 else n
    return pl.pallas_call(
        _proj_nsa_kernel,
        grid=(n // tm,),
        in_specs=[pl.BlockSpec((tm, d), lambda i: (i, 0)),
                  _mod_spec(sc, rows_per_group, tm), _mod_spec(sh, rows_per_group, tm),
                  pl.BlockSpec((d, 1920), lambda i: (0, 0)),
                  pl.BlockSpec((CMP_BLOCK, LANES), lambda i: (0, 0)),
                  pl.BlockSpec((CMP_BLOCK, LANES), lambda i: (0, 0))],
        out_specs=out_specs,
        out_shape=out_shape,
        compiler_params=_cparams(("parallel",)),
        name="proj_nsa",
    )(x, sc, sh, w, wk, wv)


def _rwkv_prep_body(p, prev, mu_ref, w0_ref, a0_ref, wwa_ref, gup_ref, kkw_ref, kaw_ref, rk_ref, outs):
    r_ref, lw_ref, k_ref, v_ref, kk_ref, kka_ref, g_ref, bonus_ref = outs
    xs = p + (prev - p) * mu_ref[...]
    r = xs[:, 0:512]
    xk = xs[:, 512:1024]
    v = xs[:, 1024:1536]
    t12 = xs[:, 1536:1664]
    gd = xs[:, 1664:1792]
    lane = lax.broadcasted_iota(jnp.int32, (1, LANES), 1)
    lo = lane < HALF
    z = jnp.where(lo, jnp.tanh(t12), t12).astype(BF16)
    dwa = _dot(z, wwa_ref[...])
    w_log = -_softplus(-(w0_ref[...] + dwa[:, :512])) - 0.5
    a = _sigmoid(a0_ref[...] + dwa[:, 512:])
    g_ref[...] = _dot(_sigmoid(gd).astype(BF16), gup_ref[...])
    kmod = xk * (1.0 + (a - 1.0) * kaw_ref[...])
    kkr = xk * kkw_ref[...]
    rkr = r * kmod * rk_ref[...]
    for m in range(4):
        sl = slice(m * LANES, (m + 1) * LANES)
        x = kkr[:, sl]
        nrm = jnp.sqrt(_half_sum(x * x, lo))
        kk = x / jnp.maximum(nrm, 1e-12)
        kk_ref[:, sl] = kk
        kka_ref[:, sl] = kk * a[:, sl]
        bonus_ref[:, sl] = _half_sum(rkr[:, sl], lo) * v[:, sl]
    r_ref[...] = r
    lw_ref[...] = -jnp.exp(w_log)
    k_ref[...] = kmod
    v_ref[...] = v


def _rwkv_prep_seq_kernel(p_ref, pprev_ref, prow_ref, *rest, seq_tiles):
    i = pl.program_id(0)
    p = p_ref[...]
    first = (i % seq_tiles) == 0
    prev_row = jnp.where(first, prow_ref[...], pprev_ref[7:8, :])
    rolled = pltpu.roll(p, 1, 0)
    rowid = lax.broadcasted_iota(jnp.int32, (p.shape[0], 1), 0)
    prev = jnp.where(rowid == 0, prev_row, rolled)
    _rwkv_prep_body(p, prev, *rest[:8], rest[8:])


def _rwkv_prep_step_kernel(p_ref, prev_ref, *rest):
    _rwkv_prep_body(p_ref[...], prev_ref[...], *rest[:8], rest[8:])


def _rwkv_prep(p_rw, prev_rows, wts, t, tm):
    n = p_rw.shape[0]
    vec = lambda c: pl.BlockSpec((1, c), lambda i: (0, 0))
    w_specs = [vec(RW_COLS), vec(512), vec(512), pl.BlockSpec((LANES, 1024), lambda i: (0, 0)),
               pl.BlockSpec((LANES, 512), lambda i: (0, 0)), vec(512), vec(512), vec(512)]
    out_specs = [pl.BlockSpec((tm, 512), lambda i: (i, 0))] * 8
    out_shape = [jax.ShapeDtypeStruct((n, 512), F32)] * 8
    if t == 1:
        kern = _rwkv_prep_step_kernel
        in_specs = [pl.BlockSpec((tm, RW_COLS), lambda i: (i, 0)), pl.BlockSpec((tm, RW_COLS), lambda i: (i, 0))]
        args = (p_rw, prev_rows)
    else:
        tps = t // tm
        kern = functools.partial(_rwkv_prep_seq_kernel, seq_tiles=tps)
        in_specs = [pl.BlockSpec((tm, RW_COLS), lambda i: (i, 0)),
                    pl.BlockSpec((8, RW_COLS), lambda i: (jnp.maximum(i * (tm // 8) - 1, 0), 0)),
                    pl.BlockSpec((None, 1, RW_COLS), lambda i: (i // tps, 0, 0))]
        args = (p_rw, p_rw, prev_rows[:, None, :])
    return pl.pallas_call(
        kern, grid=(n // tm,), in_specs=in_specs + w_specs, out_specs=out_specs, out_shape=out_shape,
        compiler_params=_cparams(("parallel",)), name="rwkv_prep",
    )(*args, *wts)


def _split3(x):
    x1 = x.astype(BF16)
    r1 = x - x1.astype(F32)
    x2 = r1.astype(BF16)
    x3 = (r1 - x2.astype(F32)).astype(BF16)
    return x1, x2, x3


def _rwkv_chunk_kernel(r_ref, lw_ref, k_ref, v_ref, kk_ref, kka_ref, g_ref, bonus_ref, gng_ref, gnb_ref,
                       o_ref, sout_ref, s_ref):
    c = pl.program_id(1)
    C = RW_CHUNK

    @pl.when(c == 0)
    def _():
        s_ref[...] = jnp.zeros(s_ref.shape, F32)

    lw = lw_ref[...]
    row = lax.broadcasted_iota(jnp.int32, (C, C), 0)
    col = lax.broadcasted_iota(jnp.int32, (C, C), 1)
    tri = (row >= col).astype(BF16)
    l1, l2, l3 = _split3(lw)
    cl = _dot(tri, l1) + _dot(tri, l2) + _dot(tri, l3)
    cl_last = cl[C - 1:C, :]
    g_in = jnp.exp(cl)
    g_ex = jnp.exp(cl - lw)
    g_inv = jnp.exp(-cl)
    g_rem = jnp.exp(cl_last - cl)
    gc = jnp.exp(cl_last)
    k = k_ref[...]
    kka = kka_ref[...]
    qk_all = kk_ref[...] * g_ex
    r_all = r_ref[...] * g_in
    kt_all = k * g_inv
    at_all = kka * g_inv
    kd_all = k * g_rem
    ad_all = kka * g_rem
    v_all = v_ref[...]

    lane = lax.broadcasted_iota(jnp.int32, (1, LANES), 1)
    lo = lane < HALF

    def st(x):
        return jnp.concatenate([jnp.where(lo, x, 0.0), jnp.where(lo, 0.0, x)], axis=0)

    r2 = lax.broadcasted_iota(jnp.int32, (4 * C, 4 * C), 0)
    c2 = lax.broadcasted_iota(jnp.int32, (4 * C, 4 * C), 1)
    rt = r2 % C
    ct = c2 % C
    tmask = (ct < rt) | ((r2 >= 2 * C) & (ct == rt))
    ri = lax.broadcasted_iota(jnp.int32, (2 * C, 2 * C), 0)
    ci = lax.broadcasted_iota(jnp.int32, (2 * C, 2 * C), 1)
    eye = (ri == ci).astype(F32)

    for m in range(4):
        sl = slice(m * LANES, (m + 1) * LANES)
        qk_st = st(qk_all[:, sl])
        r_st = st(r_all[:, sl])
        qr = jnp.concatenate([qk_st, r_st], axis=0).astype(BF16)
        ak = jnp.concatenate([st(at_all[:, sl]), st(kt_all[:, sl])], axis=0).astype(BF16)
        xm = jnp.where(tmask, _dot_nt(qr, ak), 0.0)
        v_st = st(v_all[:, sl])
        av = _dot(xm[:, 2 * C:].astype(BF16), v_st.astype(BF16))
        aak_v = av[:2 * C]
        ark_v = av[2 * C:]
        p = xm[:2 * C, :2 * C]
        tm_ = eye - p
        for _ in range(5):
            pb = p.astype(BF16)
            p = _dot(pb, pb)
            tm_ = _dot(tm_.astype(BF16), (eye + p).astype(BF16))
        wu = _dot(tm_.astype(BF16), jnp.concatenate([qk_st, aak_v], axis=1).astype(BF16))
        ry = _dot(xm[2 * C:, :2 * C].astype(BF16), wu.astype(BF16))
        wm = wu[:, :LANES]
        u = wu[:, LANES:]
        rq = r_st - ry[:, :LANES]
        y0 = ark_v - ry[:, LANES:]
        ad_st = st(ad_all[:, sl])
        kd_st = st(kd_all[:, sl])
        gt = eye * gc[:, sl] - _dot_tn(wm.astype(BF16), ad_st.astype(BF16))
        ht = _dot_tn(jnp.concatenate([v_st, u], axis=0).astype(BF16),
                     jnp.concatenate([kd_st, -ad_st], axis=0).astype(BF16))
        s = s_ref[m]
        sb = s.astype(BF16)
        y_st = _dot_nt(rq.astype(BF16), sb) + y0
        s_new = _dot(sb, gt.astype(BF16)) + ht
        s_ref[m] = s_new
        y = y_st[:C] + y_st[C:]
        o_ref[:, sl] = _rwkv_out(y, lo, gng_ref[:, sl], gnb_ref[:, sl], bonus_ref[:, sl], g_ref[:, sl])

        @pl.when(c == pl.num_programs(1) - 1)
        def _():
            sout_ref[m] = s_new[:C] + s_new[C:]


def _rwkv_out(y, lo, gn_g, gn_b, bonus, g):
    mu = _half_sum(y, lo) * (1.0 / RW_HEAD_DIM)
    d = y - mu
    var = _half_sum(d * d, lo) * (1.0 / RW_HEAD_DIM)
    yn = d * lax.rsqrt(var + RW_GN_EPS) * gn_g + gn_b
    return ((yn + bonus) * g).astype(BF16)


def _rwkv_chunk(seqs, gn_g, gn_b, n_seq, t):
    C = RW_CHUNK
    nch = t // C
    row_spec = pl.BlockSpec((C, 512), lambda b, c: (b * nch + c, 0))
    vec = pl.BlockSpec((1, 512), lambda b, c: (0, 0))
    return pl.pallas_call(
        _rwkv_chunk_kernel,
        grid=(n_seq, nch),
        in_specs=[row_spec] * 8 + [vec, vec],
        out_specs=[row_spec, pl.BlockSpec((None, 4, RW_HEAD_DIM, LANES), lambda b, c: (b, 0, 0, 0))],
        out_shape=[jax.ShapeDtypeStruct((n_seq * t, 512), BF16),
                   jax.ShapeDtypeStruct((n_seq, 4, RW_HEAD_DIM, LANES), F32)],
        scratch_shapes=[pltpu.VMEM((4, LANES, LANES), F32)],
        compiler_params=_cparams(("parallel", "arbitrary")),
        name="rwkv_chunk",
    )(*seqs, gn_g, gn_b)


def _rwkv_step_kernel(r_ref, lw_ref, k_ref, v_ref, kk_ref, kka_ref, g_ref, bonus_ref, s_ref, gng_ref, gnb_ref,
                      o_ref, sout_ref):
    lane = lax.broadcasted_iota(jnp.int32, (1, LANES), 1)
    lo = lane < HALF
    rowi = lax.broadcasted_iota(jnp.int32, (RW_HEAD_DIM, LANES), 0)
    diag = (rowi == lax.broadcasted_iota(jnp.int32, (RW_HEAD_DIM, LANES), 1) % HALF).astype(F32)
    for m in range(4):
        sl = slice(m * LANES, (m + 1) * LANES)
        s = s_ref[m]
        sk = _half_sum(s * kk_ref[:, sl], lo)
        v_col = _half_sum(diag * v_ref[:, sl], lo)
        s_new = s * jnp.exp(lw_ref[:, sl]) - sk * kka_ref[:, sl] + v_col * k_ref[:, sl]
        sout_ref[m] = s_new
        y_col = _half_sum(s_new * r_ref[:, sl], lo)
        y = jnp.sum(diag * y_col, axis=0, keepdims=True)
        o_ref[:, sl] = _rwkv_out(y, lo, gng_ref[:, sl], gnb_ref[:, sl], bonus_ref[:, sl], g_ref[:, sl])


def _rwkv_step(seqs, s_pairs, gn_g, gn_b):
    bs = s_pairs.shape[0]
    row_spec = pl.BlockSpec((None, 1, 512), lambda b: (b, 0, 0))
    st_spec = pl.BlockSpec((None, 4, RW_HEAD_DIM, LANES), lambda b: (b, 0, 0, 0))
    vec = pl.BlockSpec((1, 512), lambda b: (0, 0))
    o, s = pl.pallas_call(
        _rwkv_step_kernel,
        grid=(bs,),
        in_specs=[row_spec] * 8 + [st_spec, vec, vec],
        out_specs=[row_spec, st_spec],
        out_shape=[jax.ShapeDtypeStruct((bs, 1, 512), BF16),
                   jax.ShapeDtypeStruct((bs, 4, RW_HEAD_DIM, LANES), F32)],
        compiler_params=_cparams(("parallel",)),
        name="rwkv_step",
    )(*[a[:, None, :] for a in seqs], s_pairs, gn_g, gn_b)
    return o[:, 0], s


def _state_to_pairs(s):
    b = s.shape[0]
    return jnp.transpose(s.reshape(b, 4, 2, 64, 64).astype(F32), (0, 1, 3, 2, 4)).reshape(b, 4, 64, LANES)


def _state_from_pairs(sp):
    b = sp.shape[0]
    return jnp.transpose(sp.reshape(b, 4, 64, 2, 64), (0, 1, 3, 2, 4)).reshape(b, 8, 64, 64)


def _nsa_prompt_kernel(q_ref, kvt_ref, cmp_ref, gate_ref, o_ref, mx_ref, acc_ref, *, k_top):
    qi = pl.program_id(1)
    tq = NSA_TQ
    q0 = qi * tq
    lane = lax.broadcasted_iota(jnp.int32, (1, LANES), 1)
    lo = lane < HALF
    tok = q0 + lax.broadcasted_iota(jnp.int32, (tq, 1), 0)
    cmpv = cmp_ref[...]
    kc = cmpv[:, :LANES].astype(BF16)
    vc = cmpv[:, LANES:].astype(BF16)
    gates = gate_ref[...]
    blk_c = 2 * (lane % HALF) + lane // HALF
    dist_c = tok - (blk_c * CMP_BLOCK + (CMP_BLOCK - 1))
    mask_c = dist_c >= 0
    dist_cf = dist_c.astype(F32)
    cur = tok // SEL_BLOCK
    valid = (lane * SEL_BLOCK <= tok) & lo
    forced = (lane == 0) | (lane == cur) | (lane == cur - 1)
    blk_row = lax.broadcasted_iota(jnp.int32, (LANES, LANES), 0)
    key_blk = lax.broadcasted_iota(jnp.int32, (LANES, LANES), 1) // SEL_BLOCK

    rel = lax.broadcasted_iota(jnp.int32, (tq, LANES), 0) - lane
    rel_f = rel.astype(F32)
    diag_bias = jnp.where(rel >= 0, 0.0, NEG_INF)

    def attend(qg, g, k_idx, v_idx, j_lo, bias_fn):
        in_group = lo if g == 0 else jnp.logical_not(lo)

        def scores(j, diag):
            start = pl.multiple_of(j * tq, tq)
            s = _dot_nt(qg, kvt_ref[k_idx, pl.ds(start, tq), :])
            dist_f = rel_f + ((qi - j) * tq).astype(F32)
            bias = bias_fn(j, dist_f)
            if diag:
                bias = diag_bias if bias is None else bias + diag_bias
            xs = []
            for r in range(NSA_GROUP):
                x = s[r * tq:(r + 1) * tq] - NSA_SLOPES[NSA_GROUP * g + r] * dist_f
                xs.append(x if bias is None else x + bias)
            return start, xs

        def max_tile(j, diag):
            _, xs = scores(j, diag)
            for r in range(NSA_GROUP):
                rows = slice(r * tq, (r + 1) * tq)
                mx_ref[rows] = jnp.maximum(mx_ref[rows], xs[r])

        def sum_tile(j, diag):
            start, xs = scores(j, diag)
            es = [jnp.exp(xs[r] - mx_ref[r * tq:(r + 1) * tq]).astype(BF16) for r in range(NSA_GROUP)]
            vt = kvt_ref[v_idx, pl.ds(start, tq), :]
            vt = jnp.where(in_group, vt, jnp.ones_like(vt))
            acc_ref[...] += _dot(jnp.concatenate(es, axis=0), vt)

        mx_ref[...] = jnp.full(mx_ref.shape, NEG_INF, F32)
        lax.fori_loop(j_lo, qi, lambda j, c: (max_tile(j, False), c)[1], 0)
        max_tile(qi, True)
        mx_ref[...] = jnp.broadcast_to(jnp.max(mx_ref[...], axis=1, keepdims=True), mx_ref.shape)
        acc_ref[...] = jnp.zeros(acc_ref.shape, F32)
        lax.fori_loop(j_lo, qi, lambda j, c: (sum_tile(j, False), c)[1], 0)
        sum_tile(qi, True)
        acc = acc_ref[...]
        den_lane = HALF if g == 0 else 0
        return acc / jnp.maximum(acc[:, den_lane:den_lane + 1], 1e-30)

    for g in range(NSA_KV_HEADS):
        qg = jnp.concatenate([q_ref[:, (NSA_GROUP * g + r) * LANES:(NSA_GROUP * g + r + 1) * LANES]
                              for r in range(NSA_GROUP)], axis=0)
        s = _dot_nt(qg, kc)
        ps = []
        imp = jnp.zeros((tq, LANES), F32)
        for r in range(NSA_GROUP):
            sr = jnp.where(mask_c, s[r * tq:(r + 1) * tq] - NSA_SLOPES[NSA_GROUP * g + r] * dist_cf, NEG_INF)
            mx = jnp.max(sr, axis=1, keepdims=True)
            e = jnp.where(mask_c, jnp.exp(sr - mx), 0.0)
            p = e / jnp.maximum(jnp.sum(e, axis=1, keepdims=True), 1e-30)
            ps.append(p.astype(BF16))
            imp = imp + p
        o_c = _dot(jnp.concatenate(ps, axis=0), vc)
        imp_sel = imp + pltpu.roll(imp, HALF, 1)
        score = jnp.where(valid, imp_sel + jnp.where(forced, FORCE_BONUS, 0.0), NEG_INF)
        score = jnp.where(lo, score, BELOW_NEG_INF)
        sel = jnp.zeros((tq, LANES), F32)
        for _ in range(k_top):
            mx = jnp.max(score, axis=1, keepdims=True)
            idx = jnp.min(jnp.where(score == mx, lane, 4 * LANES), axis=1, keepdims=True)
            hit = lane == idx
            sel = jnp.where(hit, 1.0, sel)
            score = jnp.where(hit, BELOW_NEG_INF, score)
        sel_b = sel.astype(BF16)

        def sel_bias(j, dist_f):
            expand = (blk_row == 2 * j + key_blk).astype(BF16)
            return (_dot(sel_b, expand) - 1.0) * (-NEG_INF)

        def win_bias(j, dist_f):
            return jnp.where(dist_f <= float(WINDOW), 0.0, NEG_INF)

        o_s = attend(qg, g, 2, 3, 0, sel_bias)
        o_w = attend(qg, g, 4, 5, jnp.maximum(qi - WINDOW // tq, 0), win_bias)
        in_group = lo if g == 0 else jnp.logical_not(lo)
        for r in range(NSA_GROUP):
            h = NSA_GROUP * g + r
            rows = slice(r * tq, (r + 1) * tq)
            o = (gates[:, 3 * h:3 * h + 1] * o_c[rows] + gates[:, 3 * h + 1:3 * h + 2] * o_s[rows]
                 + gates[:, 3 * h + 2:3 * h + 3] * o_w[rows])
            o_ref[:, h * LANES:(h + 1) * LANES] = jnp.where(in_group, o, 0.0).astype(BF16)


def _nsa_prompt(q_pad, kvt, cmp_perm, gates, n_seq, t):
    tq = NSA_TQ
    nq = t // tq
    k_top = min(TOP_N, t // SEL_BLOCK)
    return pl.pallas_call(
        functools.partial(_nsa_prompt_kernel, k_top=k_top),
        grid=(n_seq, nq),
        in_specs=[pl.BlockSpec((tq, 1024), lambda b, i: (b * nq + i, 0)),
                  pl.BlockSpec((None, 6, t, LANES), lambda b, i: (b, 0, 0, 0)),
                  pl.BlockSpec((None, LANES, 2 * LANES), lambda b, i: (b, 0, 0)),
                  pl.BlockSpec((tq, LANES), lambda b, i: (b * nq + i, 0))],
        out_specs=pl.BlockSpec((tq, 1024), lambda b, i: (b * nq + i, 0)),
        out_shape=jax.ShapeDtypeStruct((n_seq * t, 1024), BF16),
        scratch_shapes=[pltpu.VMEM((NSA_GROUP * tq, LANES), F32), pltpu.VMEM((NSA_GROUP * tq, LANES), F32)],
        compiler_params=_cparams(("parallel", "parallel")),
        name="nsa_prompt",
    )(q_pad, kvt, cmp_perm, gates)


def _permute_cmp(cmp, n_seq, t):
    nc = t // CMP_BLOCK
    c = cmp.reshape(n_seq, nc // 2, 2, 2 * LANES)
    c = jnp.pad(c, ((0, 0), (0, HALF - nc // 2), (0, 0), (0, 0)))
    return jnp.transpose(c, (0, 2, 1, 3)).reshape(n_seq, LANES, 2 * LANES)


def _compress_pool_kernel(x_ref, w_ref, o_ref):
    pb = x_ref.shape[0]
    page = x_ref.shape[-1]
    nb = page // CMP_BLOCK
    blk = lax.broadcasted_iota(jnp.int32, (8, page), 0)
    seg = (lax.broadcasted_iota(jnp.int32, (8, page), 1) // CMP_BLOCK == blk).astype(BF16)
    for p in range(pb):
        for j in range(2):
            kw = x_ref[p, j].reshape(LANES, page) * w_ref[j]
            k1, k2, k3 = _split3(kw)
            res = _dot_nt(seg, k1) + _dot_nt(seg, k2) + _dot_nt(seg, k3)
            o_ref[p, j * nb:(j + 1) * nb, :] = res[0:nb]


def _compress_pool(cache_t, w_t, pb):
    L, n_pool = cache_t.shape[:2]
    page = cache_t.shape[-1]
    nb = page // CMP_BLOCK
    return pl.pallas_call(
        _compress_pool_kernel,
        grid=(L, n_pool // pb),
        in_specs=[pl.BlockSpec((None, pb, 2, NSA_KV_HEADS, NSA_HEAD_DIM, page), lambda l, i: (l, i, 0, 0, 0, 0)),
                  pl.BlockSpec((None, 2, LANES, page), lambda l, i: (l, 0, 0, 0))],
        out_specs=pl.BlockSpec((None, pb, 2 * nb, LANES), lambda l, i: (l, i, 0, 0)),
        out_shape=jax.ShapeDtypeStruct((L, n_pool, 2 * nb, LANES), F32),
        compiler_params=_cparams(("parallel", "parallel")),
        name="compress_pool",
    )(cache_t, w_t)


def _nsa_dec_cmp_kernel(pt_ref, pool_ref, q_ref, gate_ref, slope_ref, win_ref, kvn_ref, part_ref, idx_ref, gath_ref,
                        *, n_pages, past_len, k_top):
    b = pl.program_id(0)
    t = past_len
    for p in range(n_pages):
        tile = pool_ref[pt_ref[b, p]]
        for r in range(8):
            gath_ref[r, p:p + 1, :] = tile[r:r + 1, :]
    x = [gath_ref[r] for r in range(8)]
    q = q_ref[...]
    slope = slope_ref[:, 0:1]
    gates = gate_ref[...]
    lane = lax.broadcasted_iota(jnp.int32, (1, LANES), 1)
    lane_p = lax.broadcasted_iota(jnp.int32, (1, 2 * n_pages), 1)
    page_of = lane_p % n_pages
    ss, masks = [], []
    for pair in range(2):
        kmat = jnp.concatenate([x[2 * pair], x[2 * pair + 1]], axis=0).astype(BF16)
        blk = page_of * 4 + 2 * pair + lane_p // n_pages
        dist = t - (blk * CMP_BLOCK + (CMP_BLOCK - 1))
        mask = dist >= 0
        s = _dot_nt(q, kmat) - slope * dist.astype(F32)
        ss.append(jnp.where(mask, s, NEG_INF))
        masks.append(mask)
    mx = jnp.maximum(jnp.max(ss[0], axis=1, keepdims=True), jnp.max(ss[1], axis=1, keepdims=True))
    es = [jnp.where(masks[i], jnp.exp(ss[i] - mx), 0.0) for i in range(2)]
    den = jnp.maximum(jnp.sum(es[0], axis=1, keepdims=True) + jnp.sum(es[1], axis=1, keepdims=True), 1e-30)
    ps = [e / den for e in es]
    o_c = jnp.zeros((NSA_HEADS, LANES), F32)
    for pair in range(2):
        vmat = jnp.concatenate([x[4 + 2 * pair], x[5 + 2 * pair]], axis=0).astype(BF16)
        o_c = o_c + _dot(ps[pair].astype(BF16), vmat)
    n_sel_blk = lane_p // n_pages + 2 * page_of
    cur = t // SEL_BLOCK
    forced = (n_sel_blk == 0) | (n_sel_blk == cur) | (n_sel_blk == cur - 1)
    valid = n_sel_blk * SEL_BLOCK <= t
    idx_rows = []
    for g in range(NSA_KV_HEADS):
        imp_a = jnp.sum(ps[0][NSA_GROUP * g:NSA_GROUP * (g + 1)], axis=0, keepdims=True)
        imp_b = jnp.sum(ps[1][NSA_GROUP * g:NSA_GROUP * (g + 1)], axis=0, keepdims=True)
        ev = imp_a + pltpu.roll(imp_a, n_pages, 1)
        od = imp_b + pltpu.roll(imp_b, n_pages, 1)
        imp = jnp.where(lane_p < n_pages, ev, od)
        score = jnp.where(valid, imp + jnp.where(forced, FORCE_BONUS, 0.0), NEG_INF)
        chosen = jnp.where(lane == 0, cur, 0)
        for it in range(1, k_top):
            mxs = jnp.max(score, axis=1, keepdims=True)
            pick = jnp.min(jnp.where(score == mxs, n_sel_blk, 1 << 30), axis=1, keepdims=True)
            score = jnp.where(n_sel_blk == pick, BELOW_NEG_INF, score)
            chosen = jnp.where(lane == it, pick, chosen)
        idx_rows.append(chosen)
    rowi = lax.broadcasted_iota(jnp.int32, (8, LANES), 0)
    idx_ref[...] = jnp.where(rowi == 0, idx_rows[0], jnp.where(rowi == 1, idx_rows[1], 0))
    wl = win_ref.shape[-1]
    kw = win_ref[0].reshape(LANES, wl).astype(BF16)
    vw = win_ref[1].reshape(LANES, wl).astype(BF16)
    col = lax.broadcasted_iota(jnp.int32, (1, wl), 1)
    dist_w = wl - col
    mask_w = (t - dist_w >= 0) & (dist_w <= WINDOW)
    s_w = jnp.where(mask_w, _dot(q, kw) - slope * dist_w.astype(F32), NEG_INF)
    kvn = kvn_ref[...]
    qf = q.astype(F32)
    s_n = jnp.sum(qf * kvn[4:5, :].astype(BF16).astype(F32), axis=1, keepdims=True)
    mw = jnp.maximum(jnp.max(s_w, axis=1, keepdims=True), s_n)
    e_w = jnp.where(mask_w, jnp.exp(s_w - mw), 0.0)
    e_n = jnp.exp(s_n - mw)
    den_w = jnp.maximum(jnp.sum(e_w, axis=1, keepdims=True) + e_n, 1e-30)
    o_w = (_dot_nt((e_w / den_w).astype(BF16), vw)
           + (e_n / den_w).astype(BF16).astype(F32) * kvn[5:6, :].astype(BF16).astype(F32))
    part_ref[...] = gates[:, 0:1] * o_c + gates[:, 2:3] * o_w


def _nsa_dec_sel_kernel(pt_ref, sel_ref, blk_ref, q_ref, gate_ref, slope_ref, kvn_ref, part_ref, o_ref,
                        m_ref, l_ref, acc_ref, *, past_len, k_top):
    b = pl.program_id(0)
    j = pl.program_id(1)
    g = j // k_top
    t = past_len

    @pl.when(j == 0)
    def _():
        m_ref[...] = jnp.full(m_ref.shape, NEG_INF, F32)
        l_ref[...] = jnp.zeros(l_ref.shape, F32)
        acc_ref[...] = jnp.zeros(acc_ref.shape, F32)

    n = sel_ref[b, j]
    q = q_ref[...]
    slope = slope_ref[:, 0:1]
    page = blk_ref.shape[-1]
    halves = page // SEL_BLOCK
    kt = blk_ref[0].reshape(LANES, page).astype(BF16)
    vt = blk_ref[1].reshape(LANES, page).astype(BF16)
    col = lax.broadcasted_iota(jnp.int32, (1, page), 1)
    pos = (n // halves) * page + col
    mask = (col // SEL_BLOCK == n % halves) & (pos < t)
    rowh = lax.broadcasted_iota(jnp.int32, (NSA_HEADS, 1), 0)
    in_g = (rowh // NSA_GROUP) == g
    s = jnp.where(mask, _dot(q, kt) - slope * (t - pos).astype(F32), NEG_INF)
    is_cur = n * SEL_BLOCK + SEL_BLOCK > t
    kvn = kvn_ref[...]
    s_n = jnp.sum(q.astype(F32) * kvn[2:3, :].astype(BF16).astype(F32), axis=1, keepdims=True)
    s_n = jnp.where(is_cur, s_n, NEG_INF)
    m_old = m_ref[...]
    m_new = jnp.maximum(m_old, jnp.maximum(jnp.max(s, axis=1, keepdims=True), s_n))
    alpha = jnp.exp(m_old - m_new)
    e = jnp.where(mask, jnp.exp(s - m_new), 0.0)
    e_n = jnp.where(is_cur, jnp.exp(s_n - m_new), 0.0)
    l_new = alpha * l_ref[...] + jnp.sum(e, axis=1, keepdims=True) + e_n
    acc_new = (alpha * acc_ref[...] + _dot_nt(e.astype(BF16), vt)
               + e_n.astype(BF16).astype(F32) * kvn[3:4, :].astype(BF16).astype(F32))
    m_ref[...] = jnp.where(in_g, m_new, m_old)
    l_ref[...] = jnp.where(in_g, l_new, l_ref[...])
    acc_ref[...] = jnp.where(in_g, acc_new, acc_ref[...])

    @pl.when(j == pl.num_programs(1) - 1)
    def _():
        lane = lax.broadcasted_iota(jnp.int32, (1, LANES), 1)
        o_s = acc_ref[...] / jnp.maximum(l_ref[...], 1e-30)
        o = part_ref[...] + gate_ref[:, 1:2] * o_s
        in_half = (lane // HALF) == (rowh // NSA_GROUP)
        o_ref[...] = jnp.where(in_half, o, 0.0).astype(BF16)


def _nsa_decode(page_table, cmp_pool_l, cache_l, win_l, q_pad, gates, kv_new, slopes, past_len):
    bs, n_pages = page_table.shape
    n_pool, page = cache_l.shape[0], cache_l.shape[-1]
    k_top = min(TOP_N, past_len // SEL_BLOCK + 1)
    q3 = q_pad.reshape(bs, NSA_HEADS, LANES)
    g3 = jnp.pad(gates[:, :3 * NSA_HEADS].reshape(bs, NSA_HEADS, 3), ((0, 0), (0, 0), (0, LANES - 3)))
    kvn = kv_new.reshape(bs, 6, LANES)
    wl = win_l.shape[-1]
    part, idx = pl.pallas_call(
        functools.partial(_nsa_dec_cmp_kernel, n_pages=n_pages, past_len=past_len, k_top=k_top),
        grid_spec=pltpu.PrefetchScalarGridSpec(
            num_scalar_prefetch=1, grid=(bs,),
            in_specs=[pl.BlockSpec((n_pool, 8, LANES), lambda b, pt: (0, 0, 0)),
                      pl.BlockSpec((None, NSA_HEADS, LANES), lambda b, pt: (b, 0, 0)),
                      pl.BlockSpec((None, NSA_HEADS, LANES), lambda b, pt: (b, 0, 0)),
                      pl.BlockSpec((NSA_HEADS, LANES), lambda b, pt: (0, 0)),
                      pl.BlockSpec((None, 2, NSA_KV_HEADS, NSA_HEAD_DIM, wl), lambda b, pt: (b, 0, 0, 0, 0)),
                      pl.BlockSpec((None, 6, LANES), lambda b, pt: (b, 0, 0))],
            out_specs=[pl.BlockSpec((None, NSA_HEADS, LANES), lambda b, pt: (b, 0, 0)),
                       pl.BlockSpec((None, 8, LANES), lambda b, pt: (b, 0, 0))],
            scratch_shapes=[pltpu.VMEM((8, n_pages, LANES), F32)]),
        out_shape=[jax.ShapeDtypeStruct((bs, NSA_HEADS, LANES), F32),
                   jax.ShapeDtypeStruct((bs, 8, LANES), jnp.int32)],
        compiler_params=_cparams(("arbitrary",)),
        name="nsa_dec_cmp",
    )(page_table, cmp_pool_l, q3, g3, slopes, win_l, kvn)
    sel = idx[:, :NSA_KV_HEADS, :k_top].reshape(bs, NSA_KV_HEADS * k_top)
    halves = page // SEL_BLOCK

    def blk_map(b, j, pt, sl):
        pg = pt[b, jnp.minimum(sl[b, j] // halves, n_pages - 1)]
        return (pg, 1, 0, 0, 0)

    out = pl.pallas_call(
        functools.partial(_nsa_dec_sel_kernel, past_len=past_len, k_top=k_top),
        grid_spec=pltpu.PrefetchScalarGridSpec(
            num_scalar_prefetch=2, grid=(bs, NSA_KV_HEADS * k_top),
            in_specs=[pl.BlockSpec((None, 2, NSA_KV_HEADS, NSA_HEAD_DIM, page), blk_map),
                      pl.BlockSpec((None, NSA_HEADS, LANES), lambda b, j, pt, sl: (b, 0, 0)),
                      pl.BlockSpec((None, NSA_HEADS, LANES), lambda b, j, pt, sl: (b, 0, 0)),
                      pl.BlockSpec((NSA_HEADS, LANES), lambda b, j, pt, sl: (0, 0)),
                      pl.BlockSpec((None, 6, LANES), lambda b, j, pt, sl: (b, 0, 0)),
                      pl.BlockSpec((None, NSA_HEADS, LANES), lambda b, j, pt, sl: (b, 0, 0))],
            out_specs=pl.BlockSpec((None, NSA_HEADS, LANES), lambda b, j, pt, sl: (b, 0, 0)),
            scratch_shapes=[pltpu.VMEM((NSA_HEADS, 1), F32), pltpu.VMEM((NSA_HEADS, 1), F32),
                            pltpu.VMEM((NSA_HEADS, LANES), F32)]),
        out_shape=jax.ShapeDtypeStruct((bs, NSA_HEADS, LANES), BF16),
        compiler_params=_cparams(("arbitrary", "arbitrary")),
        name="nsa_dec_sel",
    )(page_table, sel, cache_l, q3, g3, slopes, kvn, part)
    return out.reshape(bs, NSA_HEADS * LANES)


def _pool_tail(sums, cur, cnts, pw_ref, scale_ref, o_ref):
    for gi in range(len(POOL_WINDOWS)):
        sl = slice(gi * LANES, (gi + 1) * LANES)
        d = sums[gi] / cnts[gi] - cur[:, sl]
        y = _dot(d.astype(BF16), pw_ref[gi])
        o_ref[:, sl] = (y * scale_ref[:, sl]).astype(BF16)


def _pool_seq_kernel(p_ref, pprev_ref, hist_ref, pw_ref, scale_ref, o_ref, zz_ref, *, seq_tiles, tm, start_pos):
    i = pl.program_id(0)
    first = (i % seq_tiles) == 0
    cur = p_ref[...]
    zz_ref[0:16, :] = jnp.where(first, hist_ref[...], pprev_ref[...])
    zz_ref[16:16 + tm, :] = cur
    pos = start_pos + (i % seq_tiles) * tm + lax.broadcasted_iota(jnp.int32, (tm, 1), 0)
    sums, cnts = [], []
    for gi, w in enumerate(POOL_WINDOWS):
        sl = slice(gi * LANES, (gi + 1) * LANES)
        s = cur[:, sl]
        for k in range(1, w):
            s = s + zz_ref[16 - k:16 - k + tm, sl]
        sums.append(s)
        cnts.append(jnp.minimum(pos + 1, w).astype(F32))
    _pool_tail(sums, cur, cnts, pw_ref, scale_ref, o_ref)


def _pool_step_kernel(z_ref, pw_ref, scale_ref, o_ref, *, start_pos):
    z = z_ref[...]
    cur = z[:, 15, :]
    sums, cnts = [], []
    for gi, w in enumerate(POOL_WINDOWS):
        sl = slice(gi * LANES, (gi + 1) * LANES)
        sums.append(jnp.sum(z[:, 16 - w:16, sl], axis=1))
        cnts.append(float(min(start_pos + 1, w)))
    _pool_tail(sums, cur, cnts, pw_ref, scale_ref, o_ref)


def _pool_seq(p_pool, hist16, pool_w, pool_scale, t, tm, start_pos):
    n = p_pool.shape[0]
    tps = t // tm
    return pl.pallas_call(
        functools.partial(_pool_seq_kernel, seq_tiles=tps, tm=tm, start_pos=start_pos),
        grid=(n // tm,),
        in_specs=[pl.BlockSpec((tm, 512), lambda i: (i, 0)),
                  pl.BlockSpec((16, 512), lambda i: (jnp.maximum(i * (tm // 16) - 1, 0), 0)),
                  pl.BlockSpec((None, 16, 512), lambda i: (i // tps, 0, 0)),
                  pl.BlockSpec((4, LANES, LANES), lambda i: (0, 0, 0)),
                  pl.BlockSpec((1, 512), lambda i: (0, 0))],
        out_specs=pl.BlockSpec((tm, 512), lambda i: (i, 0)),
        out_shape=jax.ShapeDtypeStruct((n, 512), BF16),
        scratch_shapes=[pltpu.VMEM((tm + 16, 512), F32)],
        compiler_params=_cparams(("parallel",)),
        name="pool_seq",
    )(p_pool, p_pool, hist16, pool_w, pool_scale)


def _pool_step(z16, pool_w, pool_scale, start_pos):
    bs = z16.shape[0]
    return pl.pallas_call(
        functools.partial(_pool_step_kernel, start_pos=start_pos),
        grid=(1,),
        in_specs=[pl.BlockSpec((bs, 16, 512), lambda i: (0, 0, 0)),
                  pl.BlockSpec((4, LANES, LANES), lambda i: (0, 0, 0)),
                  pl.BlockSpec((1, 512), lambda i: (0, 0))],
        out_specs=pl.BlockSpec((bs, 512), lambda i: (0, 0)),
        out_shape=jax.ShapeDtypeStruct((bs, 512), BF16),
        compiler_params=_cparams(("arbitrary",)),
        name="pool_step",
    )(z16, pool_w, pool_scale)


def _merge_kernel(x_ref, sc1_ref, sh1_ref, g1_ref, sc2_ref, sh2_ref, orw_ref, onsa_ref, opool_ref,
                  wg_ref, wbr_ref, wbn_ref, wbp_ref, wout_ref, lng_ref, lnb_ref, wr_ref, br_ref,
                  x1_ref, u2_ref, route_ref, *, alpha):
    x = x_ref[...]
    d = x.shape[1]
    u = (x * (1.0 + sc1_ref[...]) + sh1_ref[...]).astype(BF16)
    mixed = jnp.zeros(x.shape, F32)
    for bi, (o_ref, wb_ref) in enumerate(((orw_ref, wbr_ref), (onsa_ref, wbn_ref), (opool_ref, wbp_ref))):
        gate = _sigmoid(_dot(u, wg_ref[:, bi * d:(bi + 1) * d]))
        mixed = mixed + gate * _dot(o_ref[...], wb_ref[...])
    m = _dot(mixed.astype(BF16), wout_ref[...])
    x1 = _layer_norm(alpha * x + (1.0 + g1_ref[...]) * m, lng_ref[...], lnb_ref[...])
    x1_ref[...] = x1
    u2 = x1 * (1.0 + sc2_ref[...]) + sh2_ref[...]
    u2b = u2.astype(BF16)
    u2_ref[...] = u2b
    lg = _dot(u2b, wr_ref[...]) + br_ref[...]
    lane = lax.broadcasted_iota(jnp.int32, (1, LANES), 1)
    is_g = lane < MOE_GROUPS
    mg = jnp.max(jnp.where(is_g, lg, BELOW_NEG_INF), axis=1, keepdims=True)
    gsel = jnp.min(jnp.where(is_g & (lg == mg), lane, LANES), axis=1, keepdims=True)
    wgrp = 1.0 / jnp.sum(jnp.where(is_g, jnp.exp(lg - mg), 0.0), axis=1, keepdims=True)
    e_lane = lane - ROUTE_LANE0
    in_grp = (e_lane >= 0) & (e_lane < MOE_GROUPS * EXPERTS_PER_GROUP) & ((e_lane // EXPERTS_PER_GROUP) == gsel)
    v1 = jnp.max(jnp.where(in_grp, lg, BELOW_NEG_INF), axis=1, keepdims=True)
    i1 = jnp.min(jnp.where(in_grp & (lg == v1), lane, LANES), axis=1, keepdims=True)
    rest = in_grp & (lane != i1)
    v2 = jnp.max(jnp.where(rest, lg, BELOW_NEG_INF), axis=1, keepdims=True)
    i2 = jnp.min(jnp.where(rest & (lg == v2), lane, LANES), axis=1, keepdims=True)
    e2 = jnp.exp(v2 - v1)
    w1 = wgrp / (1.0 + e2)
    w2 = wgrp * e2 / (1.0 + e2)
    for g in range(MOE_GROUPS):
        src = lane + (ROUTE_LANE0 + EXPERTS_PER_GROUP * g)
        route_ref[g] = jnp.where(lane < EXPERTS_PER_GROUP,
                                 jnp.where(src == i1, w1, jnp.where(src == i2, w2, 0.0)), 0.0)


def _merge(x, mods, o_rw, o_nsa, o_pool, wts, rows_per_group, tm, alpha):
    n, d = x.shape
    full = lambda a: pl.BlockSpec(a.shape, lambda i: (0,) * a.ndim)
    row = lambda c: pl.BlockSpec((tm, c), lambda i: (i, 0))
    return pl.pallas_call(
        functools.partial(_merge_kernel, alpha=alpha),
        grid=(n // tm,),
        in_specs=[row(d)] + [_mod_spec(m, rows_per_group, tm) for m in mods]
                 + [row(512), row(1024), row(512)] + [full(w) for w in wts],
        out_specs=[row(d), row(d), pl.BlockSpec((MOE_GROUPS, tm, LANES), lambda i: (0, i, 0))],
        out_shape=[jax.ShapeDtypeStruct((n, d), F32), jax.ShapeDtypeStruct((n, d), BF16),
                   jax.ShapeDtypeStruct((MOE_GROUPS, n, LANES), F32)],
        compiler_params=_cparams(("parallel",), 56),
        name="merge",
    )(x, *mods, o_rw, o_nsa, o_pool, *wts)


def _moe_kernel(x1_ref, u2_ref, route_ref, g2_ref, wg_ref, wu_ref, wd_ref, lng_ref, lnb_ref, o_ref, acc_ref, *, alpha):
    g = pl.program_id(1)

    @pl.when(g == 0)
    def _():
        acc_ref[...] = jnp.zeros(acc_ref.shape, F32)

    u = u2_ref[...]
    hg = _dot(u, wg_ref[...])
    h = (hg * _sigmoid(hg) * _dot(u, wu_ref[...])).astype(BF16)
    route = route_ref[...]
    y = acc_ref[...]
    for e in range(EXPERTS_PER_GROUP):
        w = route[:, e:e + 1]
        ye = _dot(h[:, e * EXPERT_HIDDEN:(e + 1) * EXPERT_HIDDEN], wd_ref[e * EXPERT_HIDDEN:(e + 1) * EXPERT_HIDDEN, :])
        y = y + jnp.where(w != 0.0, w * ye, 0.0)
    acc_ref[...] = y

    @pl.when(g == pl.num_programs(1) - 1)
    def _():
        h2 = alpha * x1_ref[...] + (1.0 + g2_ref[...]) * acc_ref[...]
        o_ref[...] = _layer_norm(h2, lng_ref[...], lnb_ref[...])


def _moe(x1, u2, route, g2, w_gate, w_up, w_down, ln_g, ln_b, rows_per_group, tm, alpha):
    n, d = x1.shape
    hid = EXPERTS_PER_GROUP * EXPERT_HIDDEN
    r = g2.shape[1]
    return pl.pallas_call(
        functools.partial(_moe_kernel, alpha=alpha),
        grid=(n // tm, MOE_GROUPS),
        in_specs=[pl.BlockSpec((tm, d), lambda i, g: (i, 0)),
                  pl.BlockSpec((tm, d), lambda i, g: (i, 0)),
                  pl.BlockSpec((None, tm, LANES), lambda i, g: (g, i, 0)),
                  pl.BlockSpec((None, r, d), lambda i, g: ((i * tm) // rows_per_group, 0, 0)),
                  pl.BlockSpec((None, d, hid), lambda i, g: (g, 0, 0)),
                  pl.BlockSpec((None, d, hid), lambda i, g: (g, 0, 0)),
                  pl.BlockSpec((None, hid, d), lambda i, g: (g, 0, 0)),
                  pl.BlockSpec((1, d), lambda i, g: (0, 0)),
                  pl.BlockSpec((1, d), lambda i, g: (0, 0))],
        out_specs=pl.BlockSpec((tm, d), lambda i, g: (i, 0)),
        out_shape=jax.ShapeDtypeStruct((n, d), F32),
        scratch_shapes=[pltpu.VMEM((tm, d), F32)],
        compiler_params=_cparams(("parallel", "arbitrary")),
        name="moe",
    )(x1, u2, route, g2, w_gate, w_up, w_down, ln_g, ln_b)


def _pad_heads(w, axis):
    shp = w.shape
    w = w.reshape(shp[:axis] + (NSA_KV_HEADS, NSA_GROUP, 1, NSA_HEAD_DIM) + shp[axis + 1:])
    sel = jnp.eye(NSA_KV_HEADS, dtype=w.dtype).reshape((1,) * axis + (NSA_KV_HEADS, 1, NSA_KV_HEADS, 1) + (1,) * (len(shp) - axis - 1))
    w = w * sel
    return w.reshape(shp[:axis] + (NSA_HEADS * LANES,) + shp[axis + 1:])


def _prep_weights(w_in, rw_w_up, rw_a_up, w_branch, moe_router_g, moe_bias_g, moe_router_e, moe_bias_e,
                  moe_w_gate, moe_w_up, moe_w_down, nsa_cmp_wk, nsa_cmp_wv):
    L, D, _ = w_in.shape
    w = {}
    w['rw'] = w_in[:, :, :RW_END].astype(BF16)
    wq = _pad_heads(w_in[:, :, RW_END:NSA_Q_END] * (NSA_HEAD_DIM ** -0.5), 2)
    wg = jnp.pad(w_in[:, :, NSA_KV_END:NSA_END], ((0, 0), (0, 0), (0, LANES - 3 * NSA_HEADS)))
    w['nsa'] = jnp.concatenate([wq, w_in[:, :, NSA_Q_END:NSA_KV_END], wg], axis=-1).astype(BF16)
    w['pool'] = w_in[:, :, NSA_END:POOL_END].astype(BF16)
    w['gate'] = w_in[:, :, POOL_END:].astype(BF16)
    z = jnp.zeros((L, HALF, RW_WIDTH), F32)
    w['wwa'] = jnp.concatenate([jnp.concatenate([rw_w_up, z], axis=2),
                                jnp.concatenate([z, rw_a_up], axis=2)], axis=1).astype(BF16)
    w['b_rw'] = w_branch[:, 0].astype(BF16)
    w['b_nsa'] = _pad_heads(w_branch[:, 1], 1).astype(BF16)
    w['b_pool'] = w_branch[:, 2].astype(BF16)
    pad_r = LANES - MOE_GROUPS - MOE_GROUPS * EXPERTS_PER_GROUP
    w['router'] = jnp.pad(jnp.concatenate([moe_router_g, moe_router_e], axis=2), ((0, 0), (0, 0), (0, pad_r))).astype(BF16)
    w['router_b'] = jnp.pad(jnp.concatenate([moe_bias_g, moe_bias_e], axis=1), ((0, 0), (0, pad_r)))[:, None, :]
    hid = EXPERTS_PER_GROUP * EXPERT_HIDDEN
    grp = lambda a: jnp.transpose(a.reshape(L, MOE_GROUPS, EXPERTS_PER_GROUP, D, EXPERT_HIDDEN),
                                  (0, 1, 3, 2, 4)).reshape(L, MOE_GROUPS, D, hid).astype(BF16)
    w['moe_gate'] = grp(moe_w_gate)
    w['moe_up'] = grp(moe_w_up)
    w['moe_down'] = moe_w_down.reshape(L, MOE_GROUPS, hid, D).astype(BF16)
    w['cmp_wk'] = jnp.tile(nsa_cmp_wk, (1, 1, NSA_KV_HEADS))
    w['cmp_wv'] = jnp.tile(nsa_cmp_wv, (1, 1, NSA_KV_HEADS))
    return w


def kernel(x_prompt, x_sample, cache_nsa_kv, cache_win_kv, state_rwkv, state_rwkv_shift, state_pool, page_table, c_prompt, c_sample, ada_w, ada_b, w_in, rw_mu, rw_w0, rw_w_up, rw_a0, rw_a_up, rw_g_up, rw_k_k, rw_k_a, rw_r_k, rw_gn_g, rw_gn_b, nsa_cmp_wk, nsa_cmp_wv, pool_w, pool_scale, w_branch, w_out, ln1_g, ln1_b, moe_router_g, moe_bias_g, moe_router_e, moe_bias_e, moe_w_gate, moe_w_up, moe_w_down, ln2_g, ln2_b):
    bp, t, d = x_prompt.shape
    bs, ts, _ = x_sample.shape
    L = ada_w.shape[0]
    n_pool, page = cache_nsa_kv.shape[1], cache_nsa_kv.shape[2]
    n_pages = page_table.shape[1]
    past_len = n_pages * page
    assert ts == 1 and d == D_MODEL
    assert t % 256 == 0 and t // CMP_BLOCK <= LANES and past_len % SEL_BLOCK == 0
    assert cache_win_kv.shape[2] == WINDOW and bs % 8 == 0 and 2 * n_pages == LANES
    alpha = (2 * L) ** 0.25
    tm = 256
    chunk_pad = RW_CHUNK

    w = _prep_weights(w_in, rw_w_up, rw_a_up, w_branch, moe_router_g, moe_bias_g, moe_router_e, moe_bias_e,
                      moe_w_gate, moe_w_up, moe_w_down, nsa_cmp_wk, nsa_cmp_wv)
    w_out_b = w_out.astype(BF16)
    pool_w_b = pool_w.astype(BF16)
    gup_b = rw_g_up.astype(BF16)

    nb = bp + bs
    nb_pad = -(-nb // 8) * 8
    c_all = jnp.pad(jnp.concatenate([c_prompt, c_sample], axis=0), ((0, nb_pad - nb), (0, 0)))
    mods = _ada_mod(c_all, ada_w, ada_b)

    cache_t = jnp.transpose(cache_nsa_kv, (0, 1, 3, 4, 5, 2)).astype(F32)
    win_cache = jnp.transpose(cache_win_kv, (0, 1, 3, 4, 5, 2)).astype(F32)
    pb = 8 if n_pool % 8 == 0 else 1
    cmp_w_t = jnp.tile(jnp.transpose(jnp.stack([nsa_cmp_wk, nsa_cmp_wv], axis=1), (0, 1, 3, 2)),
                       (1, 1, NSA_KV_HEADS, page // CMP_BLOCK))
    cmp_pool = _compress_pool(cache_t, cmp_w_t, pb)
    slopes = jnp.broadcast_to(jnp.asarray(NSA_SLOPES, F32)[:, None], (NSA_HEADS, LANES))

    xp = x_prompt.reshape(bp * t, d)
    xs = x_sample.reshape(bs, d)
    zeros_prev = jnp.zeros((bp, RW_COLS), F32)
    zeros_hist = jnp.zeros((bp, 16, POOL_WIDTH), F32)
    outs = {k: [] for k in ('nsa_p', 'nsa_s', 'win_p', 'win_s', 'rw_p', 'rw_s', 'sh_p', 'sh_s', 'pool_p', 'pool_s')}

    for l in range(L):
        vec = lambda a: a[l].reshape(1, -1)
        rw_wts = (vec(rw_mu), vec(rw_w0), vec(rw_a0), w['wwa'][l], gup_b[l], vec(rw_k_k), vec(rw_k_a), vec(rw_r_k))
        merge_wts = (w['gate'][l], w['b_rw'][l], w['b_nsa'][l], w['b_pool'][l], w_out_b[l], vec(ln1_g), vec(ln1_b),
                     w['router'][l], w['router_b'][l])
        mod_p = [m[:, None, :] for m in jnp.split(mods[l, :bp], 6, axis=-1)]
        mod_s = [m[None] for m in jnp.split(mods[l, bp:bp + bs], 6, axis=-1)]

        p_rw = _proj(xp, mod_p[1], mod_p[0], w['rw'][l], t, tm)
        q_pad, kv, gates, kvt, cmp = _proj_nsa(xp, mod_p[1], mod_p[0], w['nsa'][l], w['cmp_wk'][l], w['cmp_wv'][l], bp, t, tm)
        p_pool = _proj(xp, mod_p[1], mod_p[0], w['pool'][l], t, tm)
        seqs = _rwkv_prep(p_rw, zeros_prev, rw_wts, t, tm)
        o_rw, s_pairs = _rwkv_chunk(seqs, vec(rw_gn_g), vec(rw_gn_b), bp, t)
        o_nsa = _nsa_prompt(q_pad, kvt, _permute_cmp(cmp, bp, t), gates, bp, t)
        o_pool = _pool_seq(p_pool, zeros_hist, pool_w_b[l], vec(pool_scale), t, tm, 0)
        x1, u2, route = _merge(xp, (mod_p[1], mod_p[0], mod_p[2], mod_p[4], mod_p[3]), o_rw, o_nsa, o_pool,
                               merge_wts, t, tm, alpha)
        xp = _moe(x1, u2, route, mod_p[5], w['moe_gate'][l], w['moe_up'][l], w['moe_down'][l],
                  vec(ln2_g), vec(ln2_b), t, tm, alpha)
        kv3 = kv.reshape(bp, t, 6, NSA_KV_HEADS, NSA_HEAD_DIM)
        outs['nsa_p'].append(kv3[:, :, :4])
        outs['win_p'].append(kv3[:, t - min(WINDOW, t):, 4:])
        outs['rw_p'].append(_state_from_pairs(s_pairs))
        outs['sh_p'].append(p_rw.reshape(bp, t, RW_COLS)[:, -1])
        outs['pool_p'].append(p_pool.reshape(bp, t, POOL_WIDTH)[:, t - POOL_HIST:])

        p_rw = _proj(xs, mod_s[1], mod_s[0], w['rw'][l], bs, bs)
        q_pad, kv, gates = _proj_nsa(xs, mod_s[1], mod_s[0], w['nsa'][l], w['cmp_wk'][l], w['cmp_wv'][l], bs, 1, bs)
        p_pool = _proj(xs, mod_s[1], mod_s[0], w['pool'][l], bs, bs)
        seqs = _rwkv_prep(p_rw, state_rwkv_shift[l], rw_wts, 1, bs)
        o_rw, s_pairs = _rwkv_step(seqs, _state_to_pairs(state_rwkv[l]), vec(rw_gn_g), vec(rw_gn_b))
        o_nsa = _nsa_decode(page_table, cmp_pool[l], cache_t[l], win_cache[l], q_pad, gates, kv, slopes, past_len)
        z16 = jnp.concatenate([state_pool[l].astype(F32), p_pool[:, None, :]], axis=1)
        o_pool = _pool_step(z16, pool_w_b[l], vec(pool_scale), past_len)
        x1, u2, route = _merge(xs, (mod_s[1], mod_s[0], mod_s[2], mod_s[4], mod_s[3]), o_rw, o_nsa, o_pool,
                               merge_wts, bs, bs, alpha)
        xs = _moe(x1, u2, route, mod_s[5], w['moe_gate'][l], w['moe_up'][l], w['moe_down'][l],
                  vec(ln2_g), vec(ln2_b), bs, bs, alpha)
        kv3 = kv.reshape(bs, 1, 6, NSA_KV_HEADS, NSA_HEAD_DIM)
        outs['nsa_s'].append(kv3[:, :, :4])
        outs['win_s'].append(jnp.concatenate([cache_win_kv[l].astype(F32), kv3[:, :, 4:]], axis=1)[:, -WINDOW:])
        outs['rw_s'].append(_state_from_pairs(s_pairs))
        outs['sh_s'].append(p_rw)
        outs['pool_s'].append(z16[:, 1:])

    st = lambda k: jnp.stack(outs[k])
    return (xp.reshape(bp, t, d), xs.reshape(bs, 1, d), st('nsa_p'), st('nsa_s'), st('win_p'), st('win_s'),
            st('rw_p'), st('rw_s'), st('sh_p'), st('sh_s'), st('pool_p'), st('pool_s'))
```

```python
import functools

import jax
import jax.numpy as jnp
from jax import lax
from jax.experimental import pallas as pl
from jax.experimental.pallas import tpu as pltpu

F32 = jnp.float32
BF16 = jnp.bfloat16

D_MODEL = 1024
RW_HEADS = 8
RW_HEAD_DIM = 64
RW_WIDTH = 512
RW_COLS = 1792
RW_GN_EPS = 64e-5
RW_CHUNK = 64

NSA_HEADS = 8
NSA_KV_HEADS = 2
NSA_HEAD_DIM = 64
NSA_GROUP = 4
CMP_BLOCK = 32
SEL_BLOCK = 64
TOP_N = 8
WINDOW = 512
NSA_TQ = 128
NSA_TK = 512
NEG_INF = -1e30
BELOW_NEG_INF = -3e38
FORCE_BONUS = 1e9
NSA_SLOPES = tuple(2.0 ** (-8.0 * (h + 1) / NSA_HEADS) for h in range(NSA_HEADS))

POOL_WINDOWS = (2, 4, 8, 16)
POOL_HIST = 15
POOL_WIDTH = 512

RW_END = 1792
NSA_Q_END = RW_END + 512
NSA_KV_END = NSA_Q_END + 768
NSA_END = NSA_KV_END + 24
POOL_END = NSA_END + 512

MOE_GROUPS = 4
EXPERTS_PER_GROUP = 4
EXPERT_HIDDEN = 256
ROUTE_LANE0 = 4

LN_EPS = 1e-5
LANES = 128
HALF = 64


def _cparams(sem, vmem_mb=48):
    return pltpu.CompilerParams(dimension_semantics=sem, vmem_limit_bytes=vmem_mb * 1024 * 1024)


def _dot(a, b):
    return jnp.dot(a, b, preferred_element_type=F32)


def _dot_nt(a, b):
    return lax.dot_general(a, b, (((1,), (1,)), ((), ())), preferred_element_type=F32)


def _dot_tn(a, b):
    return lax.dot_general(a, b, (((0,), (0,)), ((), ())), preferred_element_type=F32)


def _sigmoid(x):
    return 1.0 / (1.0 + jnp.exp(-x))


def _softplus(x):
    return jnp.maximum(x, 0.0) + jnp.log(1.0 + jnp.exp(-jnp.abs(x)))


def _layer_norm(h, g, b):
    mu = jnp.mean(h, axis=-1, keepdims=True)
    d = h - mu
    var = jnp.mean(d * d, axis=-1, keepdims=True)
    return d * lax.rsqrt(var + LN_EPS) * g + b


def _half_sum(x, lo):
    s_lo = jnp.sum(jnp.where(lo, x, 0.0), axis=1, keepdims=True)
    s_hi = jnp.sum(jnp.where(lo, 0.0, x), axis=1, keepdims=True)
    return jnp.where(lo, s_lo, s_hi)


def _ada_kernel(c_ref, w_ref, b_ref, o_ref):
    c = c_ref[...]
    s = (c * _sigmoid(c)).astype(BF16)
    o_ref[...] = _dot(s, w_ref[...].astype(BF16)) + b_ref[...]


def _ada_mod(c_all, ada_w, ada_b):
    L, D, D6 = ada_w.shape
    nb = c_all.shape[0]
    return pl.pallas_call(
        _ada_kernel,
        grid=(L, D6 // D),
        in_specs=[pl.BlockSpec((nb, D), lambda l, j: (0, 0)),
                  pl.BlockSpec((None, D, D), lambda l, j: (l, 0, j)),
                  pl.BlockSpec((None, 1, D), lambda l, j: (l, 0, j))],
        out_specs=pl.BlockSpec((None, nb, D), lambda l, j: (l, 0, j)),
        out_shape=jax.ShapeDtypeStruct((L, nb, D6), F32),
        compiler_params=_cparams(("parallel", "parallel")),
        name="ada_mod",
    )(c_all, ada_w, ada_b.reshape(L, 1, D6))


def _mod_spec(mod, rows_per_group, tm):
    r = mod.shape[1]
    return pl.BlockSpec((None, r, mod.shape[2]), lambda i: ((i * tm) // rows_per_group, 0, 0))


def _proj_kernel(x_ref, sc_ref, sh_ref, w_ref, o_ref):
    u = (x_ref[...] * (1.0 + sc_ref[...]) + sh_ref[...]).astype(BF16)
    o_ref[...] = _dot(u, w_ref[...])


def _proj(x, sc, sh, w, rows_per_group, tm):
    n, d = x.shape
    nc = w.shape[1]
    return pl.pallas_call(
        _proj_kernel,
        grid=(n // tm,),
        in_specs=[pl.BlockSpec((tm, d), lambda i: (i, 0)),
                  _mod_spec(sc, rows_per_group, tm), _mod_spec(sh, rows_per_group, tm),
                  pl.BlockSpec((d, nc), lambda i: (0, 0))],
        out_specs=pl.BlockSpec((tm, nc), lambda i: (i, 0)),
        out_shape=jax.ShapeDtypeStruct((n, nc), F32),
        compiler_params=_cparams(("parallel",)),
        name="proj",
    )(x, sc, sh, w)


def _proj_nsa_kernel(x_ref, sc_ref, sh_ref, w_ref, wk_ref, wv_ref, q_ref, kv_ref, gate_ref, kvt_ref=None, cmp_ref=None):
    u = (x_ref[...] * (1.0 + sc_ref[...]) + sh_ref[...]).astype(BF16)
    res = _dot(u, w_ref[...])
    q_ref[...] = res[:, :1024].astype(BF16)
    kv = res[:, 1024:1792]
    kv_ref[...] = kv
    gate_ref[...] = _sigmoid(res[:, 1792:1920])
    if kvt_ref is None:
        return
    for j in range(6):
        kvt_ref[j] = kv[:, j * LANES:(j + 1) * LANES].astype(BF16)
    nb = kv.shape[0] // CMP_BLOCK
    kc = jnp.sum(kv[:, 0:LANES].reshape(nb, CMP_BLOCK, LANES) * wk_ref[...][None], axis=1)
    vc = jnp.sum(kv[:, LANES:2 * LANES].reshape(nb, CMP_BLOCK, LANES) * wv_ref[...][None], axis=1)
    cmp_ref[:, 0:LANES] = kc
    cmp_ref[:, LANES:2 * LANES] = vc


def _proj_nsa(x, sc, sh, w, wk, wv, n_seq, t, tm):
    n, d = x.shape
    out_specs = [pl.BlockSpec((tm, 1024), lambda i: (i, 0)),
                 pl.BlockSpec((tm, 768), lambda i: (i, 0)),
                 pl.BlockSpec((tm, LANES), lambda i: (i, 0))]
    out_shape = [jax.ShapeDtypeStruct((n, 1024), BF16),
                 jax.ShapeDtypeStruct((n, 768), F32),
                 jax.ShapeDtypeStruct((n, LANES), F32)]
    if t > 1:
        tps = t // tm
        nb = tm // CMP_BLOCK
        out_specs += [pl.BlockSpec((None, 6, tm, LANES), lambda i: (i // tps, 0, i % tps, 0)),
                      pl.BlockSpec((nb, 2 * LANES), lambda i: (i, 0))]
        out_shape += [jax.ShapeDtypeStruct((n_seq, 6, t, LANES), BF16),
                      jax.ShapeDtypeStruct((n // CMP_BLOCK, 2 * LANES), F32)]
    rows_per_group = t if sc.shape[1] == 1 else n
    return pl.pallas_call(
        _proj_nsa_kernel,
        grid=(n // tm,),
        in_specs=[pl.BlockSpec((tm, d), lambda i: (i, 0)),
                  _mod_spec(sc, rows_per_group, tm), _mod_spec(sh, rows_per_group, tm),
                  pl.BlockSpec((d, 1920), lambda i: (0, 0)),
                  pl.BlockSpec((CMP_BLOCK, LANES), lambda i: (0, 0)),
                  pl.BlockSpec((CMP_BLOCK, LANES), lambda i: (0, 0))],
        out_specs=out_specs,
        out_shape=out_shape,
        compiler_params=_cparams(("parallel",)),
        name="proj_nsa",
    )(x, sc, sh, w, wk, wv)


def _rwkv_prep_body(p, prev, mu_ref, w0_ref, a0_ref, wwa_ref, gup_ref, kkw_ref, kaw_ref, rk_ref, outs):
    r_ref, lw_ref, k_ref, v_ref, kk_ref, kka_ref, g_ref, bonus_ref = outs
    xs = p + (prev - p) * mu_ref[...]
    r = xs[:, 0:512]
    xk = xs[:, 512:1024]
    v = xs[:, 1024:1536]
    t12 = xs[:, 1536:1664]
    gd = xs[:, 1664:1792]
    lane = lax.broadcasted_iota(jnp.int32, (1, LANES), 1)
    lo = lane < HALF
    z = jnp.where(lo, jnp.tanh(t12), t12).astype(BF16)
    dwa = _dot(z, wwa_ref[...])
    w_log = -_softplus(-(w0_ref[...] + dwa[:, :512])) - 0.5
    a = _sigmoid(a0_ref[...] + dwa[:, 512:])
    g_ref[...] = _dot(_sigmoid(gd).astype(BF16), gup_ref[...])
    kmod = xk * (1.0 + (a - 1.0) * kaw_ref[...])
    kkr = xk * kkw_ref[...]
    rkr = r * kmod * rk_ref[...]
    for m in range(4):
        sl = slice(m * LANES, (m + 1) * LANES)
        x = kkr[:, sl]
        nrm = jnp.sqrt(_half_sum(x * x, lo))
        kk = x / jnp.maximum(nrm, 1e-12)
        kk_ref[:, sl] = kk
        kka_ref[:, sl] = kk * a[:, sl]
        bonus_ref[:, sl] = _half_sum(rkr[:, sl], lo) * v[:, sl]
    r_ref[...] = r
    lw_ref[...] = -jnp.exp(w_log)
    k_ref[...] = kmod
    v_ref[...] = v


def _rwkv_prep_seq_kernel(p_ref, pprev_ref, prow_ref, *rest, seq_tiles):
    i = pl.program_id(0)
    p = p_ref[...]
    first = (i % seq_tiles) == 0
    prev_row = jnp.where(first, prow_ref[...], pprev_ref[7:8, :])
    rolled = pltpu.roll(p, 1, 0)
    rowid = lax.broadcasted_iota(jnp.int32, (p.shape[0], 1), 0)
    prev = jnp.where(rowid == 0, prev_row, rolled)
    _rwkv_prep_body(p, prev, *rest[:8], rest[8:])


def _rwkv_prep_step_kernel(p_ref, prev_ref, *rest):
    _rwkv_prep_body(p_ref[...], prev_ref[...], *rest[:8], rest[8:])


def _rwkv_prep(p_rw, prev_rows, wts, t, tm):
    n = p_rw.shape[0]
    vec = lambda c: pl.BlockSpec((1, c), lambda i: (0, 0))
    w_specs = [vec(RW_COLS), vec(512), vec(512), pl.BlockSpec((LANES, 1024), lambda i: (0, 0)),
               pl.BlockSpec((LANES, 512), lambda i: (0, 0)), vec(512), vec(512), vec(512)]
    out_specs = [pl.BlockSpec((tm, 512), lambda i: (i, 0))] * 8
    out_shape = [jax.ShapeDtypeStruct((n, 512), F32)] * 8
    if t == 1:
        kern = _rwkv_prep_step_kernel
        in_specs = [pl.BlockSpec((tm, RW_COLS), lambda i: (i, 0)), pl.BlockSpec((tm, RW_COLS), lambda i: (i, 0))]
        args = (p_rw, prev_rows)
    else:
        tps = t // tm
        kern = functools.partial(_rwkv_prep_seq_kernel, seq_tiles=tps)
        in_specs = [pl.BlockSpec((tm, RW_COLS), lambda i: (i, 0)),
                    pl.BlockSpec((8, RW_COLS), lambda i: (jnp.maximum(i * (tm // 8) - 1, 0), 0)),
                    pl.BlockSpec((None, 1, RW_COLS), lambda i: (i // tps, 0, 0))]
        args = (p_rw, p_rw, prev_rows[:, None, :])
    return pl.pallas_call(
        kern, grid=(n // tm,), in_specs=in_specs + w_specs, out_specs=out_specs, out_shape=out_shape,
        compiler_params=_cparams(("parallel",)), name="rwkv_prep",
    )(*args, *wts)


def _split3(x):
    x1 = x.astype(BF16)
    r1 = x - x1.astype(F32)
    x2 = r1.astype(BF16)
    x3 = (r1 - x2.astype(F32)).astype(BF16)
    return x1, x2, x3


def _rwkv_chunk_kernel(r_ref, lw_ref, k_ref, v_ref, kk_ref, kka_ref, g_ref, bonus_ref, gng_ref, gnb_ref,
                       o_ref, sout_ref, s_ref):
    c = pl.program_id(1)
    C = RW_CHUNK

    @pl.when(c == 0)
    def _():
        s_ref[...] = jnp.zeros(s_ref.shape, F32)

    lw = lw_ref[...]
    row = lax.broadcasted_iota(jnp.int32, (C, C), 0)
    col = lax.broadcasted_iota(jnp.int32, (C, C), 1)
    tri = (row >= col).astype(BF16)
    l1, l2, l3 = _split3(lw)
    cl = _dot(tri, l1) + _dot(tri, l2) + _dot(tri, l3)
    cl_last = cl[C - 1:C, :]
    g_in = jnp.exp(cl)
    g_ex = jnp.exp(cl - lw)
    g_inv = jnp.exp(-cl)
    g_rem = jnp.exp(cl_last - cl)
    gc = jnp.exp(cl_last)
    k = k_ref[...]
    kka = kka_ref[...]
    qk_all = kk_ref[...] * g_ex
    r_all = r_ref[...] * g_in
    kt_all = k * g_inv
    at_all = kka * g_inv
    kd_all = k * g_rem
    ad_all = kka * g_rem
    v_all = v_ref[...]

    lane = lax.broadcasted_iota(jnp.int32, (1, LANES), 1)
    lo = lane < HALF

    def st(x):
        return jnp.concatenate([jnp.where(lo, x, 0.0), jnp.where(lo, 0.0, x)], axis=0)

    r2 = lax.broadcasted_iota(jnp.int32, (4 * C, 4 * C), 0)
    c2 = lax.broadcasted_iota(jnp.int32, (4 * C, 4 * C), 1)
    rt = r2 % C
    ct = c2 % C
    tmask = (ct < rt) | ((r2 >= 2 * C) & (ct == rt))
    ri = lax.broadcasted_iota(jnp.int32, (2 * C, 2 * C), 0)
    ci = lax.broadcasted_iota(jnp.int32, (2 * C, 2 * C), 1)
    eye = (ri == ci).astype(F32)

    pairs = range(4)
    sls = [slice(m * LANES, (m + 1) * LANES) for m in pairs]
    bf = lambda x: x.astype(BF16)
    qk_st = [st(qk_all[:, sl]) for sl in sls]
    r_st = [st(r_all[:, sl]) for sl in sls]
    v_st = [st(v_all[:, sl]) for sl in sls]
    ad_st = [st(ad_all[:, sl]) for sl in sls]
    kd_st = [st(kd_all[:, sl]) for sl in sls]
    xm = [jnp.where(tmask, _dot_nt(bf(jnp.concatenate([qk_st[m], r_st[m]], axis=0)),
                                   bf(jnp.concatenate([st(at_all[:, sls[m]]), st(kt_all[:, sls[m]])], axis=0))), 0.0)
          for m in pairs]
    av = [_dot(bf(xm[m][:, 2 * C:]), bf(v_st[m])) for m in pairs]
    p = [xm[m][:2 * C, :2 * C] for m in pairs]
    tm_ = [eye - p[m] for m in pairs]
    for _ in range(5):
        p = [_dot(bf(p[m]), bf(p[m])) for m in pairs]
        tm_ = [_dot(bf(tm_[m]), bf(eye + p[m])) for m in pairs]
    wu = [_dot(bf(tm_[m]), bf(jnp.concatenate([qk_st[m], av[m][:2 * C]], axis=1))) for m in pairs]
    ry = [_dot(bf(xm[m][2 * C:, :2 * C]), bf(wu[m])) for m in pairs]
    rq = [r_st[m] - ry[m][:, :LANES] for m in pairs]
    y0 = [av[m][2 * C:] - ry[m][:, LANES:] for m in pairs]
    gt = [eye * gc[:, sls[m]] - _dot_tn(bf(wu[m][:, :LANES]), bf(ad_st[m])) for m in pairs]
    ht = [_dot_tn(bf(jnp.concatenate([v_st[m], wu[m][:, LANES:]], axis=0)),
                  bf(jnp.concatenate([kd_st[m], -ad_st[m]], axis=0))) for m in pairs]
    sb = [bf(s_ref[m]) for m in pairs]
    y_st = [_dot_nt(bf(rq[m]), sb[m]) + y0[m] for m in pairs]
    s_new = [_dot(sb[m], bf(gt[m])) + ht[m] for m in pairs]
    for m in pairs:
        sl = sls[m]
        s_ref[m] = s_new[m]
        sout_ref[m] = s_new[m][:C] + s_new[m][C:]
        y = y_st[m][:C] + y_st[m][C:]
        o_ref[:, sl] = _rwkv_out(y, lo, gng_ref[:, sl], gnb_ref[:, sl], bonus_ref[:, sl], g_ref[:, sl])


def _rwkv_out(y, lo, gn_g, gn_b, bonus, g):
    mu = _half_sum(y, lo) * (1.0 / RW_HEAD_DIM)
    d = y - mu
    var = _half_sum(d * d, lo) * (1.0 / RW_HEAD_DIM)
    yn = d * lax.rsqrt(var + RW_GN_EPS) * gn_g + gn_b
    return ((yn + bonus) * g).astype(BF16)


def _rwkv_chunk(seqs, gn_g, gn_b, n_seq, t):
    C = RW_CHUNK
    nch = t // C
    row_spec = pl.BlockSpec((C, 512), lambda b, c: (b * nch + c, 0))
    vec = pl.BlockSpec((1, 512), lambda b, c: (0, 0))
    return pl.pallas_call(
        _rwkv_chunk_kernel,
        grid=(n_seq, nch),
        in_specs=[row_spec] * 8 + [vec, vec],
        out_specs=[row_spec, pl.BlockSpec((None, 4, RW_HEAD_DIM, LANES), lambda b, c: (b, 0, 0, 0))],
        out_shape=[jax.ShapeDtypeStruct((n_seq * t, 512), BF16),
                   jax.ShapeDtypeStruct((n_seq, 4, RW_HEAD_DIM, LANES), F32)],
        scratch_shapes=[pltpu.VMEM((4, LANES, LANES), F32)],
        compiler_params=_cparams(("parallel", "arbitrary")),
        name="rwkv_chunk",
    )(*seqs, gn_g, gn_b)


def _rwkv_step_kernel(r_ref, lw_ref, k_ref, v_ref, kk_ref, kka_ref, g_ref, bonus_ref, s_ref, gng_ref, gnb_ref,
                      o_ref, sout_ref):
    lane = lax.broadcasted_iota(jnp.int32, (1, LANES), 1)
    lo = lane < HALF
    rowi = lax.broadcasted_iota(jnp.int32, (RW_HEAD_DIM, LANES), 0)
    diag = (rowi == lax.broadcasted_iota(jnp.int32, (RW_HEAD_DIM, LANES), 1) % HALF).astype(F32)
    for m in range(4):
        sl = slice(m * LANES, (m + 1) * LANES)
        s = s_ref[m]
        sk = _half_sum(s * kk_ref[:, sl], lo)
        v_col = _half_sum(diag * v_ref[:, sl], lo)
        s_new = s * jnp.exp(lw_ref[:, sl]) - sk * kka_ref[:, sl] + v_col * k_ref[:, sl]
        sout_ref[m] = s_new
        y_col = _half_sum(s_new * r_ref[:, sl], lo)
        y = jnp.sum(diag * y_col, axis=0, keepdims=True)
        o_ref[:, sl] = _rwkv_out(y, lo, gng_ref[:, sl], gnb_ref[:, sl], bonus_ref[:, sl], g_ref[:, sl])


def _rwkv_step(seqs, s_pairs, gn_g, gn_b):
    bs = s_pairs.shape[0]
    row_spec = pl.BlockSpec((None, 1, 512), lambda b: (b, 0, 0))
    st_spec = pl.BlockSpec((None, 4, RW_HEAD_DIM, LANES), lambda b: (b, 0, 0, 0))
    vec = pl.BlockSpec((1, 512), lambda b: (0, 0))
    o, s = pl.pallas_call(
        _rwkv_step_kernel,
        grid=(bs,),
        in_specs=[row_spec] * 8 + [st_spec, vec, vec],
        out_specs=[row_spec, st_spec],
        out_shape=[jax.ShapeDtypeStruct((bs, 1, 512), BF16),
                   jax.ShapeDtypeStruct((bs, 4, RW_HEAD_DIM, LANES), F32)],
        compiler_params=_cparams(("parallel",)),
        name="rwkv_step",
    )(*[a[:, None, :] for a in seqs], s_pairs, gn_g, gn_b)
    return o[:, 0], s


def _state_to_pairs(s):
    b = s.shape[0]
    return jnp.transpose(s.reshape(b, 4, 2, 64, 64).astype(F32), (0, 1, 3, 2, 4)).reshape(b, 4, 64, LANES)


def _state_from_pairs(sp):
    b = sp.shape[0]
    return jnp.transpose(sp.reshape(b, 4, 64, 2, 64), (0, 1, 3, 2, 4)).reshape(b, 8, 64, 64)


def _nsa_prompt_kernel(q_ref, kvt_ref, cmp_ref, gate_ref, o_ref, m_ref, acc_ref, *, k_top, tk):
    qi = pl.program_id(1)
    tq = NSA_TQ
    q0 = qi * tq
    lane = lax.broadcasted_iota(jnp.int32, (1, LANES), 1)
    lo = lane < HALF
    tok = q0 + lax.broadcasted_iota(jnp.int32, (tq, 1), 0)
    cmpv = cmp_ref[...]
    kc = cmpv[:, :LANES].astype(BF16)
    vc = cmpv[:, LANES:].astype(BF16)
    gates = gate_ref[...]
    blk_c = 2 * (lane % HALF) + lane // HALF
    dist_c = tok - (blk_c * CMP_BLOCK + (CMP_BLOCK - 1))
    mask_c = dist_c >= 0
    dist_cf = dist_c.astype(F32)
    cur = tok // SEL_BLOCK
    valid = (lane * SEL_BLOCK <= tok) & lo
    forced = (lane == 0) | (lane == cur) | (lane == cur - 1)
    in_grp = (lo, jnp.logical_not(lo))
    den_lane = (HALF, 0)
    hrows = lambda h: slice(h * tq, (h + 1) * tq)
    grows = lambda g: slice(g * NSA_GROUP * tq, (g + 1) * NSA_GROUP * tq)
    q_all = jnp.concatenate([q_ref[:, h * LANES:(h + 1) * LANES] for h in range(NSA_HEADS)], axis=0)

    s = _dot_nt(q_all, kc)
    ps = []
    imp = [jnp.zeros((tq, LANES), F32) for _ in range(NSA_KV_HEADS)]
    for h in range(NSA_HEADS):
        sr = jnp.where(mask_c, s[hrows(h)] - NSA_SLOPES[h] * dist_cf, NEG_INF)
        mx = jnp.max(sr, axis=1, keepdims=True)
        e = jnp.where(mask_c, jnp.exp(sr - mx), 0.0)
        p = e / jnp.maximum(jnp.sum(e, axis=1, keepdims=True), 1e-30)
        ps.append(p.astype(BF16))
        imp[h // NSA_GROUP] = imp[h // NSA_GROUP] + p
    o_c = _dot(jnp.concatenate(ps, axis=0), vc)

    score, sel = [], []
    for g in range(NSA_KV_HEADS):
        imp_sel = imp[g] + pltpu.roll(imp[g], HALF, 1)
        sc = jnp.where(valid, imp_sel + jnp.where(forced, FORCE_BONUS, 0.0), NEG_INF)
        score.append(jnp.where(lo, sc, BELOW_NEG_INF))
        sel.append(jnp.zeros((tq, LANES), F32))
    for _ in range(k_top):
        for g in range(NSA_KV_HEADS):
            mx = jnp.max(score[g], axis=1, keepdims=True)
            idx = jnp.min(jnp.where(score[g] == mx, lane, 4 * LANES), axis=1, keepdims=True)
            hit = lane == idx
            sel[g] = jnp.where(hit, 1.0, sel[g])
            score[g] = jnp.where(hit, BELOW_NEG_INF, score[g])
    sel_b = [x.astype(BF16) for x in sel]

    n_sub = tk // LANES
    blk_row = lax.broadcasted_iota(jnp.int32, (LANES, tk), 0)
    key_blk = lax.broadcasted_iota(jnp.int32, (LANES, tk), 1) // SEL_BLOCK
    col_k = lax.broadcasted_iota(jnp.int32, (1, tk), 1)
    m_ref[...] = jnp.full(m_ref.shape, NEG_INF, F32)
    acc_ref[...] = jnp.zeros(acc_ref.shape, F32)

    def sel_tile(j, carry):
        start = pl.multiple_of(j * tk, tk)
        kt = kvt_ref[2, pl.ds(start, tk), :]
        vt = kvt_ref[3, pl.ds(start, tk), :]
        s = _dot_nt(q_all, kt)
        dist = tok - (start + col_k)
        dist_f = dist.astype(F32)
        causal = jnp.where(dist >= 0, 0.0, NEG_INF)
        expand = (blk_row == (tk // SEL_BLOCK) * j + key_blk).astype(BF16)
        for g in range(NSA_KV_HEADS):
            bias = (_dot(sel_b[g], expand) - 1.0) * (-NEG_INF) + causal
            es = []
            for r in range(NSA_GROUP):
                h = NSA_GROUP * g + r
                x = s[hrows(h)] - NSA_SLOPES[h] * dist_f + bias
                xm = x[:, 0:LANES]
                for i in range(1, n_sub):
                    xm = jnp.maximum(xm, x[:, i * LANES:(i + 1) * LANES])
                m_old = m_ref[hrows(h)]
                m_new = jnp.maximum(m_old, jnp.max(xm, axis=1, keepdims=True))
                m_ref[hrows(h)] = m_new
                acc_ref[hrows(h)] = jnp.exp(m_old - m_new) * acc_ref[hrows(h)]
                es.append(jnp.exp(x - jnp.concatenate([m_new] * n_sub, axis=1)).astype(BF16))
            vg = jnp.where(in_grp[g], vt, jnp.ones_like(vt))
            acc_ref[grows(g)] += _dot(jnp.concatenate(es, axis=0), vg)
        return carry

    lax.fori_loop(0, (q0 + tq - 1) // tk + 1, sel_tile, 0)
    acc = acc_ref[...]
    o_s = [acc[grows(g)] / jnp.maximum(acc[grows(g), den_lane[g]:den_lane[g] + 1], 1e-30) for g in range(NSA_KV_HEADS)]

    wk = WINDOW + tq
    kstart = pl.multiple_of(jnp.maximum(q0 - WINDOW, 0), tq)
    kt = kvt_ref[4, pl.ds(kstart, wk), :]
    vt = kvt_ref[5, pl.ds(kstart, wk), :]
    s = _dot_nt(q_all, kt)
    dist = tok - (kstart + lax.broadcasted_iota(jnp.int32, (1, wk), 1))
    dist_f = dist.astype(F32)
    bias_w = jnp.where((dist >= 0) & (dist <= WINDOW), 0.0, NEG_INF)
    o_w = []
    for g in range(NSA_KV_HEADS):
        es = []
        for r in range(NSA_GROUP):
            h = NSA_GROUP * g + r
            x = s[hrows(h)] - NSA_SLOPES[h] * dist_f + bias_w
            es.append(jnp.exp(x - jnp.max(x, axis=1, keepdims=True)).astype(BF16))
        ow = _dot(jnp.concatenate(es, axis=0), jnp.where(in_grp[g], vt, jnp.ones_like(vt)))
        o_w.append(ow / jnp.maximum(ow[:, den_lane[g]:den_lane[g] + 1], 1e-30))

    for h in range(NSA_HEADS):
        g, r = divmod(h, NSA_GROUP)
        rows = slice(r * tq, (r + 1) * tq)
        o = (gates[:, 3 * h:3 * h + 1] * o_c[hrows(h)] + gates[:, 3 * h + 1:3 * h + 2] * o_s[g][rows]
             + gates[:, 3 * h + 2:3 * h + 3] * o_w[g][rows])
        o_ref[:, h * LANES:(h + 1) * LANES] = jnp.where(in_grp[g], o, 0.0).astype(BF16)


def _nsa_prompt(q_pad, kvt, cmp_perm, gates, n_seq, t):
    tq = NSA_TQ
    nq = t // tq
    k_top = min(TOP_N, t // SEL_BLOCK)
    assert k_top >= 3 and t >= WINDOW + tq and t % NSA_TK == 0
    return pl.pallas_call(
        functools.partial(_nsa_prompt_kernel, k_top=k_top, tk=NSA_TK),
        grid=(n_seq, nq),
        in_specs=[pl.BlockSpec((tq, 1024), lambda b, i: (b * nq + i, 0)),
                  pl.BlockSpec((None, 6, t, LANES), lambda b, i: (b, 0, 0, 0)),
                  pl.BlockSpec((None, LANES, 2 * LANES), lambda b, i: (b, 0, 0)),
                  pl.BlockSpec((tq, LANES), lambda b, i: (b * nq + i, 0))],
        out_specs=pl.BlockSpec((tq, 1024), lambda b, i: (b * nq + i, 0)),
        out_shape=jax.ShapeDtypeStruct((n_seq * t, 1024), BF16),
        scratch_shapes=[pltpu.VMEM((NSA_HEADS * tq, LANES), F32), pltpu.VMEM((NSA_HEADS * tq, LANES), F32)],
        compiler_params=_cparams(("parallel", "parallel")),
        name="nsa_prompt",
    )(q_pad, kvt, cmp_perm, gates)


def _permute_cmp(cmp, n_seq, t):
    nc = t // CMP_BLOCK
    c = cmp.reshape(n_seq, nc // 2, 2, 2 * LANES)
    c = jnp.pad(c, ((0, 0), (0, HALF - nc // 2), (0, 0), (0, 0)))
    return jnp.transpose(c, (0, 2, 1, 3)).reshape(n_seq, LANES, 2 * LANES)


def _compress_pool_kernel(x_ref, w_ref, o_ref):
    pb = x_ref.shape[0]
    page = x_ref.shape[-1]
    nb = page // CMP_BLOCK
    blk = lax.broadcasted_iota(jnp.int32, (8, page), 0)
    seg = (lax.broadcasted_iota(jnp.int32, (8, page), 1) // CMP_BLOCK == blk).astype(BF16)
    items = [(p, j) for p in range(pb) for j in range(2)]
    parts = [_split3(x_ref[p, j].reshape(LANES, page) * w_ref[j]) for p, j in items]
    sums = [_dot_nt(seg, k1) + _dot_nt(seg, k2) + _dot_nt(seg, k3) for k1, k2, k3 in parts]
    for (p, j), res in zip(items, sums):
        o_ref[p, j * nb:(j + 1) * nb, :] = res[0:nb]


def _compress_pool(cache_t, w_t, pb):
    L, n_pool = cache_t.shape[:2]
    page = cache_t.shape[-1]
    nb = page // CMP_BLOCK
    return pl.pallas_call(
        _compress_pool_kernel,
        grid=(L, n_pool // pb),
        in_specs=[pl.BlockSpec((None, pb, 2, NSA_KV_HEADS, NSA_HEAD_DIM, page), lambda l, i: (l, i, 0, 0, 0, 0)),
                  pl.BlockSpec((None, 2, LANES, page), lambda l, i: (l, 0, 0, 0))],
        out_specs=pl.BlockSpec((None, pb, 2 * nb, LANES), lambda l, i: (l, i, 0, 0)),
        out_shape=jax.ShapeDtypeStruct((L, n_pool, 2 * nb, LANES), F32),
        compiler_params=_cparams(("parallel", "parallel")),
        name="compress_pool",
    )(cache_t, w_t)


def _nsa_dec_cmp_kernel(pt_ref, pool_ref, q_ref, gate_ref, slope_ref, win_ref, kvn_ref, part_ref, idx_ref, gath_ref,
                        *, n_pages, past_len, k_top):
    b = pl.program_id(0)
    t = past_len
    for p in range(n_pages):
        tile = pool_ref[pt_ref[b, p]]
        for r in range(8):
            gath_ref[r, p:p + 1, :] = tile[r:r + 1, :]
    x = [gath_ref[r] for r in range(8)]
    q = q_ref[...]
    slope = slope_ref[:, 0:1]
    gates = gate_ref[...]
    lane = lax.broadcasted_iota(jnp.int32, (1, LANES), 1)
    lane_p = lax.broadcasted_iota(jnp.int32, (1, 2 * n_pages), 1)
    page_of = lane_p % n_pages
    ss, masks = [], []
    for pair in range(2):
        kmat = jnp.concatenate([x[2 * pair], x[2 * pair + 1]], axis=0).astype(BF16)
        blk = page_of * 4 + 2 * pair + lane_p // n_pages
        dist = t - (blk * CMP_BLOCK + (CMP_BLOCK - 1))
        mask = dist >= 0
        s = _dot_nt(q, kmat) - slope * dist.astype(F32)
        ss.append(jnp.where(mask, s, NEG_INF))
        masks.append(mask)
    mx = jnp.maximum(jnp.max(ss[0], axis=1, keepdims=True), jnp.max(ss[1], axis=1, keepdims=True))
    es = [jnp.where(masks[i], jnp.exp(ss[i] - mx), 0.0) for i in range(2)]
    den = jnp.maximum(jnp.sum(es[0], axis=1, keepdims=True) + jnp.sum(es[1], axis=1, keepdims=True), 1e-30)
    ps = [e / den for e in es]
    o_c = jnp.zeros((NSA_HEADS, LANES), F32)
    for pair in range(2):
        vmat = jnp.concatenate([x[4 + 2 * pair], x[5 + 2 * pair]], axis=0).astype(BF16)
        o_c = o_c + _dot(ps[pair].astype(BF16), vmat)
    n_sel_blk = lane_p // n_pages + 2 * page_of
    cur = t // SEL_BLOCK
    forced = (n_sel_blk == 0) | (n_sel_blk == cur) | (n_sel_blk == cur - 1)
    valid = n_sel_blk * SEL_BLOCK <= t
    idx_rows = []
    for g in range(NSA_KV_HEADS):
        imp_a = jnp.sum(ps[0][NSA_GROUP * g:NSA_GROUP * (g + 1)], axis=0, keepdims=True)
        imp_b = jnp.sum(ps[1][NSA_GROUP * g:NSA_GROUP * (g + 1)], axis=0, keepdims=True)
        ev = imp_a + pltpu.roll(imp_a, n_pages, 1)
        od = imp_b + pltpu.roll(imp_b, n_pages, 1)
        imp = jnp.where(lane_p < n_pages, ev, od)
        score = jnp.where(valid, imp + jnp.where(forced, FORCE_BONUS, 0.0), NEG_INF)
        chosen = jnp.where(lane == 0, cur, 0)
        for it in range(1, k_top):
            mxs = jnp.max(score, axis=1, keepdims=True)
            pick = jnp.min(jnp.where(score == mxs, n_sel_blk, 1 << 30), axis=1, keepdims=True)
            score = jnp.where(n_sel_blk == pick, BELOW_NEG_INF, score)
            chosen = jnp.where(lane == it, pick, chosen)
        idx_rows.append(chosen)
    rowi = lax.broadcasted_iota(jnp.int32, (8, LANES), 0)
    idx_ref[...] = jnp.where(rowi == 0, idx_rows[0], jnp.where(rowi == 1, idx_rows[1], 0))
    wl = win_ref.shape[-1]
    kw = win_ref[0].reshape(LANES, wl).astype(BF16)
    vw = win_ref[1].reshape(LANES, wl).astype(BF16)
    col = lax.broadcasted_iota(jnp.int32, (1, wl), 1)
    dist_w = wl - col
    mask_w = (t - dist_w >= 0) & (dist_w <= WINDOW)
    s_w = jnp.where(mask_w, _dot(q, kw) - slope * dist_w.astype(F32), NEG_INF)
    kvn = kvn_ref[...]
    qf = q.astype(F32)
    s_n = jnp.sum(qf * kvn[4:5, :].astype(BF16).astype(F32), axis=1, keepdims=True)
    mw = jnp.maximum(jnp.max(s_w, axis=1, keepdims=True), s_n)
    e_w = jnp.where(mask_w, jnp.exp(s_w - mw), 0.0)
    e_n = jnp.exp(s_n - mw)
    den_w = jnp.maximum(jnp.sum(e_w, axis=1, keepdims=True) + e_n, 1e-30)
    o_w = (_dot_nt((e_w / den_w).astype(BF16), vw)
           + (e_n / den_w).astype(BF16).astype(F32) * kvn[5:6, :].astype(BF16).astype(F32))
    part_ref[...] = gates[:, 0:1] * o_c + gates[:, 2:3] * o_w


def _nsa_dec_sel_kernel(pt_ref, sel_ref, blk_ref, q_ref, gate_ref, slope_ref, kvn_ref, part_ref, o_ref,
                        m_ref, l_ref, acc_ref, *, past_len, k_top):
    b = pl.program_id(0)
    j = pl.program_id(1)
    g = j // k_top
    t = past_len

    @pl.when(j == 0)
    def _():
        m_ref[...] = jnp.full(m_ref.shape, NEG_INF, F32)
        l_ref[...] = jnp.zeros(l_ref.shape, F32)
        acc_ref[...] = jnp.zeros(acc_ref.shape, F32)

    n = sel_ref[b, j]
    q = q_ref[...]
    slope = slope_ref[:, 0:1]
    page = blk_ref.shape[-1]
    halves = page // SEL_BLOCK
    kt = blk_ref[0].reshape(LANES, page).astype(BF16)
    vt = blk_ref[1].reshape(LANES, page).astype(BF16)
    col = lax.broadcasted_iota(jnp.int32, (1, page), 1)
    pos = (n // halves) * page + col
    mask = (col // SEL_BLOCK == n % halves) & (pos < t)
    rowh = lax.broadcasted_iota(jnp.int32, (NSA_HEADS, 1), 0)
    in_g = (rowh // NSA_GROUP) == g
    s = jnp.where(mask, _dot(q, kt) - slope * (t - pos).astype(F32), NEG_INF)
    is_cur = n * SEL_BLOCK + SEL_BLOCK > t
    kvn = kvn_ref[...]
    s_n = jnp.sum(q.astype(F32) * kvn[2:3, :].astype(BF16).astype(F32), axis=1, keepdims=True)
    s_n = jnp.where(is_cur, s_n, NEG_INF)
    m_old = m_ref[...]
    m_new = jnp.maximum(m_old, jnp.maximum(jnp.max(s, axis=1, keepdims=True), s_n))
    alpha = jnp.exp(m_old - m_new)
    e = jnp.where(mask, jnp.exp(s - m_new), 0.0)
    e_n = jnp.where(is_cur, jnp.exp(s_n - m_new), 0.0)
    l_new = alpha * l_ref[...] + jnp.sum(e, axis=1, keepdims=True) + e_n
    acc_new = (alpha * acc_ref[...] + _dot_nt(e.astype(BF16), vt)
               + e_n.astype(BF16).astype(F32) * kvn[3:4, :].astype(BF16).astype(F32))
    m_ref[...] = jnp.where(in_g, m_new, m_old)
    l_ref[...] = jnp.where(in_g, l_new, l_ref[...])
    acc_ref[...] = jnp.where(in_g, acc_new, acc_ref[...])

    @pl.when(j == pl.num_programs(1) - 1)
    def _():
        lane = lax.broadcasted_iota(jnp.int32, (1, LANES), 1)
        o_s = acc_ref[...] / jnp.maximum(l_ref[...], 1e-30)
        o = part_ref[...] + gate_ref[:, 1:2] * o_s
        in_half = (lane // HALF) == (rowh // NSA_GROUP)
        o_ref[...] = jnp.where(in_half, o, 0.0).astype(BF16)


def _nsa_decode(page_table, cmp_pool, cache_t, win_t, l, q_pad, gates, kv_new, slopes, past_len):
    bs, n_pages = page_table.shape
    n_pool, page = cache_t.shape[1], cache_t.shape[-1]
    k_top = min(TOP_N, past_len // SEL_BLOCK + 1)
    q3 = q_pad.reshape(bs, NSA_HEADS, LANES)
    g3 = jnp.pad(gates[:, :3 * NSA_HEADS].reshape(bs, NSA_HEADS, 3), ((0, 0), (0, 0), (0, LANES - 3)))
    kvn = kv_new.reshape(bs, 6, LANES)
    wl = win_t.shape[-1]
    part, idx = pl.pallas_call(
        functools.partial(_nsa_dec_cmp_kernel, n_pages=n_pages, past_len=past_len, k_top=k_top),
        grid_spec=pltpu.PrefetchScalarGridSpec(
            num_scalar_prefetch=1, grid=(bs,),
            in_specs=[pl.BlockSpec((None, n_pool, 8, LANES), lambda b, pt: (l, 0, 0, 0)),
                      pl.BlockSpec((None, NSA_HEADS, LANES), lambda b, pt: (b, 0, 0)),
                      pl.BlockSpec((None, NSA_HEADS, LANES), lambda b, pt: (b, 0, 0)),
                      pl.BlockSpec((NSA_HEADS, LANES), lambda b, pt: (0, 0)),
                      pl.BlockSpec((None, None, 2, NSA_KV_HEADS, NSA_HEAD_DIM, wl), lambda b, pt: (l, b, 0, 0, 0, 0)),
                      pl.BlockSpec((None, 6, LANES), lambda b, pt: (b, 0, 0))],
            out_specs=[pl.BlockSpec((None, NSA_HEADS, LANES), lambda b, pt: (b, 0, 0)),
                       pl.BlockSpec((None, 8, LANES), lambda b, pt: (b, 0, 0))],
            scratch_shapes=[pltpu.VMEM((8, n_pages, LANES), F32)]),
        out_shape=[jax.ShapeDtypeStruct((bs, NSA_HEADS, LANES), F32),
                   jax.ShapeDtypeStruct((bs, 8, LANES), jnp.int32)],
        compiler_params=_cparams(("arbitrary",)),
        name="nsa_dec_cmp",
    )(page_table, cmp_pool, q3, g3, slopes, win_t, kvn)
    sel = idx[:, :NSA_KV_HEADS, :k_top].reshape(bs, NSA_KV_HEADS * k_top)
    halves = page // SEL_BLOCK

    def blk_map(b, j, pt, sl):
        pg = pt[b, jnp.minimum(sl[b, j] // halves, n_pages - 1)]
        return (l, pg, 1, 0, 0, 0)

    out = pl.pallas_call(
        functools.partial(_nsa_dec_sel_kernel, past_len=past_len, k_top=k_top),
        grid_spec=pltpu.PrefetchScalarGridSpec(
            num_scalar_prefetch=2, grid=(bs, NSA_KV_HEADS * k_top),
            in_specs=[pl.BlockSpec((None, None, 2, NSA_KV_HEADS, NSA_HEAD_DIM, page), blk_map),
                      pl.BlockSpec((None, NSA_HEADS, LANES), lambda b, j, pt, sl: (b, 0, 0)),
                      pl.BlockSpec((None, NSA_HEADS, LANES), lambda b, j, pt, sl: (b, 0, 0)),
                      pl.BlockSpec((NSA_HEADS, LANES), lambda b, j, pt, sl: (0, 0)),
                      pl.BlockSpec((None, 6, LANES), lambda b, j, pt, sl: (b, 0, 0)),
                      pl.BlockSpec((None, NSA_HEADS, LANES), lambda b, j, pt, sl: (b, 0, 0))],
            out_specs=pl.BlockSpec((None, NSA_HEADS, LANES), lambda b, j, pt, sl: (b, 0, 0)),
            scratch_shapes=[pltpu.VMEM((NSA_HEADS, 1), F32), pltpu.VMEM((NSA_HEADS, 1), F32),
                            pltpu.VMEM((NSA_HEADS, LANES), F32)]),
        out_shape=jax.ShapeDtypeStruct((bs, NSA_HEADS, LANES), BF16),
        compiler_params=_cparams(("arbitrary", "arbitrary")),
        name="nsa_dec_sel",
    )(page_table, sel, cache_t, q3, g3, slopes, kvn, part)
    return out.reshape(bs, NSA_HEADS * LANES)


def _pool_tail(sums, cur, cnts, pw_ref, scale_ref, o_ref):
    for gi in range(len(POOL_WINDOWS)):
        sl = slice(gi * LANES, (gi + 1) * LANES)
        d = sums[gi] / cnts[gi] - cur[:, sl]
        y = _dot(d.astype(BF16), pw_ref[gi])
        o_ref[:, sl] = (y * scale_ref[:, sl]).astype(BF16)


def _pool_seq_kernel(p_ref, pprev_ref, hist_ref, pw_ref, scale_ref, o_ref, zz_ref, *, seq_tiles, tm, start_pos):
    i = pl.program_id(0)
    first = (i % seq_tiles) == 0
    cur = p_ref[...]
    zz_ref[0:16, :] = jnp.where(first, hist_ref[...], pprev_ref[...])
    zz_ref[16:16 + tm, :] = cur
    pos = start_pos + (i % seq_tiles) * tm + lax.broadcasted_iota(jnp.int32, (tm, 1), 0)
    sums, cnts = [], []
    for gi, w in enumerate(POOL_WINDOWS):
        sl = slice(gi * LANES, (gi + 1) * LANES)
        s = cur[:, sl]
        for k in range(1, w):
            s = s + zz_ref[16 - k:16 - k + tm, sl]
        sums.append(s)
        cnts.append(jnp.minimum(pos + 1, w).astype(F32))
    _pool_tail(sums, cur, cnts, pw_ref, scale_ref, o_ref)


def _pool_step_kernel(z_ref, pw_ref, scale_ref, o_ref, *, start_pos):
    z = z_ref[...]
    cur = z[:, 15, :]
    sums, cnts = [], []
    for gi, w in enumerate(POOL_WINDOWS):
        sl = slice(gi * LANES, (gi + 1) * LANES)
        sums.append(jnp.sum(z[:, 16 - w:16, sl], axis=1))
        cnts.append(float(min(start_pos + 1, w)))
    _pool_tail(sums, cur, cnts, pw_ref, scale_ref, o_ref)


def _pool_seq(p_pool, hist16, pool_w, pool_scale, t, tm, start_pos):
    n = p_pool.shape[0]
    tps = t // tm
    return pl.pallas_call(
        functools.partial(_pool_seq_kernel, seq_tiles=tps, tm=tm, start_pos=start_pos),
        grid=(n // tm,),
        in_specs=[pl.BlockSpec((tm, 512), lambda i: (i, 0)),
                  pl.BlockSpec((16, 512), lambda i: (jnp.maximum(i * (tm // 16) - 1, 0), 0)),
                  pl.BlockSpec((None, 16, 512), lambda i: (i // tps, 0, 0)),
                  pl.BlockSpec((4, LANES, LANES), lambda i: (0, 0, 0)),
                  pl.BlockSpec((1, 512), lambda i: (0, 0))],
        out_specs=pl.BlockSpec((tm, 512), lambda i: (i, 0)),
        out_shape=jax.ShapeDtypeStruct((n, 512), BF16),
        scratch_shapes=[pltpu.VMEM((tm + 16, 512), F32)],
        compiler_params=_cparams(("parallel",)),
        name="pool_seq",
    )(p_pool, p_pool, hist16, pool_w, pool_scale)


def _pool_step(z16, pool_w, pool_scale, start_pos):
    bs = z16.shape[0]
    return pl.pallas_call(
        functools.partial(_pool_step_kernel, start_pos=start_pos),
        grid=(1,),
        in_specs=[pl.BlockSpec((bs, 16, 512), lambda i: (0, 0, 0)),
                  pl.BlockSpec((4, LANES, LANES), lambda i: (0, 0, 0)),
                  pl.BlockSpec((1, 512), lambda i: (0, 0))],
        out_specs=pl.BlockSpec((bs, 512), lambda i: (0, 0)),
        out_shape=jax.ShapeDtypeStruct((bs, 512), BF16),
        compiler_params=_cparams(("arbitrary",)),
        name="pool_step",
    )(z16, pool_w, pool_scale)


def _merge_kernel(x_ref, sc1_ref, sh1_ref, g1_ref, sc2_ref, sh2_ref, orw_ref, onsa_ref, opool_ref,
                  wg_ref, wbr_ref, wbn_ref, wbp_ref, wout_ref, lng_ref, lnb_ref, wr_ref, br_ref,
                  x1_ref, u2_ref, route_ref, *, alpha):
    x = x_ref[...]
    d = x.shape[1]
    u = (x * (1.0 + sc1_ref[...]) + sh1_ref[...]).astype(BF16)
    mixed = jnp.zeros(x.shape, F32)
    for bi, (o_ref, wb_ref) in enumerate(((orw_ref, wbr_ref), (onsa_ref, wbn_ref), (opool_ref, wbp_ref))):
        gate = _sigmoid(_dot(u, wg_ref[:, bi * d:(bi + 1) * d]))
        mixed = mixed + gate * _dot(o_ref[...], wb_ref[...])
    m = _dot(mixed.astype(BF16), wout_ref[...])
    x1 = _layer_norm(alpha * x + (1.0 + g1_ref[...]) * m, lng_ref[...], lnb_ref[...])
    x1_ref[...] = x1
    u2 = x1 * (1.0 + sc2_ref[...]) + sh2_ref[...]
    u2b = u2.astype(BF16)
    u2_ref[...] = u2b
    lg = _dot(u2b, wr_ref[...]) + br_ref[...]
    lane = lax.broadcasted_iota(jnp.int32, (1, LANES), 1)
    is_g = lane < MOE_GROUPS
    mg = jnp.max(jnp.where(is_g, lg, BELOW_NEG_INF), axis=1, keepdims=True)
    gsel = jnp.min(jnp.where(is_g & (lg == mg), lane, LANES), axis=1, keepdims=True)
    wgrp = 1.0 / jnp.sum(jnp.where(is_g, jnp.exp(lg - mg), 0.0), axis=1, keepdims=True)
    e_lane = lane - ROUTE_LANE0
    in_grp = (e_lane >= 0) & (e_lane < MOE_GROUPS * EXPERTS_PER_GROUP) & ((e_lane // EXPERTS_PER_GROUP) == gsel)
    v1 = jnp.max(jnp.where(in_grp, lg, BELOW_NEG_INF), axis=1, keepdims=True)
    i1 = jnp.min(jnp.where(in_grp & (lg == v1), lane, LANES), axis=1, keepdims=True)
    rest = in_grp & (lane != i1)
    v2 = jnp.max(jnp.where(rest, lg, BELOW_NEG_INF), axis=1, keepdims=True)
    i2 = jnp.min(jnp.where(rest & (lg == v2), lane, LANES), axis=1, keepdims=True)
    e2 = jnp.exp(v2 - v1)
    w1 = wgrp / (1.0 + e2)
    w2 = wgrp * e2 / (1.0 + e2)
    for g in range(MOE_GROUPS):
        src = lane + (ROUTE_LANE0 + EXPERTS_PER_GROUP * g)
        route_ref[g] = jnp.where(lane < EXPERTS_PER_GROUP,
                                 jnp.where(src == i1, w1, jnp.where(src == i2, w2, 0.0)), 0.0)


def _merge(x, mods, o_rw, o_nsa, o_pool, wts, rows_per_group, tm, alpha):
    n, d = x.shape
    full = lambda a: pl.BlockSpec(a.shape, lambda i: (0,) * a.ndim)
    row = lambda c: pl.BlockSpec((tm, c), lambda i: (i, 0))
    return pl.pallas_call(
        functools.partial(_merge_kernel, alpha=alpha),
        grid=(n // tm,),
        in_specs=[row(d)] + [_mod_spec(m, rows_per_group, tm) for m in mods]
                 + [row(512), row(1024), row(512)] + [full(w) for w in wts],
        out_specs=[row(d), row(d), pl.BlockSpec((MOE_GROUPS, tm, LANES), lambda i: (0, i, 0))],
        out_shape=[jax.ShapeDtypeStruct((n, d), F32), jax.ShapeDtypeStruct((n, d), BF16),
                   jax.ShapeDtypeStruct((MOE_GROUPS, n, LANES), F32)],
        compiler_params=_cparams(("parallel",), 56),
        name="merge",
    )(x, *mods, o_rw, o_nsa, o_pool, *wts)


def _moe_kernel(x1_ref, u2_ref, route_ref, g2_ref, wg_ref, wu_ref, wd_ref, lng_ref, lnb_ref, o_ref, acc_ref, *, alpha):
    g = pl.program_id(1)

    @pl.when(g == 0)
    def _():
        acc_ref[...] = jnp.zeros(acc_ref.shape, F32)

    u = u2_ref[...]
    hg = _dot(u, wg_ref[...])
    h = (hg * _sigmoid(hg) * _dot(u, wu_ref[...])).astype(BF16)
    route = route_ref[...]
    y = acc_ref[...]
    for e in range(EXPERTS_PER_GROUP):
        w = route[:, e:e + 1]
        ye = _dot(h[:, e * EXPERT_HIDDEN:(e + 1) * EXPERT_HIDDEN], wd_ref[e * EXPERT_HIDDEN:(e + 1) * EXPERT_HIDDEN, :])
        y = y + jnp.where(w != 0.0, w * ye, 0.0)
    acc_ref[...] = y

    @pl.when(g == pl.num_programs(1) - 1)
    def _():
        h2 = alpha * x1_ref[...] + (1.0 + g2_ref[...]) * acc_ref[...]
        o_ref[...] = _layer_norm(h2, lng_ref[...], lnb_ref[...])


def _moe(x1, u2, route, g2, w_gate, w_up, w_down, ln_g, ln_b, rows_per_group, tm, alpha):
    n, d = x1.shape
    hid = EXPERTS_PER_GROUP * EXPERT_HIDDEN
    r = g2.shape[1]
    return pl.pallas_call(
        functools.partial(_moe_kernel, alpha=alpha),
        grid=(n // tm, MOE_GROUPS),
        in_specs=[pl.BlockSpec((tm, d), lambda i, g: (i, 0)),
                  pl.BlockSpec((tm, d), lambda i, g: (i, 0)),
                  pl.BlockSpec((None, tm, LANES), lambda i, g: (g, i, 0)),
                  pl.BlockSpec((None, r, d), lambda i, g: ((i * tm) // rows_per_group, 0, 0)),
                  pl.BlockSpec((None, d, hid), lambda i, g: (g, 0, 0)),
                  pl.BlockSpec((None, d, hid), lambda i, g: (g, 0, 0)),
                  pl.BlockSpec((None, hid, d), lambda i, g: (g, 0, 0)),
                  pl.BlockSpec((1, d), lambda i, g: (0, 0)),
                  pl.BlockSpec((1, d), lambda i, g: (0, 0))],
        out_specs=pl.BlockSpec((tm, d), lambda i, g: (i, 0)),
        out_shape=jax.ShapeDtypeStruct((n, d), F32),
        scratch_shapes=[pltpu.VMEM((tm, d), F32)],
        compiler_params=_cparams(("parallel", "arbitrary")),
        name="moe",
    )(x1, u2, route, g2, w_gate, w_up, w_down, ln_g, ln_b)


def _pad_heads(w, axis):
    shp = w.shape
    w = w.reshape(shp[:axis] + (NSA_KV_HEADS, NSA_GROUP, 1, NSA_HEAD_DIM) + shp[axis + 1:])
    sel = jnp.eye(NSA_KV_HEADS, dtype=w.dtype).reshape((1,) * axis + (NSA_KV_HEADS, 1, NSA_KV_HEADS, 1) + (1,) * (len(shp) - axis - 1))
    w = w * sel
    return w.reshape(shp[:axis] + (NSA_HEADS * LANES,) + shp[axis + 1:])


def _prep_weights(w_in, rw_w_up, rw_a_up, w_branch, moe_router_g, moe_bias_g, moe_router_e, moe_bias_e,
                  moe_w_gate, moe_w_up, moe_w_down, nsa_cmp_wk, nsa_cmp_wv):
    L, D, _ = w_in.shape
    w = {}
    w['rw'] = w_in[:, :, :RW_END].astype(BF16)
    wq = _pad_heads(w_in[:, :, RW_END:NSA_Q_END] * (NSA_HEAD_DIM ** -0.5), 2)
    wg = jnp.pad(w_in[:, :, NSA_KV_END:NSA_END], ((0, 0), (0, 0), (0, LANES - 3 * NSA_HEADS)))
    w['nsa'] = jnp.concatenate([wq, w_in[:, :, NSA_Q_END:NSA_KV_END], wg], axis=-1).astype(BF16)
    w['pool'] = w_in[:, :, NSA_END:POOL_END].astype(BF16)
    w['gate'] = w_in[:, :, POOL_END:].astype(BF16)
    z = jnp.zeros((L, HALF, RW_WIDTH), F32)
    w['wwa'] = jnp.concatenate([jnp.concatenate([rw_w_up, z], axis=2),
                                jnp.concatenate([z, rw_a_up], axis=2)], axis=1).astype(BF16)
    w['b_rw'] = w_branch[:, 0].astype(BF16)
    w['b_nsa'] = _pad_heads(w_branch[:, 1], 1).astype(BF16)
    w['b_pool'] = w_branch[:, 2].astype(BF16)
    pad_r = LANES - MOE_GROUPS - MOE_GROUPS * EXPERTS_PER_GROUP
    w['router'] = jnp.pad(jnp.concatenate([moe_router_g, moe_router_e], axis=2), ((0, 0), (0, 0), (0, pad_r))).astype(BF16)
    w['router_b'] = jnp.pad(jnp.concatenate([moe_bias_g, moe_bias_e], axis=1), ((0, 0), (0, pad_r)))[:, None, :]
    hid = EXPERTS_PER_GROUP * EXPERT_HIDDEN
    grp = lambda a: jnp.transpose(a.reshape(L, MOE_GROUPS, EXPERTS_PER_GROUP, D, EXPERT_HIDDEN),
                                  (0, 1, 3, 2, 4)).reshape(L, MOE_GROUPS, D, hid).astype(BF16)
    w['moe_gate'] = grp(moe_w_gate)
    w['moe_up'] = grp(moe_w_up)
    w['moe_down'] = moe_w_down.reshape(L, MOE_GROUPS, hid, D).astype(BF16)
    w['cmp_wk'] = jnp.tile(nsa_cmp_wk, (1, 1, NSA_KV_HEADS))
    w['cmp_wv'] = jnp.tile(nsa_cmp_wv, (1, 1, NSA_KV_HEADS))
    return w


def kernel(x_prompt, x_sample, cache_nsa_kv, cache_win_kv, state_rwkv, state_rwkv_shift, state_pool, page_table, c_prompt, c_sample, ada_w, ada_b, w_in, rw_mu, rw_w0, rw_w_up, rw_a0, rw_a_up, rw_g_up, rw_k_k, rw_k_a, rw_r_k, rw_gn_g, rw_gn_b, nsa_cmp_wk, nsa_cmp_wv, pool_w, pool_scale, w_branch, w_out, ln1_g, ln1_b, moe_router_g, moe_bias_g, moe_router_e, moe_bias_e, moe_w_gate, moe_w_up, moe_w_down, ln2_g, ln2_b):
    bp, t, d = x_prompt.shape
    bs, ts, _ = x_sample.shape
    L = ada_w.shape[0]
    n_pool, page = cache_nsa_kv.shape[1], cache_nsa_kv.shape[2]
    n_pages = page_table.shape[1]
    past_len = n_pages * page
    assert ts == 1 and d == D_MODEL
    assert t % 256 == 0 and t // CMP_BLOCK <= LANES and past_len % SEL_BLOCK == 0
    assert cache_win_kv.shape[2] == WINDOW and bs % 8 == 0 and 2 * n_pages == LANES
    alpha = (2 * L) ** 0.25
    tm = 256
    chunk_pad = RW_CHUNK

    w = _prep_weights(w_in, rw_w_up, rw_a_up, w_branch, moe_router_g, moe_bias_g, moe_router_e, moe_bias_e,
                      moe_w_gate, moe_w_up, moe_w_down, nsa_cmp_wk, nsa_cmp_wv)
    w_out_b = w_out.astype(BF16)
    pool_w_b = pool_w.astype(BF16)
    gup_b = rw_g_up.astype(BF16)

    nb = bp + bs
    nb_pad = -(-nb // 8) * 8
    c_all = jnp.pad(jnp.concatenate([c_prompt, c_sample], axis=0), ((0, nb_pad - nb), (0, 0)))
    mods = _ada_mod(c_all, ada_w, ada_b)

    cache_t = jnp.transpose(cache_nsa_kv, (0, 1, 3, 4, 5, 2)).astype(F32)
    win_cache = jnp.transpose(cache_win_kv, (0, 1, 3, 4, 5, 2)).astype(F32)
    pb = 8 if n_pool % 8 == 0 else 1
    cmp_w_t = jnp.tile(jnp.transpose(jnp.stack([nsa_cmp_wk, nsa_cmp_wv], axis=1), (0, 1, 3, 2)),
                       (1, 1, NSA_KV_HEADS, page // CMP_BLOCK))
    cmp_pool = _compress_pool(cache_t, cmp_w_t, pb)
    slopes = jnp.broadcast_to(jnp.asarray(NSA_SLOPES, F32)[:, None], (NSA_HEADS, LANES))

    xp = x_prompt.reshape(bp * t, d)
    xs = x_sample.reshape(bs, d)
    zeros_prev = jnp.zeros((bp, RW_COLS), F32)
    zeros_hist = jnp.zeros((bp, 16, POOL_WIDTH), F32)
    outs = {k: [] for k in ('nsa_p', 'nsa_s', 'win_p', 'win_s', 'rw_p', 'rw_s', 'sh_p', 'sh_s', 'pool_p', 'pool_s')}

    for l in range(L):
        vec = lambda a: a[l].reshape(1, -1)
        rw_wts = (vec(rw_mu), vec(rw_w0), vec(rw_a0), w['wwa'][l], gup_b[l], vec(rw_k_k), vec(rw_k_a), vec(rw_r_k))
        merge_wts = (w['gate'][l], w['b_rw'][l], w['b_nsa'][l], w['b_pool'][l], w_out_b[l], vec(ln1_g), vec(ln1_b),
                     w['router'][l], w['router_b'][l])
        mod_p = [m[:, None, :] for m in jnp.split(mods[l, :bp], 6, axis=-1)]
        mod_s = [m[None] for m in jnp.split(mods[l, bp:bp + bs], 6, axis=-1)]

        p_rw = _proj(xp, mod_p[1], mod_p[0], w['rw'][l], t, tm)
        q_pad, kv, gates, kvt, cmp = _proj_nsa(xp, mod_p[1], mod_p[0], w['nsa'][l], w['cmp_wk'][l], w['cmp_wv'][l], bp, t, tm)
        p_pool = _proj(xp, mod_p[1], mod_p[0], w['pool'][l], t, tm)
        seqs = _rwkv_prep(p_rw, zeros_prev, rw_wts, t, tm)
        o_rw, s_pairs = _rwkv_chunk(seqs, vec(rw_gn_g), vec(rw_gn_b), bp, t)
        o_nsa = _nsa_prompt(q_pad, kvt, _permute_cmp(cmp, bp, t), gates, bp, t)
        o_pool = _pool_seq(p_pool, zeros_hist, pool_w_b[l], vec(pool_scale), t, tm, 0)
        x1, u2, route = _merge(xp, (mod_p[1], mod_p[0], mod_p[2], mod_p[4], mod_p[3]), o_rw, o_nsa, o_pool,
                               merge_wts, t, tm, alpha)
        xp = _moe(x1, u2, route, mod_p[5], w['moe_gate'][l], w['moe_up'][l], w['moe_down'][l],
                  vec(ln2_g), vec(ln2_b), t, tm, alpha)
        kv3 = kv.reshape(bp, t, 6, NSA_KV_HEADS, NSA_HEAD_DIM)
        outs['nsa_p'].append(kv3[:, :, :4])
        outs['win_p'].append(kv3[:, t - min(WINDOW, t):, 4:])
        outs['rw_p'].append(_state_from_pairs(s_pairs))
        outs['sh_p'].append(p_rw.reshape(bp, t, RW_COLS)[:, -1])
        outs['pool_p'].append(p_pool.reshape(bp, t, POOL_WIDTH)[:, t - POOL_HIST:])

        p_rw = _proj(xs, mod_s[1], mod_s[0], w['rw'][l], bs, bs)
        q_pad, kv, gates = _proj_nsa(xs, mod_s[1], mod_s[0], w['nsa'][l], w['cmp_wk'][l], w['cmp_wv'][l], bs, 1, bs)
        p_pool = _proj(xs, mod_s[1], mod_s[0], w['pool'][l], bs, bs)
        seqs = _rwkv_prep(p_rw, state_rwkv_shift[l], rw_wts, 1, bs)
        o_rw, s_pairs = _rwkv_step(seqs, _state_to_pairs(state_rwkv[l]), vec(rw_gn_g), vec(rw_gn_b))
        o_nsa = _nsa_decode(page_table, cmp_pool, cache_t, win_cache, l, q_pad, gates, kv, slopes, past_len)
        z16 = jnp.concatenate([state_pool[l].astype(F32), p_pool[:, None, :]], axis=1)
        o_pool = _pool_step(z16, pool_w_b[l], vec(pool_scale), past_len)
        x1, u2, route = _merge(xs, (mod_s[1], mod_s[0], mod_s[2], mod_s[4], mod_s[3]), o_rw, o_nsa, o_pool,
                               merge_wts, bs, bs, alpha)
        xs = _moe(x1, u2, route, mod_s[5], w['moe_gate'][l], w['moe_up'][l], w['moe_down'][l],
                  vec(ln2_g), vec(ln2_b), bs, bs, alpha)
        kv3 = kv.reshape(bs, 1, 6, NSA_KV_HEADS, NSA_HEAD_DIM)
        outs['nsa_s'].append(kv3[:, :, :4])
        outs['win_s'].append(jnp.concatenate([cache_win_kv[l].astype(F32), kv3[:, :, 4:]], axis=1)[:, -WINDOW:])
        outs['rw_s'].append(_state_from_pairs(s_pairs))
        outs['sh_s'].append(p_rw)
        outs['pool_s'].append(z16[:, 1:])

    st = lambda k: jnp.stack(outs[k])
    return (xp.reshape(bp, t, d), xs.reshape(bs, 1, d), st('nsa_p'), st('nsa_s'), st('win_p'), st('win_s'),
            st('rw_p'), st('rw_s'), st('sh_p'), st('sh_s'), st('pool_p'), st('pool_s'))
```

```python
import functools

import jax
import jax.numpy as jnp
from jax import lax
from jax.experimental import pallas as pl
from jax.experimental.pallas import tpu as pltpu

F32 = jnp.float32
BF16 = jnp.bfloat16

D_MODEL = 1024
RW_HEADS = 8
RW_HEAD_DIM = 64
RW_WIDTH = 512
RW_COLS = 1792
RW_GN_EPS = 64e-5
RW_CHUNK = 64

NSA_HEADS = 8
NSA_KV_HEADS = 2
NSA_HEAD_DIM = 64
NSA_GROUP = 4
CMP_BLOCK = 32
SEL_BLOCK = 64
TOP_N = 8
WINDOW = 512
NSA_TQ = 128
NSA_TK = 512
NEG_INF = -1e30
BELOW_NEG_INF = -3e38
FORCE_BONUS = 1e9
NSA_SLOPES = tuple(2.0 ** (-8.0 * (h + 1) / NSA_HEADS) for h in range(NSA_HEADS))

POOL_WINDOWS = (2, 4, 8, 16)
POOL_HIST = 15
POOL_WIDTH = 512

RW_END = 1792
NSA_Q_END = RW_END + 512
NSA_KV_END = NSA_Q_END + 768
NSA_END = NSA_KV_END + 24
POOL_END = NSA_END + 512

MOE_GROUPS = 4
EXPERTS_PER_GROUP = 4
EXPERT_HIDDEN = 256
ROUTE_LANE0 = 4

LN_EPS = 1e-5
LANES = 128
HALF = 64


def _cparams(sem, vmem_mb=48):
    return pltpu.CompilerParams(dimension_semantics=sem, vmem_limit_bytes=vmem_mb * 1024 * 1024)


def _dot(a, b):
    return jnp.dot(a, b, preferred_element_type=F32)


def _dot_nt(a, b):
    return lax.dot_general(a, b, (((1,), (1,)), ((), ())), preferred_element_type=F32)


def _dot_tn(a, b):
    return lax.dot_general(a, b, (((0,), (0,)), ((), ())), preferred_element_type=F32)


def _sigmoid(x):
    return 1.0 / (1.0 + jnp.exp(-x))


def _softplus(x):
    return jnp.maximum(x, 0.0) + jnp.log(1.0 + jnp.exp(-jnp.abs(x)))


def _layer_norm(h, g, b):
    mu = jnp.mean(h, axis=-1, keepdims=True)
    d = h - mu
    var = jnp.mean(d * d, axis=-1, keepdims=True)
    return d * lax.rsqrt(var + LN_EPS) * g + b


def _half_sum(x, lo):
    s_lo = jnp.sum(jnp.where(lo, x, 0.0), axis=1, keepdims=True)
    s_hi = jnp.sum(jnp.where(lo, 0.0, x), axis=1, keepdims=True)
    return jnp.where(lo, s_lo, s_hi)


def _ada_kernel(c_ref, w_ref, b_ref, o_ref):
    c = c_ref[...]
    s = (c * _sigmoid(c)).astype(BF16)
    o_ref[...] = _dot(s, w_ref[...].astype(BF16)) + b_ref[...]


def _ada_mod(c_all, ada_w, ada_b):
    L, D, D6 = ada_w.shape
    nb = c_all.shape[0]
    return pl.pallas_call(
        _ada_kernel,
        grid=(L, D6 // D),
        in_specs=[pl.BlockSpec((nb, D), lambda l, j: (0, 0)),
                  pl.BlockSpec((None, D, D), lambda l, j: (l, 0, j)),
                  pl.BlockSpec((None, 1, D), lambda l, j: (l, 0, j))],
        out_specs=pl.BlockSpec((None, nb, D), lambda l, j: (l, 0, j)),
        out_shape=jax.ShapeDtypeStruct((L, nb, D6), F32),
        compiler_params=_cparams(("parallel", "parallel")),
        name="ada_mod",
    )(c_all, ada_w, ada_b.reshape(L, 1, D6))


def _mod_spec(mod, rows_per_group, tm):
    r = mod.shape[1]
    return pl.BlockSpec((None, r, mod.shape[2]), lambda i: ((i * tm) // rows_per_group, 0, 0))


def _proj_kernel(x_ref, sc_ref, sh_ref, w_ref, o_ref):
    u = (x_ref[...] * (1.0 + sc_ref[...]) + sh_ref[...]).astype(BF16)
    o_ref[...] = _dot(u, w_ref[...])


def _proj(x, sc, sh, w, rows_per_group, tm):
    n, d = x.shape
    nc = w.shape[1]
    return pl.pallas_call(
        _proj_kernel,
        grid=(n // tm,),
        in_specs=[pl.BlockSpec((tm, d), lambda i: (i, 0)),
                  _mod_spec(sc, rows_per_group, tm), _mod_spec(sh, rows_per_group, tm),
                  pl.BlockSpec((d, nc), lambda i: (0, 0))],
        out_specs=pl.BlockSpec((tm, nc), lambda i: (i, 0)),
        out_shape=jax.ShapeDtypeStruct((n, nc), F32),
        compiler_params=_cparams(("parallel",)),
        name="proj",
    )(x, sc, sh, w)


def _proj_nsa_kernel(x_ref, sc_ref, sh_ref, w_ref, wk_ref, wv_ref, q_ref, kv_ref, gate_ref, kvt_ref=None, cmp_ref=None):
    u = (x_ref[...] * (1.0 + sc_ref[...]) + sh_ref[...]).astype(BF16)
    res = _dot(u, w_ref[...])
    q_ref[...] = res[:, :1024].astype(BF16)
    kv = res[:, 1024:1792]
    kv_ref[...] = kv
    gate_ref[...] = _sigmoid(res[:, 1792:1920])
    if kvt_ref is None:
        return
    for j in range(6):
        kvt_ref[j] = kv[:, j * LANES:(j + 1) * LANES].astype(BF16)
    nb = kv.shape[0] // CMP_BLOCK
    kc = jnp.sum(kv[:, 0:LANES].reshape(nb, CMP_BLOCK, LANES) * wk_ref[...][None], axis=1)
    vc = jnp.sum(kv[:, LANES:2 * LANES].reshape(nb, CMP_BLOCK, LANES) * wv_ref[...][None], axis=1)
    cmp_ref[:, 0:LANES] = kc
    cmp_ref[:, LANES:2 * LANES] = vc


def _proj_nsa(x, sc, sh, w, wk, wv, n_seq, t, tm):
    n, d = x.shape
    out_specs = [pl.BlockSpec((tm, 1024), lambda i: (i, 0)),
                 pl.BlockSpec((tm, 768), lambda i: (i, 0)),
                 pl.BlockSpec((tm, LANES), lambda i: (i, 0))]
    out_shape = [jax.ShapeDtypeStruct((n, 1024), BF16),
                 jax.ShapeDtypeStruct((n, 768), F32),
                 jax.ShapeDtypeStruct((n, LANES), F32)]
    if t > 1:
        tps = t // tm
        nb = tm // CMP_BLOCK
        out_specs += [pl.BlockSpec((None, 6, tm, LANES), lambda i: (i // tps, 0, i % tps, 0)),
                      pl.BlockSpec((nb, 2 * LANES), lambda i: (i, 0))]
        out_shape += [jax.ShapeDtypeStruct((n_seq, 6, t, LANES), BF16),
                      jax.ShapeDtypeStruct((n // CMP_BLOCK, 2 * LANES), F32)]
    rows_per_group = t if sc.shape[1] == 1 else n
    return pl.pallas_call(
        _proj_nsa_kernel,
        grid=(n // tm,),
        in_specs=[pl.BlockSpec((tm, d), lambda i: (i, 0)),
                  _mod_spec(sc, rows_per_group, tm), _mod_spec(sh, rows_per_group, tm),
                  pl.BlockSpec((d, 1920), lambda i: (0, 0)),
                  pl.BlockSpec((CMP_BLOCK, LANES), lambda i: (0, 0)),
                  pl.BlockSpec((CMP_BLOCK, LANES), lambda i: (0, 0))],
        out_specs=out_specs,
        out_shape=out_shape,
        compiler_params=_cparams(("parallel",)),
        name="proj_nsa",
    )(x, sc, sh, w, wk, wv)


def _rwkv_prep_body(p, prev, mu_ref, w0_ref, a0_ref, wwa_ref, gup_ref, kkw_ref, kaw_ref, rk_ref, outs):
    r_ref, lw_ref, k_ref, v_ref, kk_ref, kka_ref, g_ref, bonus_ref = outs
    xs = p + (prev - p) * mu_ref[...]
    r = xs[:, 0:512]
    xk = xs[:, 512:1024]
    v = xs[:, 1024:1536]
    t12 = xs[:, 1536:1664]
    gd = xs[:, 1664:1792]
    lane = lax.broadcasted_iota(jnp.int32, (1, LANES), 1)
    lo = lane < HALF
    z = jnp.where(lo, jnp.tanh(t12), t12).astype(BF16)
    dwa = _dot(z, wwa_ref[...])
    w_log = -_softplus(-(w0_ref[...] + dwa[:, :512])) - 0.5
    a = _sigmoid(a0_ref[...] + dwa[:, 512:])
    g_ref[...] = _dot(_sigmoid(gd).astype(BF16), gup_ref[...])
    kmod = xk * (1.0 + (a - 1.0) * kaw_ref[...])
    kkr = xk * kkw_ref[...]
    rkr = r * kmod * rk_ref[...]
    for m in range(4):
        sl = slice(m * LANES, (m + 1) * LANES)
        x = kkr[:, sl]
        nrm = jnp.sqrt(_half_sum(x * x, lo))
        kk = x / jnp.maximum(nrm, 1e-12)
        kk_ref[:, sl] = kk
        kka_ref[:, sl] = kk * a[:, sl]
        bonus_ref[:, sl] = _half_sum(rkr[:, sl], lo) * v[:, sl]
    r_ref[...] = r
    lw_ref[...] = -jnp.exp(w_log)
    k_ref[...] = kmod
    v_ref[...] = v


def _rwkv_prep_seq_kernel(p_ref, pprev_ref, prow_ref, *rest, seq_tiles):
    i = pl.program_id(0)
    p = p_ref[...]
    first = (i % seq_tiles) == 0
    prev_row = jnp.where(first, prow_ref[...], pprev_ref[7:8, :])
    rolled = pltpu.roll(p, 1, 0)
    rowid = lax.broadcasted_iota(jnp.int32, (p.shape[0], 1), 0)
    prev = jnp.where(rowid == 0, prev_row, rolled)
    _rwkv_prep_body(p, prev, *rest[:8], rest[8:])


def _rwkv_prep_step_kernel(p_ref, prev_ref, *rest):
    _rwkv_prep_body(p_ref[...], prev_ref[...], *rest[:8], rest[8:])


def _rwkv_prep(p_rw, prev_rows, wts, t, tm):
    n = p_rw.shape[0]
    vec = lambda c: pl.BlockSpec((1, c), lambda i: (0, 0))
    w_specs = [vec(RW_COLS), vec(512), vec(512), pl.BlockSpec((LANES, 1024), lambda i: (0, 0)),
               pl.BlockSpec((LANES, 512), lambda i: (0, 0)), vec(512), vec(512), vec(512)]
    out_specs = [pl.BlockSpec((tm, 512), lambda i: (i, 0))] * 8
    out_shape = [jax.ShapeDtypeStruct((n, 512), F32)] * 8
    if t == 1:
        kern = _rwkv_prep_step_kernel
        in_specs = [pl.BlockSpec((tm, RW_COLS), lambda i: (i, 0)), pl.BlockSpec((tm, RW_COLS), lambda i: (i, 0))]
        args = (p_rw, prev_rows)
    else:
        tps = t // tm
        kern = functools.partial(_rwkv_prep_seq_kernel, seq_tiles=tps)
        in_specs = [pl.BlockSpec((tm, RW_COLS), lambda i: (i, 0)),
                    pl.BlockSpec((8, RW_COLS), lambda i: (jnp.maximum(i * (tm // 8) - 1, 0), 0)),
                    pl.BlockSpec((None, 1, RW_COLS), lambda i: (i // tps, 0, 0))]
        args = (p_rw, p_rw, prev_rows[:, None, :])
    return pl.pallas_call(
        kern, grid=(n // tm,), in_specs=in_specs + w_specs, out_specs=out_specs, out_shape=out_shape,
        compiler_params=_cparams(("parallel",)), name="rwkv_prep",
    )(*args, *wts)


def _split3(x):
    x1 = x.astype(BF16)
    r1 = x - x1.astype(F32)
    x2 = r1.astype(BF16)
    x3 = (r1 - x2.astype(F32)).astype(BF16)
    return x1, x2, x3


def _rwkv_chunk_kernel(r_ref, lw_ref, k_ref, v_ref, kk_ref, kka_ref, g_ref, bonus_ref, gng_ref, gnb_ref,
                       o_ref, sout_ref, s_ref):
    c = pl.program_id(1)
    C = RW_CHUNK

    @pl.when(c == 0)
    def _():
        s_ref[...] = jnp.zeros(s_ref.shape, F32)

    ns = lw_ref.shape[0]
    flat = lambda ref: ref[...].reshape(ns * C, ref.shape[-1])
    lw = flat(lw_ref)
    row = lax.broadcasted_iota(jnp.int32, (ns * C, ns * C), 0)
    col = lax.broadcasted_iota(jnp.int32, (ns * C, ns * C), 1)
    tri = ((row >= col) & (row // C == col // C)).astype(BF16)
    l1, l2, l3 = _split3(lw)
    cl = _dot(tri, l1) + _dot(tri, l2) + _dot(tri, l3)
    cl_last = jnp.concatenate([jnp.broadcast_to(cl[(q + 1) * C - 1:(q + 1) * C, :], (C, cl.shape[1]))
                               for q in range(ns)], axis=0)
    g_in = jnp.exp(cl)
    g_ex = jnp.exp(cl - lw)
    g_inv = jnp.exp(-cl)
    g_rem = jnp.exp(cl_last - cl)
    gc = jnp.exp(cl_last)
    k = flat(k_ref)
    kka = flat(kka_ref)
    qk_all = flat(kk_ref) * g_ex
    r_all = flat(r_ref) * g_in
    kt_all = k * g_inv
    at_all = kka * g_inv
    kd_all = k * g_rem
    ad_all = kka * g_rem
    v_all = flat(v_ref)

    lane = lax.broadcasted_iota(jnp.int32, (1, LANES), 1)
    lo = lane < HALF

    def st(x):
        return jnp.concatenate([jnp.where(lo, x, 0.0), jnp.where(lo, 0.0, x)], axis=0)

    r2 = lax.broadcasted_iota(jnp.int32, (4 * C, 4 * C), 0)
    c2 = lax.broadcasted_iota(jnp.int32, (4 * C, 4 * C), 1)
    rt = r2 % C
    ct = c2 % C
    tmask = (ct < rt) | ((r2 >= 2 * C) & (ct == rt))
    ri = lax.broadcasted_iota(jnp.int32, (2 * C, 2 * C), 0)
    ci = lax.broadcasted_iota(jnp.int32, (2 * C, 2 * C), 1)
    eye = (ri == ci).astype(F32)

    units = [(q, m) for q in range(ns) for m in range(4)]
    pairs = range(len(units))
    cut = lambda x, u: x[units[u][0] * C:(units[u][0] + 1) * C, units[u][1] * LANES:(units[u][1] + 1) * LANES]
    sls = [slice(m * LANES, (m + 1) * LANES) for _, m in units]
    bf = lambda x: x.astype(BF16)
    qk_st = [st(cut(qk_all, u)) for u in pairs]
    r_st = [st(cut(r_all, u)) for u in pairs]
    v_st = [st(cut(v_all, u)) for u in pairs]
    ad_st = [st(cut(ad_all, u)) for u in pairs]
    kd_st = [st(cut(kd_all, u)) for u in pairs]
    xm = [jnp.where(tmask, _dot_nt(bf(jnp.concatenate([qk_st[m], r_st[m]], axis=0)),
                                   bf(jnp.concatenate([st(cut(at_all, m)), st(cut(kt_all, m))], axis=0))), 0.0)
          for m in pairs]
    av = [_dot(bf(xm[m][:, 2 * C:]), bf(v_st[m])) for m in pairs]
    p = [xm[m][:2 * C, :2 * C] for m in pairs]
    tm_ = [eye - p[m] for m in pairs]
    for _ in range(5):
        p = [_dot(bf(p[m]), bf(p[m])) for m in pairs]
        tm_ = [_dot(bf(tm_[m]), bf(eye + p[m])) for m in pairs]
    wu = [_dot(bf(tm_[m]), bf(jnp.concatenate([qk_st[m], av[m][:2 * C]], axis=1))) for m in pairs]
    ry = [_dot(bf(xm[m][2 * C:, :2 * C]), bf(wu[m])) for m in pairs]
    rq = [r_st[m] - ry[m][:, :LANES] for m in pairs]
    y0 = [av[m][2 * C:] - ry[m][:, LANES:] for m in pairs]
    gt = [eye * cut(gc, m)[0:1] - _dot_tn(bf(wu[m][:, :LANES]), bf(ad_st[m])) for m in pairs]
    ht = [_dot_tn(bf(jnp.concatenate([v_st[m], wu[m][:, LANES:]], axis=0)),
                  bf(jnp.concatenate([kd_st[m], -ad_st[m]], axis=0))) for m in pairs]
    sb = [bf(s_ref[q, m]) for q, m in units]
    y_st = [_dot_nt(bf(rq[m]), sb[m]) + y0[m] for m in pairs]
    s_new = [_dot(sb[m], bf(gt[m])) + ht[m] for m in pairs]
    for u, (q, m) in enumerate(units):
        sl = sls[u]
        s_ref[q, m] = s_new[u]
        sout_ref[q, m] = s_new[u][:C] + s_new[u][C:]
        y = y_st[u][:C] + y_st[u][C:]
        o_ref[q, :, sl] = _rwkv_out(y, lo, gng_ref[:, sl], gnb_ref[:, sl], bonus_ref[q, :, sl], g_ref[q, :, sl])


def _rwkv_out(y, lo, gn_g, gn_b, bonus, g):
    mu = _half_sum(y, lo) * (1.0 / RW_HEAD_DIM)
    d = y - mu
    var = _half_sum(d * d, lo) * (1.0 / RW_HEAD_DIM)
    yn = d * lax.rsqrt(var + RW_GN_EPS) * gn_g + gn_b
    return ((yn + bonus) * g).astype(BF16)


def _rwkv_chunk(seqs, gn_g, gn_b, n_seq, t):
    C = RW_CHUNK
    nch = t // C
    ns = 2 if n_seq % 2 == 0 else 1
    row_spec = pl.BlockSpec((ns, C, 512), lambda b, c: (b, c, 0))
    vec = pl.BlockSpec((1, 512), lambda b, c: (0, 0))
    o, s = pl.pallas_call(
        _rwkv_chunk_kernel,
        grid=(n_seq // ns, nch),
        in_specs=[row_spec] * 8 + [vec, vec],
        out_specs=[row_spec, pl.BlockSpec((ns, 4, RW_HEAD_DIM, LANES), lambda b, c: (b, 0, 0, 0))],
        out_shape=[jax.ShapeDtypeStruct((n_seq, t, 512), BF16),
                   jax.ShapeDtypeStruct((n_seq, 4, RW_HEAD_DIM, LANES), F32)],
        scratch_shapes=[pltpu.VMEM((ns, 4, LANES, LANES), F32)],
        compiler_params=_cparams(("parallel", "arbitrary")),
        name="rwkv_chunk",
    )(*[a.reshape(n_seq, t, 512) for a in seqs], gn_g, gn_b)
    return o.reshape(n_seq * t, 512), s


def _rwkv_step_kernel(r_ref, lw_ref, k_ref, v_ref, kk_ref, kka_ref, g_ref, bonus_ref, s_ref, gng_ref, gnb_ref,
                      o_ref, sout_ref):
    lane = lax.broadcasted_iota(jnp.int32, (1, LANES), 1)
    lo = lane < HALF
    rowi = lax.broadcasted_iota(jnp.int32, (RW_HEAD_DIM, LANES), 0)
    diag = (rowi == lax.broadcasted_iota(jnp.int32, (RW_HEAD_DIM, LANES), 1) % HALF).astype(F32)
    for m in range(4):
        sl = slice(m * LANES, (m + 1) * LANES)
        s = s_ref[m]
        sk = _half_sum(s * kk_ref[:, sl], lo)
        v_col = _half_sum(diag * v_ref[:, sl], lo)
        s_new = s * jnp.exp(lw_ref[:, sl]) - sk * kka_ref[:, sl] + v_col * k_ref[:, sl]
        sout_ref[m] = s_new
        y_col = _half_sum(s_new * r_ref[:, sl], lo)
        y = jnp.sum(diag * y_col, axis=0, keepdims=True)
        o_ref[:, sl] = _rwkv_out(y, lo, gng_ref[:, sl], gnb_ref[:, sl], bonus_ref[:, sl], g_ref[:, sl])


def _rwkv_step(seqs, s_pairs, gn_g, gn_b):
    bs = s_pairs.shape[0]
    row_spec = pl.BlockSpec((None, 1, 512), lambda b: (b, 0, 0))
    st_spec = pl.BlockSpec((None, 4, RW_HEAD_DIM, LANES), lambda b: (b, 0, 0, 0))
    vec = pl.BlockSpec((1, 512), lambda b: (0, 0))
    o, s = pl.pallas_call(
        _rwkv_step_kernel,
        grid=(bs,),
        in_specs=[row_spec] * 8 + [st_spec, vec, vec],
        out_specs=[row_spec, st_spec],
        out_shape=[jax.ShapeDtypeStruct((bs, 1, 512), BF16),
                   jax.ShapeDtypeStruct((bs, 4, RW_HEAD_DIM, LANES), F32)],
        compiler_params=_cparams(("parallel",)),
        name="rwkv_step",
    )(*[a[:, None, :] for a in seqs], s_pairs, gn_g, gn_b)
    return o[:, 0], s


def _state_to_pairs(s):
    b = s.shape[0]
    return jnp.transpose(s.reshape(b, 4, 2, 64, 64).astype(F32), (0, 1, 3, 2, 4)).reshape(b, 4, 64, LANES)


def _state_from_pairs(sp):
    b = sp.shape[0]
    return jnp.transpose(sp.reshape(b, 4, 64, 2, 64), (0, 1, 3, 2, 4)).reshape(b, 8, 64, 64)


def _nsa_prompt_kernel(q_ref, kvt_ref, cmp_ref, gate_ref, o_ref, m_ref, acc_ref, *, k_top, tk):
    qi = pl.program_id(1)
    tq = NSA_TQ
    q0 = qi * tq
    lane = lax.broadcasted_iota(jnp.int32, (1, LANES), 1)
    lo = lane < HALF
    tok = q0 + lax.broadcasted_iota(jnp.int32, (tq, 1), 0)
    cmpv = cmp_ref[...]
    kc = cmpv[:, :LANES].astype(BF16)
    vc = cmpv[:, LANES:].astype(BF16)
    gates = gate_ref[...]
    blk_c = 2 * (lane % HALF) + lane // HALF
    dist_c = tok - (blk_c * CMP_BLOCK + (CMP_BLOCK - 1))
    mask_c = dist_c >= 0
    dist_cf = dist_c.astype(F32)
    cur = tok // SEL_BLOCK
    valid = (lane * SEL_BLOCK <= tok) & lo
    forced = (lane == 0) | (lane == cur) | (lane == cur - 1)
    in_grp = (lo, jnp.logical_not(lo))
    den_lane = (HALF, 0)
    hrows = lambda h: slice(h * tq, (h + 1) * tq)
    grows = lambda g: slice(g * NSA_GROUP * tq, (g + 1) * NSA_GROUP * tq)
    q_all = jnp.concatenate([q_ref[:, h * LANES:(h + 1) * LANES] for h in range(NSA_HEADS)], axis=0)

    s = _dot_nt(q_all, kc)
    ps = []
    imp = [jnp.zeros((tq, LANES), F32) for _ in range(NSA_KV_HEADS)]
    for h in range(NSA_HEADS):
        sr = jnp.where(mask_c, s[hrows(h)] - NSA_SLOPES[h] * dist_cf, NEG_INF)
        mx = jnp.max(sr, axis=1, keepdims=True)
        e = jnp.where(mask_c, jnp.exp(sr - mx), 0.0)
        p = e / jnp.maximum(jnp.sum(e, axis=1, keepdims=True), 1e-30)
        ps.append(p.astype(BF16))
        imp[h // NSA_GROUP] = imp[h // NSA_GROUP] + p
    o_c = _dot(jnp.concatenate(ps, axis=0), vc)

    score, sel = [], []
    for g in range(NSA_KV_HEADS):
        imp_sel = imp[g] + pltpu.roll(imp[g], HALF, 1)
        sc = jnp.where(valid, imp_sel + jnp.where(forced, FORCE_BONUS, 0.0), NEG_INF)
        score.append(jnp.where(lo, sc, BELOW_NEG_INF))
        sel.append(jnp.zeros((tq, LANES), F32))
    for _ in range(k_top):
        for g in range(NSA_KV_HEADS):
            mx = jnp.max(score[g], axis=1, keepdims=True)
            idx = jnp.min(jnp.where(score[g] == mx, lane, 4 * LANES), axis=1, keepdims=True)
            hit = lane == idx
            sel[g] = jnp.where(hit, 1.0, sel[g])
            score[g] = jnp.where(hit, BELOW_NEG_INF, score[g])
    sel_b = [x.astype(BF16) for x in sel]

    n_sub = tk // LANES
    blk_row = lax.broadcasted_iota(jnp.int32, (LANES, tk), 0)
    key_blk = lax.broadcasted_iota(jnp.int32, (LANES, tk), 1) // SEL_BLOCK
    col_k = lax.broadcasted_iota(jnp.int32, (1, tk), 1)
    m_ref[...] = jnp.full(m_ref.shape, NEG_INF, F32)
    acc_ref[...] = jnp.zeros(acc_ref.shape, F32)

    def sel_tile(j, carry):
        start = pl.multiple_of(j * tk, tk)
        kt = kvt_ref[2, pl.ds(start, tk), :]
        vt = kvt_ref[3, pl.ds(start, tk), :]
        s = _dot_nt(q_all, kt)
        dist = tok - (start + col_k)
        dist_f = dist.astype(F32)
        causal = jnp.where(dist >= 0, 0.0, NEG_INF)
        expand = (blk_row == (tk // SEL_BLOCK) * j + key_blk).astype(BF16)
        for g in range(NSA_KV_HEADS):
            bias = (_dot(sel_b[g], expand) - 1.0) * (-NEG_INF) + causal
            es = []
            for r in range(NSA_GROUP):
                h = NSA_GROUP * g + r
                x = s[hrows(h)] - NSA_SLOPES[h] * dist_f + bias
                xm = x[:, 0:LANES]
                for i in range(1, n_sub):
                    xm = jnp.maximum(xm, x[:, i * LANES:(i + 1) * LANES])
                m_old = m_ref[hrows(h)]
                m_new = jnp.maximum(m_old, jnp.max(xm, axis=1, keepdims=True))
                m_ref[hrows(h)] = m_new
                acc_ref[hrows(h)] = jnp.exp(m_old - m_new) * acc_ref[hrows(h)]
                es.append(jnp.exp(x - jnp.concatenate([m_new] * n_sub, axis=1)).astype(BF16))
            vg = jnp.where(in_grp[g], vt, jnp.ones_like(vt))
            acc_ref[grows(g)] += _dot(jnp.concatenate(es, axis=0), vg)
        return carry

    lax.fori_loop(0, (q0 + tq - 1) // tk + 1, sel_tile, 0)
    acc = acc_ref[...]
    o_s = [acc[grows(g)] / jnp.maximum(acc[grows(g), den_lane[g]:den_lane[g] + 1], 1e-30) for g in range(NSA_KV_HEADS)]

    wk = WINDOW + tq
    kstart = pl.multiple_of(jnp.maximum(q0 - WINDOW, 0), tq)
    kt = kvt_ref[4, pl.ds(kstart, wk), :]
    vt = kvt_ref[5, pl.ds(kstart, wk), :]
    s = _dot_nt(q_all, kt)
    dist = tok - (kstart + lax.broadcasted_iota(jnp.int32, (1, wk), 1))
    dist_f = dist.astype(F32)
    bias_w = jnp.where((dist >= 0) & (dist <= WINDOW), 0.0, NEG_INF)
    o_w = []
    for g in range(NSA_KV_HEADS):
        es = []
        for r in range(NSA_GROUP):
            h = NSA_GROUP * g + r
            x = s[hrows(h)] - NSA_SLOPES[h] * dist_f + bias_w
            e = jnp.exp(x - jnp.max(x, axis=1, keepdims=True))
            es.append((e / jnp.sum(e, axis=1, keepdims=True)).astype(BF16))
        o_w.append(_dot(jnp.concatenate(es, axis=0), vt))

    for h in range(NSA_HEADS):
        g, r = divmod(h, NSA_GROUP)
        rows = slice(r * tq, (r + 1) * tq)
        o = (gates[:, 3 * h:3 * h + 1] * o_c[hrows(h)] + gates[:, 3 * h + 1:3 * h + 2] * o_s[g][rows]
             + gates[:, 3 * h + 2:3 * h + 3] * o_w[g][rows])
        o_ref[:, h * LANES:(h + 1) * LANES] = jnp.where(in_grp[g], o, 0.0).astype(BF16)


def _nsa_prompt(q_pad, kvt, cmp_perm, gates, n_seq, t):
    tq = NSA_TQ
    nq = t // tq
    k_top = min(TOP_N, t // SEL_BLOCK)
    assert k_top >= 3 and t >= WINDOW + tq and t % NSA_TK == 0
    return pl.pallas_call(
        functools.partial(_nsa_prompt_kernel, k_top=k_top, tk=NSA_TK),
        grid=(n_seq, nq),
        in_specs=[pl.BlockSpec((tq, 1024), lambda b, i: (b * nq + i, 0)),
                  pl.BlockSpec((None, 6, t, LANES), lambda b, i: (b, 0, 0, 0)),
                  pl.BlockSpec((None, LANES, 2 * LANES), lambda b, i: (b, 0, 0)),
                  pl.BlockSpec((tq, LANES), lambda b, i: (b * nq + i, 0))],
        out_specs=pl.BlockSpec((tq, 1024), lambda b, i: (b * nq + i, 0)),
        out_shape=jax.ShapeDtypeStruct((n_seq * t, 1024), BF16),
        scratch_shapes=[pltpu.VMEM((NSA_HEADS * tq, LANES), F32), pltpu.VMEM((NSA_HEADS * tq, LANES), F32)],
        compiler_params=_cparams(("parallel", "parallel")),
        name="nsa_prompt",
    )(q_pad, kvt, cmp_perm, gates)


def _permute_cmp(cmp, n_seq, t):
    nc = t // CMP_BLOCK
    c = cmp.reshape(n_seq, nc // 2, 2, 2 * LANES)
    c = jnp.pad(c, ((0, 0), (0, HALF - nc // 2), (0, 0), (0, 0)))
    return jnp.transpose(c, (0, 2, 1, 3)).reshape(n_seq, LANES, 2 * LANES)


def _compress_pool_kernel(x_ref, w_ref, o_ref):
    pb = x_ref.shape[0]
    page = x_ref.shape[-1]
    nb = page // CMP_BLOCK
    blk = lax.broadcasted_iota(jnp.int32, (8, page), 0)
    seg = (lax.broadcasted_iota(jnp.int32, (8, page), 1) // CMP_BLOCK == blk).astype(BF16)
    items = [(p, j) for p in range(pb) for j in range(2)]
    parts = [_split3(x_ref[p, j].reshape(LANES, page) * w_ref[j]) for p, j in items]
    sums = [_dot_nt(seg, k1) + _dot_nt(seg, k2) + _dot_nt(seg, k3) for k1, k2, k3 in parts]
    for (p, j), res in zip(items, sums):
        o_ref[p, j * nb:(j + 1) * nb, :] = res[0:nb]


def _compress_pool(cache_t, w_t, pb):
    L, n_pool = cache_t.shape[:2]
    page = cache_t.shape[-1]
    nb = page // CMP_BLOCK
    return pl.pallas_call(
        _compress_pool_kernel,
        grid=(L, n_pool // pb),
        in_specs=[pl.BlockSpec((None, pb, 2, NSA_KV_HEADS, NSA_HEAD_DIM, page), lambda l, i: (l, i, 0, 0, 0, 0)),
                  pl.BlockSpec((None, 2, LANES, page), lambda l, i: (l, 0, 0, 0))],
        out_specs=pl.BlockSpec((None, pb, 2 * nb, LANES), lambda l, i: (l, i, 0, 0)),
        out_shape=jax.ShapeDtypeStruct((L, n_pool, 2 * nb, LANES), F32),
        compiler_params=_cparams(("parallel", "parallel")),
        name="compress_pool",
    )(cache_t, w_t)


def _nsa_dec_cmp_kernel(pt_ref, pool_ref, q_ref, gate_ref, slope_ref, win_ref, kvn_ref, part_ref, idx_ref, gath_ref,
                        *, n_pages, past_len, k_top):
    b = pl.program_id(0)
    t = past_len
    for p in range(n_pages):
        tile = pool_ref[pt_ref[b, p]]
        for r in range(8):
            gath_ref[r, p:p + 1, :] = tile[r:r + 1, :]
    x = [gath_ref[r] for r in range(8)]
    q = q_ref[...]
    slope = slope_ref[:, 0:1]
    gates = gate_ref[...]
    lane = lax.broadcasted_iota(jnp.int32, (1, LANES), 1)
    lane_p = lax.broadcasted_iota(jnp.int32, (1, 2 * n_pages), 1)
    page_of = lane_p % n_pages
    ss, masks = [], []
    for pair in range(2):
        kmat = jnp.concatenate([x[2 * pair], x[2 * pair + 1]], axis=0).astype(BF16)
        blk = page_of * 4 + 2 * pair + lane_p // n_pages
        dist = t - (blk * CMP_BLOCK + (CMP_BLOCK - 1))
        mask = dist >= 0
        s = _dot_nt(q, kmat) - slope * dist.astype(F32)
        ss.append(jnp.where(mask, s, NEG_INF))
        masks.append(mask)
    mx = jnp.maximum(jnp.max(ss[0], axis=1, keepdims=True), jnp.max(ss[1], axis=1, keepdims=True))
    es = [jnp.where(masks[i], jnp.exp(ss[i] - mx), 0.0) for i in range(2)]
    den = jnp.maximum(jnp.sum(es[0], axis=1, keepdims=True) + jnp.sum(es[1], axis=1, keepdims=True), 1e-30)
    ps = [e / den for e in es]
    o_c = jnp.zeros((NSA_HEADS, LANES), F32)
    for pair in range(2):
        vmat = jnp.concatenate([x[4 + 2 * pair], x[5 + 2 * pair]], axis=0).astype(BF16)
        o_c = o_c + _dot(ps[pair].astype(BF16), vmat)
    n_sel_blk = lane_p // n_pages + 2 * page_of
    cur = t // SEL_BLOCK
    forced = (n_sel_blk == 0) | (n_sel_blk == cur) | (n_sel_blk == cur - 1)
    valid = n_sel_blk * SEL_BLOCK <= t
    idx_rows = []
    for g in range(NSA_KV_HEADS):
        imp_a = jnp.sum(ps[0][NSA_GROUP * g:NSA_GROUP * (g + 1)], axis=0, keepdims=True)
        imp_b = jnp.sum(ps[1][NSA_GROUP * g:NSA_GROUP * (g + 1)], axis=0, keepdims=True)
        ev = imp_a + pltpu.roll(imp_a, n_pages, 1)
        od = imp_b + pltpu.roll(imp_b, n_pages, 1)
        imp = jnp.where(lane_p < n_pages, ev, od)
        score = jnp.where(valid, imp + jnp.where(forced, FORCE_BONUS, 0.0), NEG_INF)
        chosen = jnp.where(lane == 0, cur, 0)
        for it in range(1, k_top):
            mxs = jnp.max(score, axis=1, keepdims=True)
            pick = jnp.min(jnp.where(score == mxs, n_sel_blk, 1 << 30), axis=1, keepdims=True)
            score = jnp.where(n_sel_blk == pick, BELOW_NEG_INF, score)
            chosen = jnp.where(lane == it, pick, chosen)
        idx_rows.append(chosen)
    rowi = lax.broadcasted_iota(jnp.int32, (8, LANES), 0)
    idx_ref[...] = jnp.where(rowi == 0, idx_rows[0], jnp.where(rowi == 1, idx_rows[1], 0))
    wl = win_ref.shape[-1]
    kw = win_ref[0].reshape(LANES, wl).astype(BF16)
    vw = win_ref[1].reshape(LANES, wl).astype(BF16)
    col = lax.broadcasted_iota(jnp.int32, (1, wl), 1)
    dist_w = wl - col
    mask_w = (t - dist_w >= 0) & (dist_w <= WINDOW)
    s_w = jnp.where(mask_w, _dot(q, kw) - slope * dist_w.astype(F32), NEG_INF)
    kvn = kvn_ref[...]
    qf = q.astype(F32)
    s_n = jnp.sum(qf * kvn[4:5, :].astype(BF16).astype(F32), axis=1, keepdims=True)
    mw = jnp.maximum(jnp.max(s_w, axis=1, keepdims=True), s_n)
    e_w = jnp.where(mask_w, jnp.exp(s_w - mw), 0.0)
    e_n = jnp.exp(s_n - mw)
    den_w = jnp.maximum(jnp.sum(e_w, axis=1, keepdims=True) + e_n, 1e-30)
    o_w = (_dot_nt((e_w / den_w).astype(BF16), vw)
           + (e_n / den_w).astype(BF16).astype(F32) * kvn[5:6, :].astype(BF16).astype(F32))
    part_ref[...] = gates[:, 0:1] * o_c + gates[:, 2:3] * o_w


def _nsa_dec_sel_kernel(pt_ref, sel_ref, blk_ref, q_ref, gate_ref, slope_ref, kvn_ref, part_ref, o_ref,
                        stage_ref, *, past_len, k_top):
    b = pl.program_id(0)
    j = pl.program_id(1)
    t = past_len
    stage_ref[j] = blk_ref[...]

    @pl.when(j == pl.num_programs(1) - 1)
    def _():
        q = q_ref[...]
        slope = slope_ref[:, 0:1]
        kvn = kvn_ref[...]
        page = blk_ref.shape[-1]
        halves = page // SEL_BLOCK
        col = lax.broadcasted_iota(jnp.int32, (1, page), 1)
        rowh = lax.broadcasted_iota(jnp.int32, (NSA_HEADS, 1), 0)
        lane = lax.broadcasted_iota(jnp.int32, (1, LANES), 1)
        s_new = jnp.sum(q.astype(F32) * kvn[2:3, :].astype(BF16).astype(F32), axis=1, keepdims=True)
        v_new = kvn[3:4, :].astype(BF16).astype(F32)
        o_groups = []
        for g in range(NSA_KV_HEADS):
            ss, masks, vts = [], [], []
            has_cur = False
            for i in range(k_top):
                jj = g * k_top + i
                n = sel_ref[b, jj]
                kt = stage_ref[jj, 0].reshape(LANES, page).astype(BF16)
                vts.append(stage_ref[jj, 1].reshape(LANES, page).astype(BF16))
                pos = (n // halves) * page + col
                mask = (col // SEL_BLOCK == n % halves) & (pos < t)
                ss.append(jnp.where(mask, _dot(q, kt) - slope * (t - pos).astype(F32), NEG_INF))
                masks.append(mask)
                has_cur = jnp.logical_or(has_cur, n * SEL_BLOCK + SEL_BLOCK > t)
            s_n = jnp.where(has_cur, s_new, NEG_INF)
            mx = s_n
            for s in ss:
                mx = jnp.maximum(mx, jnp.max(s, axis=1, keepdims=True))
            es = [jnp.where(masks[i], jnp.exp(ss[i] - mx), 0.0) for i in range(k_top)]
            e_n = jnp.where(has_cur, jnp.exp(s_n - mx), 0.0)
            den = e_n
            for e in es:
                den = den + jnp.sum(e, axis=1, keepdims=True)
            den = jnp.maximum(den, 1e-30)
            o = (e_n / den).astype(BF16).astype(F32) * v_new
            for i in range(k_top):
                o = o + _dot_nt((es[i] / den).astype(BF16), vts[i])
            o_groups.append(o)
        o_s = jnp.where(rowh // NSA_GROUP == 0, o_groups[0], o_groups[1])
        o = part_ref[...] + gate_ref[:, 1:2] * o_s
        in_half = (lane // HALF) == (rowh // NSA_GROUP)
        o_ref[...] = jnp.where(in_half, o, 0.0).astype(BF16)


def _nsa_decode(page_table, cmp_pool, cache_t, win_t, l, q_pad, gates, kv_new, slopes, past_len):
    bs, n_pages = page_table.shape
    n_pool, page = cache_t.shape[1], cache_t.shape[-1]
    k_top = min(TOP_N, past_len // SEL_BLOCK + 1)
    q3 = q_pad.reshape(bs, NSA_HEADS, LANES)
    g3 = jnp.pad(gates[:, :3 * NSA_HEADS].reshape(bs, NSA_HEADS, 3), ((0, 0), (0, 0), (0, LANES - 3)))
    kvn = kv_new.reshape(bs, 6, LANES)
    wl = win_t.shape[-1]
    part, idx = pl.pallas_call(
        functools.partial(_nsa_dec_cmp_kernel, n_pages=n_pages, past_len=past_len, k_top=k_top),
        grid_spec=pltpu.PrefetchScalarGridSpec(
            num_scalar_prefetch=1, grid=(bs,),
            in_specs=[pl.BlockSpec((None, n_pool, 8, LANES), lambda b, pt: (l, 0, 0, 0)),
                      pl.BlockSpec((None, NSA_HEADS, LANES), lambda b, pt: (b, 0, 0)),
                      pl.BlockSpec((None, NSA_HEADS, LANES), lambda b, pt: (b, 0, 0)),
                      pl.BlockSpec((NSA_HEADS, LANES), lambda b, pt: (0, 0)),
                      pl.BlockSpec((None, None, 2, NSA_KV_HEADS, NSA_HEAD_DIM, wl), lambda b, pt: (l, b, 0, 0, 0, 0)),
                      pl.BlockSpec((None, 6, LANES), lambda b, pt: (b, 0, 0))],
            out_specs=[pl.BlockSpec((None, NSA_HEADS, LANES), lambda b, pt: (b, 0, 0)),
                       pl.BlockSpec((None, 8, LANES), lambda b, pt: (b, 0, 0))],
            scratch_shapes=[pltpu.VMEM((8, n_pages, LANES), F32)]),
        out_shape=[jax.ShapeDtypeStruct((bs, NSA_HEADS, LANES), F32),
                   jax.ShapeDtypeStruct((bs, 8, LANES), jnp.int32)],
        compiler_params=_cparams(("arbitrary",)),
        name="nsa_dec_cmp",
    )(page_table, cmp_pool, q3, g3, slopes, win_t, kvn)
    sel = idx[:, :NSA_KV_HEADS, :k_top].reshape(bs, NSA_KV_HEADS * k_top)
    halves = page // SEL_BLOCK

    def blk_map(b, j, pt, sl):
        pg = pt[b, jnp.minimum(sl[b, j] // halves, n_pages - 1)]
        return (l, pg, 1, 0, 0, 0)

    out = pl.pallas_call(
        functools.partial(_nsa_dec_sel_kernel, past_len=past_len, k_top=k_top),
        grid_spec=pltpu.PrefetchScalarGridSpec(
            num_scalar_prefetch=2, grid=(bs, NSA_KV_HEADS * k_top),
            in_specs=[pl.BlockSpec((None, None, 2, NSA_KV_HEADS, NSA_HEAD_DIM, page), blk_map),
                      pl.BlockSpec((None, NSA_HEADS, LANES), lambda b, j, pt, sl: (b, 0, 0)),
                      pl.BlockSpec((None, NSA_HEADS, LANES), lambda b, j, pt, sl: (b, 0, 0)),
                      pl.BlockSpec((NSA_HEADS, LANES), lambda b, j, pt, sl: (0, 0)),
                      pl.BlockSpec((None, 6, LANES), lambda b, j, pt, sl: (b, 0, 0)),
                      pl.BlockSpec((None, NSA_HEADS, LANES), lambda b, j, pt, sl: (b, 0, 0))],
            out_specs=pl.BlockSpec((None, NSA_HEADS, LANES), lambda b, j, pt, sl: (b, 0, 0)),
            scratch_shapes=[pltpu.VMEM((NSA_KV_HEADS * k_top, 2, NSA_KV_HEADS, NSA_HEAD_DIM, page), F32)]),
        out_shape=jax.ShapeDtypeStruct((bs, NSA_HEADS, LANES), BF16),
        compiler_params=_cparams(("arbitrary", "arbitrary")),
        name="nsa_dec_sel",
    )(page_table, sel, cache_t, q3, g3, slopes, kvn, part)
    return out.reshape(bs, NSA_HEADS * LANES)


def _pool_tail(sums, cur, cnts, pw_ref, scale_ref, o_ref):
    for gi in range(len(POOL_WINDOWS)):
        sl = slice(gi * LANES, (gi + 1) * LANES)
        d = sums[gi] / cnts[gi] - cur[:, sl]
        y = _dot(d.astype(BF16), pw_ref[gi])
        o_ref[:, sl] = (y * scale_ref[:, sl]).astype(BF16)


def _pool_seq_kernel(p_ref, pprev_ref, hist_ref, pw_ref, scale_ref, o_ref, zz_ref, *, seq_tiles, tm, start_pos):
    i = pl.program_id(0)
    first = (i % seq_tiles) == 0
    cur = p_ref[...]
    zz_ref[0:16, :] = jnp.where(first, hist_ref[...], pprev_ref[...])
    zz_ref[16:16 + tm, :] = cur
    pos = start_pos + (i % seq_tiles) * tm + lax.broadcasted_iota(jnp.int32, (tm, 1), 0)
    sums, cnts = [], []
    for gi, w in enumerate(POOL_WINDOWS):
        sl = slice(gi * LANES, (gi + 1) * LANES)
        s = cur[:, sl]
        for k in range(1, w):
            s = s + zz_ref[16 - k:16 - k + tm, sl]
        sums.append(s)
        cnts.append(jnp.minimum(pos + 1, w).astype(F32))
    _pool_tail(sums, cur, cnts, pw_ref, scale_ref, o_ref)


def _pool_step_kernel(z_ref, pw_ref, scale_ref, o_ref, *, start_pos):
    z = z_ref[...]
    cur = z[:, 15, :]
    sums, cnts = [], []
    for gi, w in enumerate(POOL_WINDOWS):
        sl = slice(gi * LANES, (gi + 1) * LANES)
        sums.append(jnp.sum(z[:, 16 - w:16, sl], axis=1))
        cnts.append(float(min(start_pos + 1, w)))
    _pool_tail(sums, cur, cnts, pw_ref, scale_ref, o_ref)


def _pool_seq(p_pool, hist16, pool_w, pool_scale, t, tm, start_pos):
    n = p_pool.shape[0]
    tps = t // tm
    return pl.pallas_call(
        functools.partial(_pool_seq_kernel, seq_tiles=tps, tm=tm, start_pos=start_pos),
        grid=(n // tm,),
        in_specs=[pl.BlockSpec((tm, 512), lambda i: (i, 0)),
                  pl.BlockSpec((16, 512), lambda i: (jnp.maximum(i * (tm // 16) - 1, 0), 0)),
                  pl.BlockSpec((None, 16, 512), lambda i: (i // tps, 0, 0)),
                  pl.BlockSpec((4, LANES, LANES), lambda i: (0, 0, 0)),
                  pl.BlockSpec((1, 512), lambda i: (0, 0))],
        out_specs=pl.BlockSpec((tm, 512), lambda i: (i, 0)),
        out_shape=jax.ShapeDtypeStruct((n, 512), BF16),
        scratch_shapes=[pltpu.VMEM((tm + 16, 512), F32)],
        compiler_params=_cparams(("parallel",)),
        name="pool_seq",
    )(p_pool, p_pool, hist16, pool_w, pool_scale)


def _pool_step(z16, pool_w, pool_scale, start_pos):
    bs = z16.shape[0]
    return pl.pallas_call(
        functools.partial(_pool_step_kernel, start_pos=start_pos),
        grid=(1,),
        in_specs=[pl.BlockSpec((bs, 16, 512), lambda i: (0, 0, 0)),
                  pl.BlockSpec((4, LANES, LANES), lambda i: (0, 0, 0)),
                  pl.BlockSpec((1, 512), lambda i: (0, 0))],
        out_specs=pl.BlockSpec((bs, 512), lambda i: (0, 0)),
        out_shape=jax.ShapeDtypeStruct((bs, 512), BF16),
        compiler_params=_cparams(("arbitrary",)),
        name="pool_step",
    )(z16, pool_w, pool_scale)


def _merge_kernel(x_ref, sc1_ref, sh1_ref, g1_ref, sc2_ref, sh2_ref, orw_ref, onsa_ref, opool_ref,
                  wg_ref, wbr_ref, wbn_ref, wbp_ref, wout_ref, lng_ref, lnb_ref, wr_ref, br_ref,
                  x1_ref, u2_ref, route_ref, *, alpha):
    x = x_ref[...]
    d = x.shape[1]
    u = (x * (1.0 + sc1_ref[...]) + sh1_ref[...]).astype(BF16)
    mixed = jnp.zeros(x.shape, F32)
    for bi, (o_ref, wb_ref) in enumerate(((orw_ref, wbr_ref), (onsa_ref, wbn_ref), (opool_ref, wbp_ref))):
        gate = _sigmoid(_dot(u, wg_ref[:, bi * d:(bi + 1) * d]))
        mixed = mixed + gate * _dot(o_ref[...], wb_ref[...])
    m = _dot(mixed.astype(BF16), wout_ref[...])
    x1 = _layer_norm(alpha * x + (1.0 + g1_ref[...]) * m, lng_ref[...], lnb_ref[...])
    x1_ref[...] = x1
    u2 = x1 * (1.0 + sc2_ref[...]) + sh2_ref[...]
    u2b = u2.astype(BF16)
    u2_ref[...] = u2b
    lg = _dot(u2b, wr_ref[...]) + br_ref[...]
    lane = lax.broadcasted_iota(jnp.int32, (1, LANES), 1)
    is_g = lane < MOE_GROUPS
    mg = jnp.max(jnp.where(is_g, lg, BELOW_NEG_INF), axis=1, keepdims=True)
    gsel = jnp.min(jnp.where(is_g & (lg == mg), lane, LANES), axis=1, keepdims=True)
    wgrp = 1.0 / jnp.sum(jnp.where(is_g, jnp.exp(lg - mg), 0.0), axis=1, keepdims=True)
    e_lane = lane - ROUTE_LANE0
    in_grp = (e_lane >= 0) & (e_lane < MOE_GROUPS * EXPERTS_PER_GROUP) & ((e_lane // EXPERTS_PER_GROUP) == gsel)
    v1 = jnp.max(jnp.where(in_grp, lg, BELOW_NEG_INF), axis=1, keepdims=True)
    i1 = jnp.min(jnp.where(in_grp & (lg == v1), lane, LANES), axis=1, keepdims=True)
    rest = in_grp & (lane != i1)
    v2 = jnp.max(jnp.where(rest, lg, BELOW_NEG_INF), axis=1, keepdims=True)
    i2 = jnp.min(jnp.where(rest & (lg == v2), lane, LANES), axis=1, keepdims=True)
    e2 = jnp.exp(v2 - v1)
    w1 = wgrp / (1.0 + e2)
    w2 = wgrp * e2 / (1.0 + e2)
    for g in range(MOE_GROUPS):
        src = lane + (ROUTE_LANE0 + EXPERTS_PER_GROUP * g)
        route_ref[g] = jnp.where(lane < EXPERTS_PER_GROUP,
                                 jnp.where(src == i1, w1, jnp.where(src == i2, w2, 0.0)), 0.0)


def _merge(x, mods, o_rw, o_nsa, o_pool, wts, rows_per_group, tm, alpha):
    n, d = x.shape
    full = lambda a: pl.BlockSpec(a.shape, lambda i: (0,) * a.ndim)
    row = lambda c: pl.BlockSpec((tm, c), lambda i: (i, 0))
    return pl.pallas_call(
        functools.partial(_merge_kernel, alpha=alpha),
        grid=(n // tm,),
        in_specs=[row(d)] + [_mod_spec(m, rows_per_group, tm) for m in mods]
                 + [row(512), row(1024), row(512)] + [full(w) for w in wts],
        out_specs=[row(d), row(d), pl.BlockSpec((MOE_GROUPS, tm, LANES), lambda i: (0, i, 0))],
        out_shape=[jax.ShapeDtypeStruct((n, d), F32), jax.ShapeDtypeStruct((n, d), BF16),
                   jax.ShapeDtypeStruct((MOE_GROUPS, n, LANES), F32)],
        compiler_params=_cparams(("parallel",), 56),
        name="merge",
    )(x, *mods, o_rw, o_nsa, o_pool, *wts)


def _moe_kernel(x1_ref, u2_ref, route_ref, g2_ref, wg_ref, wu_ref, wd_ref, lng_ref, lnb_ref, o_ref, acc_ref, *, alpha):
    g = pl.program_id(1)

    @pl.when(g == 0)
    def _():
        acc_ref[...] = jnp.zeros(acc_ref.shape, F32)

    u = u2_ref[...]
    hg = _dot(u, wg_ref[...])
    h = (hg * _sigmoid(hg) * _dot(u, wu_ref[...])).astype(BF16)
    route = route_ref[...]
    y = acc_ref[...]
    for e in range(EXPERTS_PER_GROUP):
        w = route[:, e:e + 1]
        ye = _dot(h[:, e * EXPERT_HIDDEN:(e + 1) * EXPERT_HIDDEN], wd_ref[e * EXPERT_HIDDEN:(e + 1) * EXPERT_HIDDEN, :])
        y = y + jnp.where(w != 0.0, w * ye, 0.0)
    acc_ref[...] = y

    @pl.when(g == pl.num_programs(1) - 1)
    def _():
        h2 = alpha * x1_ref[...] + (1.0 + g2_ref[...]) * acc_ref[...]
        o_ref[...] = _layer_norm(h2, lng_ref[...], lnb_ref[...])


def _moe(x1, u2, route, g2, w_gate, w_up, w_down, ln_g, ln_b, rows_per_group, tm, alpha):
    n, d = x1.shape
    hid = EXPERTS_PER_GROUP * EXPERT_HIDDEN
    r = g2.shape[1]
    return pl.pallas_call(
        functools.partial(_moe_kernel, alpha=alpha),
        grid=(n // tm, MOE_GROUPS),
        in_specs=[pl.BlockSpec((tm, d), lambda i, g: (i, 0)),
                  pl.BlockSpec((tm, d), lambda i, g: (i, 0)),
                  pl.BlockSpec((None, tm, LANES), lambda i, g: (g, i, 0)),
                  pl.BlockSpec((None, r, d), lambda i, g: ((i * tm) // rows_per_group, 0, 0)),
                  pl.BlockSpec((None, d, hid), lambda i, g: (g, 0, 0)),
                  pl.BlockSpec((None, d, hid), lambda i, g: (g, 0, 0)),
                  pl.BlockSpec((None, hid, d), lambda i, g: (g, 0, 0)),
                  pl.BlockSpec((1, d), lambda i, g: (0, 0)),
                  pl.BlockSpec((1, d), lambda i, g: (0, 0))],
        out_specs=pl.BlockSpec((tm, d), lambda i, g: (i, 0)),
        out_shape=jax.ShapeDtypeStruct((n, d), F32),
        scratch_shapes=[pltpu.VMEM((tm, d), F32)],
        compiler_params=_cparams(("parallel", "arbitrary")),
        name="moe",
    )(x1, u2, route, g2, w_gate, w_up, w_down, ln_g, ln_b)


def _pad_heads(w, axis):
    shp = w.shape
    w = w.reshape(shp[:axis] + (NSA_KV_HEADS, NSA_GROUP, 1, NSA_HEAD_DIM) + shp[axis + 1:])
    sel = jnp.eye(NSA_KV_HEADS, dtype=w.dtype).reshape((1,) * axis + (NSA_KV_HEADS, 1, NSA_KV_HEADS, 1) + (1,) * (len(shp) - axis - 1))
    w = w * sel
    return w.reshape(shp[:axis] + (NSA_HEADS * LANES,) + shp[axis + 1:])


def _prep_weights(w_in, rw_w_up, rw_a_up, w_branch, moe_router_g, moe_bias_g, moe_router_e, moe_bias_e,
                  moe_w_gate, moe_w_up, moe_w_down, nsa_cmp_wk, nsa_cmp_wv):
    L, D, _ = w_in.shape
    w = {}
    w['rw'] = w_in[:, :, :RW_END].astype(BF16)
    wq = _pad_heads(w_in[:, :, RW_END:NSA_Q_END] * (NSA_HEAD_DIM ** -0.5), 2)
    wg = jnp.pad(w_in[:, :, NSA_KV_END:NSA_END], ((0, 0), (0, 0), (0, LANES - 3 * NSA_HEADS)))
    w['nsa'] = jnp.concatenate([wq, w_in[:, :, NSA_Q_END:NSA_KV_END], wg], axis=-1).astype(BF16)
    w['pool'] = w_in[:, :, NSA_END:POOL_END].astype(BF16)
    w['gate'] = w_in[:, :, POOL_END:].astype(BF16)
    z = jnp.zeros((L, HALF, RW_WIDTH), F32)
    w['wwa'] = jnp.concatenate([jnp.concatenate([rw_w_up, z], axis=2),
                                jnp.concatenate([z, rw_a_up], axis=2)], axis=1).astype(BF16)
    w['b_rw'] = w_branch[:, 0].astype(BF16)
    w['b_nsa'] = _pad_heads(w_branch[:, 1], 1).astype(BF16)
    w['b_pool'] = w_branch[:, 2].astype(BF16)
    pad_r = LANES - MOE_GROUPS - MOE_GROUPS * EXPERTS_PER_GROUP
    w['router'] = jnp.pad(jnp.concatenate([moe_router_g, moe_router_e], axis=2), ((0, 0), (0, 0), (0, pad_r))).astype(BF16)
    w['router_b'] = jnp.pad(jnp.concatenate([moe_bias_g, moe_bias_e], axis=1), ((0, 0), (0, pad_r)))[:, None, :]
    hid = EXPERTS_PER_GROUP * EXPERT_HIDDEN
    grp = lambda a: jnp.transpose(a.reshape(L, MOE_GROUPS, EXPERTS_PER_GROUP, D, EXPERT_HIDDEN),
                                  (0, 1, 3, 2, 4)).reshape(L, MOE_GROUPS, D, hid).astype(BF16)
    w['moe_gate'] = grp(moe_w_gate)
    w['moe_up'] = grp(moe_w_up)
    w['moe_down'] = moe_w_down.reshape(L, MOE_GROUPS, hid, D).astype(BF16)
    w['cmp_wk'] = jnp.tile(nsa_cmp_wk, (1, 1, NSA_KV_HEADS))
    w['cmp_wv'] = jnp.tile(nsa_cmp_wv, (1, 1, NSA_KV_HEADS))
    return w


def kernel(x_prompt, x_sample, cache_nsa_kv, cache_win_kv, state_rwkv, state_rwkv_shift, state_pool, page_table, c_prompt, c_sample, ada_w, ada_b, w_in, rw_mu, rw_w0, rw_w_up, rw_a0, rw_a_up, rw_g_up, rw_k_k, rw_k_a, rw_r_k, rw_gn_g, rw_gn_b, nsa_cmp_wk, nsa_cmp_wv, pool_w, pool_scale, w_branch, w_out, ln1_g, ln1_b, moe_router_g, moe_bias_g, moe_router_e, moe_bias_e, moe_w_gate, moe_w_up, moe_w_down, ln2_g, ln2_b):
    bp, t, d = x_prompt.shape
    bs, ts, _ = x_sample.shape
    L = ada_w.shape[0]
    n_pool, page = cache_nsa_kv.shape[1], cache_nsa_kv.shape[2]
    n_pages = page_table.shape[1]
    past_len = n_pages * page
    assert ts == 1 and d == D_MODEL
    assert t % 256 == 0 and t // CMP_BLOCK <= LANES and past_len % SEL_BLOCK == 0
    assert cache_win_kv.shape[2] == WINDOW and bs % 8 == 0 and 2 * n_pages == LANES
    alpha = (2 * L) ** 0.25
    tm = 256

    w = _prep_weights(w_in, rw_w_up, rw_a_up, w_branch, moe_router_g, moe_bias_g, moe_router_e, moe_bias_e,
                      moe_w_gate, moe_w_up, moe_w_down, nsa_cmp_wk, nsa_cmp_wv)
    w_out_b = w_out.astype(BF16)
    pool_w_b = pool_w.astype(BF16)
    gup_b = rw_g_up.astype(BF16)

    nb = bp + bs
    nb_pad = -(-nb // 8) * 8
    c_all = jnp.pad(jnp.concatenate([c_prompt, c_sample], axis=0), ((0, nb_pad - nb), (0, 0)))
    mods = _ada_mod(c_all, ada_w, ada_b)

    cache_t = jnp.transpose(cache_nsa_kv, (0, 1, 3, 4, 5, 2)).astype(F32)
    win_cache = jnp.transpose(cache_win_kv, (0, 1, 3, 4, 5, 2)).astype(F32)
    pb = 8 if n_pool % 8 == 0 else 1
    cmp_w_t = jnp.tile(jnp.transpose(jnp.stack([nsa_cmp_wk, nsa_cmp_wv], axis=1), (0, 1, 3, 2)),
                       (1, 1, NSA_KV_HEADS, page // CMP_BLOCK))
    cmp_pool = _compress_pool(cache_t, cmp_w_t, pb)
    slopes = jnp.broadcast_to(jnp.asarray(NSA_SLOPES, F32)[:, None], (NSA_HEADS, LANES))

    xp = x_prompt.reshape(bp * t, d)
    xs = x_sample.reshape(bs, d)
    zeros_prev = jnp.zeros((bp, RW_COLS), F32)
    zeros_hist = jnp.zeros((bp, 16, POOL_WIDTH), F32)
    outs = {k: [] for k in ('nsa_p', 'nsa_s', 'win_p', 'win_s', 'rw_p', 'rw_s', 'sh_p', 'sh_s', 'pool_p', 'pool_s')}

    for l in range(L):
        vec = lambda a: a[l].reshape(1, -1)
        rw_wts = (vec(rw_mu), vec(rw_w0), vec(rw_a0), w['wwa'][l], gup_b[l], vec(rw_k_k), vec(rw_k_a), vec(rw_r_k))
        merge_wts = (w['gate'][l], w['b_rw'][l], w['b_nsa'][l], w['b_pool'][l], w_out_b[l], vec(ln1_g), vec(ln1_b),
                     w['router'][l], w['router_b'][l])
        mod_p = [m[:, None, :] for m in jnp.split(mods[l, :bp], 6, axis=-1)]
        mod_s = [m[None] for m in jnp.split(mods[l, bp:bp + bs], 6, axis=-1)]

        p_rw = _proj(xp, mod_p[1], mod_p[0], w['rw'][l], t, tm)
        q_pad, kv, gates, kvt, cmp = _proj_nsa(xp, mod_p[1], mod_p[0], w['nsa'][l], w['cmp_wk'][l], w['cmp_wv'][l], bp, t, tm)
        p_pool = _proj(xp, mod_p[1], mod_p[0], w['pool'][l], t, tm)
        seqs = _rwkv_prep(p_rw, zeros_prev, rw_wts, t, tm)
        o_rw, s_pairs = _rwkv_chunk(seqs, vec(rw_gn_g), vec(rw_gn_b), bp, t)
        o_nsa = _nsa_prompt(q_pad, kvt, _permute_cmp(cmp, bp, t), gates, bp, t)
        o_pool = _pool_seq(p_pool, zeros_hist, pool_w_b[l], vec(pool_scale), t, tm, 0)
        x1, u2, route = _merge(xp, (mod_p[1], mod_p[0], mod_p[2], mod_p[4], mod_p[3]), o_rw, o_nsa, o_pool,
                               merge_wts, t, tm, alpha)
        xp = _moe(x1, u2, route, mod_p[5], w['moe_gate'][l], w['moe_up'][l], w['moe_down'][l],
                  vec(ln2_g), vec(ln2_b), t, 2 * tm, alpha)
        kv3 = kv.reshape(bp, t, 6, NSA_KV_HEADS, NSA_HEAD_DIM)
        outs['nsa_p'].append(kv3[:, :, :4])
        outs['win_p'].append(kv3[:, t - min(WINDOW, t):, 4:])
        outs['rw_p'].append(_state_from_pairs(s_pairs))
        outs['sh_p'].append(p_rw.reshape(bp, t, RW_COLS)[:, -1])
        outs['pool_p'].append(p_pool.reshape(bp, t, POOL_WIDTH)[:, t - POOL_HIST:])

        p_rw = _proj(xs, mod_s[1], mod_s[0], w['rw'][l], bs, bs)
        q_pad, kv, gates = _proj_nsa(xs, mod_s[1], mod_s[0], w['nsa'][l], w['cmp_wk'][l], w['cmp_wv'][l], bs, 1, bs)
        p_pool = _proj(xs, mod_s[1], mod_s[0], w['pool'][l], bs, bs)
        seqs = _rwkv_prep(p_rw, state_rwkv_shift[l], rw_wts, 1, bs)
        o_rw, s_pairs = _rwkv_step(seqs, _state_to_pairs(state_rwkv[l]), vec(rw_gn_g), vec(rw_gn_b))
        o_nsa = _nsa_decode(page_table, cmp_pool, cache_t, win_cache, l, q_pad, gates, kv, slopes, past_len)
        z16 = jnp.concatenate([state_pool[l].astype(F32), p_pool[:, None, :]], axis=1)
        o_pool = _pool_step(z16, pool_w_b[l], vec(pool_scale), past_len)
        x1, u2, route = _merge(xs, (mod_s[1], mod_s[0], mod_s[2], mod_s[4], mod_s[3]), o_rw, o_nsa, o_pool,
                               merge_wts, bs, bs, alpha)
        xs = _moe(x1, u2, route, mod_s[5], w['moe_gate'][l], w['moe_up'][l], w['moe_down'][l],
                  vec(ln2_g), vec(ln2_b), bs, bs, alpha)
        kv3 = kv.reshape(bs, 1, 6, NSA_KV_HEADS, NSA_HEAD_DIM)
        outs['nsa_s'].append(kv3[:, :, :4])
        outs['win_s'].append(jnp.concatenate([cache_win_kv[l].astype(F32), kv3[:, :, 4:]], axis=1)[:, -WINDOW:])
        outs['rw_s'].append(_state_from_pairs(s_pairs))
        outs['sh_s'].append(p_rw)
        outs['pool_s'].append(z16[:, 1:])

    st = lambda k: jnp.stack(outs[k])
    return (xp.reshape(bp, t, d), xs.reshape(bs, 1, d), st('nsa_p'), st('nsa_s'), st('win_p'), st('win_s'),
            st('rw_p'), st('rw_s'), st('sh_p'), st('sh_s'), st('pool_p'), st('pool_s'))
```

```python
import functools

import jax
import jax.numpy as jnp
from jax import lax
from jax.experimental import pallas as pl
from jax.experimental.pallas import tpu as pltpu

F32 = jnp.float32
BF16 = jnp.bfloat16

D_MODEL = 1024
RW_HEADS = 8
RW_HEAD_DIM = 64
RW_WIDTH = 512
RW_COLS = 1792
RW_GN_EPS = 64e-5
RW_CHUNK = 64

NSA_HEADS = 8
NSA_KV_HEADS = 2
NSA_HEAD_DIM = 64
NSA_GROUP = 4
CMP_BLOCK = 32
SEL_BLOCK = 64
TOP_N = 8
WINDOW = 512
NSA_TQ = 128
NSA_TK = 512
NEG_INF = -1e30
BELOW_NEG_INF = -3e38
FORCE_BONUS = 1e9
NSA_SLOPES = tuple(2.0 ** (-8.0 * (h + 1) / NSA_HEADS) for h in range(NSA_HEADS))

POOL_WINDOWS = (2, 4, 8, 16)
POOL_HIST = 15
POOL_WIDTH = 512

RW_END = 1792
NSA_Q_END = RW_END + 512
NSA_KV_END = NSA_Q_END + 768
NSA_END = NSA_KV_END + 24
POOL_END = NSA_END + 512

MOE_GROUPS = 4
EXPERTS_PER_GROUP = 4
EXPERT_HIDDEN = 256
ROUTE_LANE0 = 4

LN_EPS = 1e-5
LANES = 128
HALF = 64


def _cparams(sem, vmem_mb=48):
    return pltpu.CompilerParams(dimension_semantics=sem, vmem_limit_bytes=vmem_mb * 1024 * 1024)


def _dot(a, b):
    return jnp.dot(a, b, preferred_element_type=F32)


def _dot_nt(a, b):
    return lax.dot_general(a, b, (((1,), (1,)), ((), ())), preferred_element_type=F32)


def _dot_tn(a, b):
    return lax.dot_general(a, b, (((0,), (0,)), ((), ())), preferred_element_type=F32)


def _sigmoid(x):
    return 1.0 / (1.0 + jnp.exp(-x))


def _softplus(x):
    return jnp.maximum(x, 0.0) + jnp.log(1.0 + jnp.exp(-jnp.abs(x)))


def _layer_norm(h, g, b):
    mu = jnp.mean(h, axis=-1, keepdims=True)
    d = h - mu
    var = jnp.mean(d * d, axis=-1, keepdims=True)
    return d * lax.rsqrt(var + LN_EPS) * g + b


def _half_sum(x, lo):
    s_lo = jnp.sum(jnp.where(lo, x, 0.0), axis=1, keepdims=True)
    s_hi = jnp.sum(jnp.where(lo, 0.0, x), axis=1, keepdims=True)
    return jnp.where(lo, s_lo, s_hi)


def _ada_kernel(c_ref, w_ref, b_ref, o_ref):
    c = c_ref[...]
    s = (c * _sigmoid(c)).astype(BF16)
    o_ref[...] = _dot(s, w_ref[...].astype(BF16)) + b_ref[...]


def _ada_mod(c_all, ada_w, ada_b):
    L, D, D6 = ada_w.shape
    nb = c_all.shape[0]
    return pl.pallas_call(
        _ada_kernel,
        grid=(L, D6 // D),
        in_specs=[pl.BlockSpec((nb, D), lambda l, j: (0, 0)),
                  pl.BlockSpec((None, D, D), lambda l, j: (l, 0, j)),
                  pl.BlockSpec((None, 1, D), lambda l, j: (l, 0, j))],
        out_specs=pl.BlockSpec((None, nb, D), lambda l, j: (l, 0, j)),
        out_shape=jax.ShapeDtypeStruct((L, nb, D6), F32),
        compiler_params=_cparams(("parallel", "parallel")),
        name="ada_mod",
    )(c_all, ada_w, ada_b.reshape(L, 1, D6))


def _mod_spec(mod, rows_per_group, tm):
    r = mod.shape[1]
    return pl.BlockSpec((None, r, mod.shape[2]), lambda i: ((i * tm) // rows_per_group, 0, 0))


def _proj_kernel(x_ref, sc_ref, sh_ref, w_ref, o_ref):
    u = (x_ref[...] * (1.0 + sc_ref[...]) + sh_ref[...]).astype(BF16)
    o_ref[...] = _dot(u, w_ref[...])


def _proj(x, sc, sh, w, rows_per_group, tm):
    n, d = x.shape
    nc = w.shape[1]
    return pl.pallas_call(
        _proj_kernel,
        grid=(n // tm,),
        in_specs=[pl.BlockSpec((tm, d), lambda i: (i, 0)),
                  _mod_spec(sc, rows_per_group, tm), _mod_spec(sh, rows_per_group, tm),
                  pl.BlockSpec((d, nc), lambda i: (0, 0))],
        out_specs=pl.BlockSpec((tm, nc), lambda i: (i, 0)),
        out_shape=jax.ShapeDtypeStruct((n, nc), F32),
        compiler_params=_cparams(("parallel",)),
        name="proj",
    )(x, sc, sh, w)


def _proj_nsa_kernel(x_ref, sc_ref, sh_ref, w_ref, wk_ref, wv_ref, q_ref, kv_ref, gate_ref, kvt_ref=None, cmp_ref=None):
    u = (x_ref[...] * (1.0 + sc_ref[...]) + sh_ref[...]).astype(BF16)
    res = _dot(u, w_ref[...])
    q_ref[...] = res[:, :1024].astype(BF16)
    kv = res[:, 1024:1792]
    kv_ref[...] = kv
    gate_ref[...] = _sigmoid(res[:, 1792:1920])
    if kvt_ref is None:
        return
    for j in range(6):
        kvt_ref[j] = kv[:, j * LANES:(j + 1) * LANES].astype(BF16)
    nb = kv.shape[0] // CMP_BLOCK
    kc = jnp.sum(kv[:, 0:LANES].reshape(nb, CMP_BLOCK, LANES) * wk_ref[...][None], axis=1)
    vc = jnp.sum(kv[:, LANES:2 * LANES].reshape(nb, CMP_BLOCK, LANES) * wv_ref[...][None], axis=1)
    cmp_ref[:, 0:LANES] = kc
    cmp_ref[:, LANES:2 * LANES] = vc


def _proj_nsa(x, sc, sh, w, wk, wv, n_seq, t, tm):
    n, d = x.shape
    out_specs = [pl.BlockSpec((tm, 1024), lambda i: (i, 0)),
                 pl.BlockSpec((tm, 768), lambda i: (i, 0)),
                 pl.BlockSpec((tm, LANES), lambda i: (i, 0))]
    out_shape = [jax.ShapeDtypeStruct((n, 1024), BF16),
                 jax.ShapeDtypeStruct((n, 768), F32),
                 jax.ShapeDtypeStruct((n, LANES), F32)]
    if t > 1:
        tps = t // tm
        nb = tm // CMP_BLOCK
        out_specs += [pl.BlockSpec((None, 6, tm, LANES), lambda i: (i // tps, 0, i % tps, 0)),
                      pl.BlockSpec((nb, 2 * LANES), lambda i: (i, 0))]
        out_shape += [jax.ShapeDtypeStruct((n_seq, 6, t, LANES), BF16),
                      jax.ShapeDtypeStruct((n // CMP_BLOCK, 2 * LANES), F32)]
    rows_per_group = t if sc.shape[1] == 1 else n
    return pl.pallas_call(
        _proj_nsa_kernel,
        grid=(n // tm,),
        in_specs=[pl.BlockSpec((tm, d), lambda i: (i, 0)),
                  _mod_spec(sc, rows_per_group, tm), _mod_spec(sh, rows_per_group, tm),
                  pl.BlockSpec((d, 1920), lambda i: (0, 0)),
                  pl.BlockSpec((CMP_BLOCK, LANES), lambda i: (0, 0)),
                  pl.BlockSpec((CMP_BLOCK, LANES), lambda i: (0, 0))],
        out_specs=out_specs,
        out_shape=out_shape,
        compiler_params=_cparams(("parallel",)),
        name="proj_nsa",
    )(x, sc, sh, w, wk, wv)


def _rwkv_prep_body(p, prev, mu_ref, w0_ref, a0_ref, wwa_ref, gup_ref, kkw_ref, kaw_ref, rk_ref, outs):
    r_ref, lw_ref, k_ref, v_ref, kk_ref, kka_ref, g_ref, bonus_ref = outs
    xs = p + (prev - p) * mu_ref[...]
    r = xs[:, 0:512]
    xk = xs[:, 512:1024]
    v = xs[:, 1024:1536]
    t12 = xs[:, 1536:1664]
    gd = xs[:, 1664:1792]
    lane = lax.broadcasted_iota(jnp.int32, (1, LANES), 1)
    lo = lane < HALF
    z = jnp.where(lo, jnp.tanh(t12), t12).astype(BF16)
    dwa = _dot(z, wwa_ref[...])
    w_log = -_softplus(-(w0_ref[...] + dwa[:, :512])) - 0.5
    a = _sigmoid(a0_ref[...] + dwa[:, 512:])
    g_ref[...] = _dot(_sigmoid(gd).astype(BF16), gup_ref[...])
    kmod = xk * (1.0 + (a - 1.0) * kaw_ref[...])
    kkr = xk * kkw_ref[...]
    rkr = r * kmod * rk_ref[...]
    for m in range(4):
        sl = slice(m * LANES, (m + 1) * LANES)
        x = kkr[:, sl]
        nrm = jnp.sqrt(_half_sum(x * x, lo))
        kk = x / jnp.maximum(nrm, 1e-12)
        kk_ref[:, sl] = kk
        kka_ref[:, sl] = kk * a[:, sl]
        bonus_ref[:, sl] = _half_sum(rkr[:, sl], lo) * v[:, sl]
    r_ref[...] = r
    lw_ref[...] = -jnp.exp(w_log)
    k_ref[...] = kmod
    v_ref[...] = v


def _rwkv_prep_seq_kernel(p_ref, pprev_ref, prow_ref, *rest, seq_tiles):
    i = pl.program_id(0)
    p = p_ref[...]
    first = (i % seq_tiles) == 0
    prev_row = jnp.where(first, prow_ref[...], pprev_ref[7:8, :])
    rolled = pltpu.roll(p, 1, 0)
    rowid = lax.broadcasted_iota(jnp.int32, (p.shape[0], 1), 0)
    prev = jnp.where(rowid == 0, prev_row, rolled)
    _rwkv_prep_body(p, prev, *rest[:8], rest[8:])


def _rwkv_prep_step_kernel(p_ref, prev_ref, *rest):
    _rwkv_prep_body(p_ref[...], prev_ref[...], *rest[:8], rest[8:])


def _rwkv_prep(p_rw, prev_rows, wts, t, tm):
    n = p_rw.shape[0]
    vec = lambda c: pl.BlockSpec((1, c), lambda i: (0, 0))
    w_specs = [vec(RW_COLS), vec(512), vec(512), pl.BlockSpec((LANES, 1024), lambda i: (0, 0)),
               pl.BlockSpec((LANES, 512), lambda i: (0, 0)), vec(512), vec(512), vec(512)]
    out_specs = [pl.BlockSpec((tm, 512), lambda i: (i, 0))] * 8
    out_shape = [jax.ShapeDtypeStruct((n, 512), F32)] * 8
    if t == 1:
        kern = _rwkv_prep_step_kernel
        in_specs = [pl.BlockSpec((tm, RW_COLS), lambda i: (i, 0)), pl.BlockSpec((tm, RW_COLS), lambda i: (i, 0))]
        args = (p_rw, prev_rows)
    else:
        tps = t // tm
        kern = functools.partial(_rwkv_prep_seq_kernel, seq_tiles=tps)
        in_specs = [pl.BlockSpec((tm, RW_COLS), lambda i: (i, 0)),
                    pl.BlockSpec((8, RW_COLS), lambda i: (jnp.maximum(i * (tm // 8) - 1, 0), 0)),
                    pl.BlockSpec((None, 1, RW_COLS), lambda i: (i // tps, 0, 0))]
        args = (p_rw, p_rw, prev_rows[:, None, :])
    return pl.pallas_call(
        kern, grid=(n // tm,), in_specs=in_specs + w_specs, out_specs=out_specs, out_shape=out_shape,
        compiler_params=_cparams(("parallel",)), name="rwkv_prep",
    )(*args, *wts)


def _split3(x):
    x1 = x.astype(BF16)
    r1 = x - x1.astype(F32)
    x2 = r1.astype(BF16)
    x3 = (r1 - x2.astype(F32)).astype(BF16)
    return x1, x2, x3


def _rwkv_chunk_kernel(r_ref, lw_ref, k_ref, v_ref, kk_ref, kka_ref, g_ref, bonus_ref, gng_ref, gnb_ref,
                       o_ref, sout_ref, s_ref):
    c = pl.program_id(1)
    C = RW_CHUNK

    @pl.when(c == 0)
    def _():
        s_ref[...] = jnp.zeros(s_ref.shape, F32)

    ns = lw_ref.shape[0]
    flat = lambda ref: ref[...].reshape(ns * C, ref.shape[-1])
    lw = flat(lw_ref)
    row = lax.broadcasted_iota(jnp.int32, (ns * C, ns * C), 0)
    col = lax.broadcasted_iota(jnp.int32, (ns * C, ns * C), 1)
    tri = ((row >= col) & (row // C == col // C)).astype(BF16)
    l1, l2, l3 = _split3(lw)
    cl = _dot(tri, l1) + _dot(tri, l2) + _dot(tri, l3)
    cl_last = jnp.concatenate([jnp.broadcast_to(cl[(q + 1) * C - 1:(q + 1) * C, :], (C, cl.shape[1]))
                               for q in range(ns)], axis=0)
    g_in = jnp.exp(cl)
    g_ex = jnp.exp(cl - lw)
    g_inv = jnp.exp(-cl)
    g_rem = jnp.exp(cl_last - cl)
    gc = jnp.exp(cl_last)
    k = flat(k_ref)
    kka = flat(kka_ref)
    qk_all = flat(kk_ref) * g_ex
    r_all = flat(r_ref) * g_in
    kt_all = k * g_inv
    at_all = kka * g_inv
    kd_all = k * g_rem
    ad_all = kka * g_rem
    v_all = flat(v_ref)

    lane = lax.broadcasted_iota(jnp.int32, (1, LANES), 1)
    lo = lane < HALF

    def st(x):
        return jnp.concatenate([jnp.where(lo, x, 0.0), jnp.where(lo, 0.0, x)], axis=0)

    r2 = lax.broadcasted_iota(jnp.int32, (4 * C, 4 * C), 0)
    c2 = lax.broadcasted_iota(jnp.int32, (4 * C, 4 * C), 1)
    rt = r2 % C
    ct = c2 % C
    tmask = (ct < rt) | ((r2 >= 2 * C) & (ct == rt))
    ri = lax.broadcasted_iota(jnp.int32, (2 * C, 2 * C), 0)
    ci = lax.broadcasted_iota(jnp.int32, (2 * C, 2 * C), 1)
    eye = (ri == ci).astype(F32)

    units = [(q, m) for q in range(ns) for m in range(4)]
    pairs = range(len(units))
    cut = lambda x, u: x[units[u][0] * C:(units[u][0] + 1) * C, units[u][1] * LANES:(units[u][1] + 1) * LANES]
    sls = [slice(m * LANES, (m + 1) * LANES) for _, m in units]
    bf = lambda x: x.astype(BF16)
    qk_st = [st(cut(qk_all, u)) for u in pairs]
    r_st = [st(cut(r_all, u)) for u in pairs]
    v_st = [st(cut(v_all, u)) for u in pairs]
    ad_st = [st(cut(ad_all, u)) for u in pairs]
    kd_st = [st(cut(kd_all, u)) for u in pairs]
    xm = [jnp.where(tmask, _dot_nt(bf(jnp.concatenate([qk_st[m], r_st[m]], axis=0)),
                                   bf(jnp.concatenate([st(cut(at_all, m)), st(cut(kt_all, m))], axis=0))), 0.0)
          for m in pairs]
    av = [_dot(bf(xm[m][:, 2 * C:]), bf(v_st[m])) for m in pairs]
    p = [xm[m][:2 * C, :2 * C] for m in pairs]
    tm_ = [eye - p[m] for m in pairs]
    for _ in range(5):
        p = [_dot(bf(p[m]), bf(p[m])) for m in pairs]
        tm_ = [_dot(bf(tm_[m]), bf(eye + p[m])) for m in pairs]
    wu = [_dot(bf(tm_[m]), bf(jnp.concatenate([qk_st[m], av[m][:2 * C]], axis=1))) for m in pairs]
    ry = [_dot(bf(xm[m][2 * C:, :2 * C]), bf(wu[m])) for m in pairs]
    rq = [r_st[m] - ry[m][:, :LANES] for m in pairs]
    y0 = [av[m][2 * C:] - ry[m][:, LANES:] for m in pairs]
    gt = [eye * cut(gc, m)[0:1] - _dot_tn(bf(wu[m][:, :LANES]), bf(ad_st[m])) for m in pairs]
    ht = [_dot_tn(bf(jnp.concatenate([v_st[m], wu[m][:, LANES:]], axis=0)),
                  bf(jnp.concatenate([kd_st[m], -ad_st[m]], axis=0))) for m in pairs]
    sb = [bf(s_ref[q, m]) for q, m in units]
    y_st = [_dot_nt(bf(rq[m]), sb[m]) + y0[m] for m in pairs]
    s_new = [_dot(sb[m], bf(gt[m])) + ht[m] for m in pairs]
    for u, (q, m) in enumerate(units):
        sl = sls[u]
        s_ref[q, m] = s_new[u]
        sout_ref[q, m] = s_new[u][:C] + s_new[u][C:]
        y = y_st[u][:C] + y_st[u][C:]
        o_ref[q, :, sl] = _rwkv_out(y, lo, gng_ref[:, sl], gnb_ref[:, sl], bonus_ref[q, :, sl], g_ref[q, :, sl])


def _rwkv_out(y, lo, gn_g, gn_b, bonus, g):
    mu = _half_sum(y, lo) * (1.0 / RW_HEAD_DIM)
    d = y - mu
    var = _half_sum(d * d, lo) * (1.0 / RW_HEAD_DIM)
    yn = d * lax.rsqrt(var + RW_GN_EPS) * gn_g + gn_b
    return ((yn + bonus) * g).astype(BF16)


def _rwkv_chunk(seqs, gn_g, gn_b, n_seq, t):
    C = RW_CHUNK
    nch = t // C
    ns = 4 if n_seq % 4 == 0 else (2 if n_seq % 2 == 0 else 1)
    row_spec = pl.BlockSpec((ns, C, 512), lambda b, c: (b, c, 0))
    vec = pl.BlockSpec((1, 512), lambda b, c: (0, 0))
    o, s = pl.pallas_call(
        _rwkv_chunk_kernel,
        grid=(n_seq // ns, nch),
        in_specs=[row_spec] * 8 + [vec, vec],
        out_specs=[row_spec, pl.BlockSpec((ns, 4, RW_HEAD_DIM, LANES), lambda b, c: (b, 0, 0, 0))],
        out_shape=[jax.ShapeDtypeStruct((n_seq, t, 512), BF16),
                   jax.ShapeDtypeStruct((n_seq, 4, RW_HEAD_DIM, LANES), F32)],
        scratch_shapes=[pltpu.VMEM((ns, 4, LANES, LANES), F32)],
        compiler_params=_cparams(("parallel", "arbitrary")),
        name="rwkv_chunk",
    )(*[a.reshape(n_seq, t, 512) for a in seqs], gn_g, gn_b)
    return o.reshape(n_seq * t, 512), s


def _rwkv_step_kernel(r_ref, lw_ref, k_ref, v_ref, kk_ref, kka_ref, g_ref, bonus_ref, s_ref, gng_ref, gnb_ref,
                      o_ref, sout_ref):
    lane = lax.broadcasted_iota(jnp.int32, (1, LANES), 1)
    lo = lane < HALF
    rowi = lax.broadcasted_iota(jnp.int32, (RW_HEAD_DIM, LANES), 0)
    diag = (rowi == lax.broadcasted_iota(jnp.int32, (RW_HEAD_DIM, LANES), 1) % HALF).astype(F32)
    for m in range(4):
        sl = slice(m * LANES, (m + 1) * LANES)
        s = s_ref[m]
        sk = _half_sum(s * kk_ref[:, sl], lo)
        v_col = _half_sum(diag * v_ref[:, sl], lo)
        s_new = s * jnp.exp(lw_ref[:, sl]) - sk * kka_ref[:, sl] + v_col * k_ref[:, sl]
        sout_ref[m] = s_new
        y_col = _half_sum(s_new * r_ref[:, sl], lo)
        y = jnp.sum(diag * y_col, axis=0, keepdims=True)
        o_ref[:, sl] = _rwkv_out(y, lo, gng_ref[:, sl], gnb_ref[:, sl], bonus_ref[:, sl], g_ref[:, sl])


def _rwkv_step(seqs, s_pairs, gn_g, gn_b):
    bs = s_pairs.shape[0]
    row_spec = pl.BlockSpec((None, 1, 512), lambda b: (b, 0, 0))
    st_spec = pl.BlockSpec((None, 4, RW_HEAD_DIM, LANES), lambda b: (b, 0, 0, 0))
    vec = pl.BlockSpec((1, 512), lambda b: (0, 0))
    o, s = pl.pallas_call(
        _rwkv_step_kernel,
        grid=(bs,),
        in_specs=[row_spec] * 8 + [st_spec, vec, vec],
        out_specs=[row_spec, st_spec],
        out_shape=[jax.ShapeDtypeStruct((bs, 1, 512), BF16),
                   jax.ShapeDtypeStruct((bs, 4, RW_HEAD_DIM, LANES), F32)],
        compiler_params=_cparams(("parallel",)),
        name="rwkv_step",
    )(*[a[:, None, :] for a in seqs], s_pairs, gn_g, gn_b)
    return o[:, 0], s


def _state_to_pairs(s):
    b = s.shape[0]
    return jnp.transpose(s.reshape(b, 4, 2, 64, 64).astype(F32), (0, 1, 3, 2, 4)).reshape(b, 4, 64, LANES)


def _state_from_pairs(sp):
    b = sp.shape[0]
    return jnp.transpose(sp.reshape(b, 4, 64, 2, 64), (0, 1, 3, 2, 4)).reshape(b, 8, 64, 64)


def _nsa_prompt_kernel(q_ref, kvt_ref, cmp_ref, gate_ref, o_ref, m_ref, acc_ref, *, k_top, tk):
    qi = pl.program_id(1)
    tq = NSA_TQ
    q0 = qi * tq
    lane = lax.broadcasted_iota(jnp.int32, (1, LANES), 1)
    lo = lane < HALF
    tok = q0 + lax.broadcasted_iota(jnp.int32, (tq, 1), 0)
    cmpv = cmp_ref[...]
    kc = cmpv[:, :LANES].astype(BF16)
    vc = cmpv[:, LANES:].astype(BF16)
    gates = gate_ref[...]
    blk_c = 2 * (lane % HALF) + lane // HALF
    dist_c = tok - (blk_c * CMP_BLOCK + (CMP_BLOCK - 1))
    mask_c = dist_c >= 0
    dist_cf = dist_c.astype(F32)
    cur = tok // SEL_BLOCK
    valid = (lane * SEL_BLOCK <= tok) & lo
    forced = (lane == 0) | (lane == cur) | (lane == cur - 1)
    in_grp = (lo, jnp.logical_not(lo))
    den_lane = (HALF, 0)
    hrows = lambda h: slice(h * tq, (h + 1) * tq)
    grows = lambda g: slice(g * NSA_GROUP * tq, (g + 1) * NSA_GROUP * tq)
    q_all = jnp.concatenate([q_ref[:, h * LANES:(h + 1) * LANES] for h in range(NSA_HEADS)], axis=0)

    s = _dot_nt(q_all, kc)
    ps = []
    imp = [jnp.zeros((tq, LANES), F32) for _ in range(NSA_KV_HEADS)]
    for h in range(NSA_HEADS):
        sr = jnp.where(mask_c, s[hrows(h)] - NSA_SLOPES[h] * dist_cf, NEG_INF)
        mx = jnp.max(sr, axis=1, keepdims=True)
        e = jnp.where(mask_c, jnp.exp(sr - mx), 0.0)
        p = e / jnp.maximum(jnp.sum(e, axis=1, keepdims=True), 1e-30)
        ps.append(p.astype(BF16))
        imp[h // NSA_GROUP] = imp[h // NSA_GROUP] + p
    o_c = _dot(jnp.concatenate(ps, axis=0), vc)

    wk = WINDOW + tq
    kstart = pl.multiple_of(jnp.maximum(q0 - WINDOW, 0), tq)
    kt_w = kvt_ref[4, pl.ds(kstart, wk), :]
    vt_w = kvt_ref[5, pl.ds(kstart, wk), :]
    s_w = _dot_nt(q_all, kt_w)
    dist_w = tok - (kstart + lax.broadcasted_iota(jnp.int32, (1, wk), 1))
    dist_wf = dist_w.astype(F32)
    bias_w = jnp.where((dist_w >= 0) & (dist_w <= WINDOW), 0.0, NEG_INF)
    o_w = []
    for g in range(NSA_KV_HEADS):
        es = []
        for r in range(NSA_GROUP):
            h = NSA_GROUP * g + r
            x = s_w[hrows(h)] - NSA_SLOPES[h] * dist_wf + bias_w
            e = jnp.exp(x - jnp.max(x, axis=1, keepdims=True))
            es.append((e / jnp.sum(e, axis=1, keepdims=True)).astype(BF16))
        o_w.append(_dot(jnp.concatenate(es, axis=0), vt_w))

    score, sel = [], []
    for g in range(NSA_KV_HEADS):
        imp_sel = imp[g] + pltpu.roll(imp[g], HALF, 1)
        sc = jnp.where(valid, imp_sel + jnp.where(forced, FORCE_BONUS, 0.0), NEG_INF)
        score.append(jnp.where(lo, sc, BELOW_NEG_INF))
        sel.append(jnp.zeros((tq, LANES), F32))
    for _ in range(k_top):
        for g in range(NSA_KV_HEADS):
            mx = jnp.max(score[g], axis=1, keepdims=True)
            idx = jnp.min(jnp.where(score[g] == mx, lane, 4 * LANES), axis=1, keepdims=True)
            hit = lane == idx
            sel[g] = jnp.where(hit, 1.0, sel[g])
            score[g] = jnp.where(hit, BELOW_NEG_INF, score[g])
    sel_b = [x.astype(BF16) for x in sel]

    n_sub = tk // LANES
    blk_row = lax.broadcasted_iota(jnp.int32, (LANES, tk), 0)
    key_blk = lax.broadcasted_iota(jnp.int32, (LANES, tk), 1) // SEL_BLOCK
    col_k = lax.broadcasted_iota(jnp.int32, (1, tk), 1)
    m_ref[...] = jnp.full(m_ref.shape, NEG_INF, F32)
    acc_ref[...] = jnp.zeros(acc_ref.shape, F32)

    def sel_tile(j, carry):
        start = pl.multiple_of(j * tk, tk)
        kt = kvt_ref[2, pl.ds(start, tk), :]
        vt = kvt_ref[3, pl.ds(start, tk), :]
        s = _dot_nt(q_all, kt)
        dist = tok - (start + col_k)
        dist_f = dist.astype(F32)
        causal = jnp.where(dist >= 0, 0.0, NEG_INF)
        expand = (blk_row == (tk // SEL_BLOCK) * j + key_blk).astype(BF16)
        for g in range(NSA_KV_HEADS):
            bias = (_dot(sel_b[g], expand) - 1.0) * (-NEG_INF) + causal
            es = []
            for r in range(NSA_GROUP):
                h = NSA_GROUP * g + r
                x = s[hrows(h)] - NSA_SLOPES[h] * dist_f + bias
                xm = x[:, 0:LANES]
                for i in range(1, n_sub):
                    xm = jnp.maximum(xm, x[:, i * LANES:(i + 1) * LANES])
                m_old = m_ref[hrows(h)]
                m_new = jnp.maximum(m_old, jnp.max(xm, axis=1, keepdims=True))
                m_ref[hrows(h)] = m_new
                acc_ref[hrows(h)] = jnp.exp(m_old - m_new) * acc_ref[hrows(h)]
                es.append(jnp.exp(x - jnp.concatenate([m_new] * n_sub, axis=1)).astype(BF16))
            vg = jnp.where(in_grp[g], vt, jnp.ones_like(vt))
            acc_ref[grows(g)] += _dot(jnp.concatenate(es, axis=0), vg)
        return carry

    lax.fori_loop(0, (q0 + tq - 1) // tk + 1, sel_tile, 0)
    acc = acc_ref[...]
    o_s = [acc[grows(g)] / jnp.maximum(acc[grows(g), den_lane[g]:den_lane[g] + 1], 1e-30) for g in range(NSA_KV_HEADS)]

    for h in range(NSA_HEADS):
        g, r = divmod(h, NSA_GROUP)
        rows = slice(r * tq, (r + 1) * tq)
        o = (gates[:, 3 * h:3 * h + 1] * o_c[hrows(h)] + gates[:, 3 * h + 1:3 * h + 2] * o_s[g][rows]
             + gates[:, 3 * h + 2:3 * h + 3] * o_w[g][rows])
        o_ref[:, h * LANES:(h + 1) * LANES] = jnp.where(in_grp[g], o, 0.0).astype(BF16)


def _nsa_prompt(q_pad, kvt, cmp_perm, gates, n_seq, t):
    tq = NSA_TQ
    nq = t // tq
    k_top = min(TOP_N, t // SEL_BLOCK)
    assert k_top >= 3 and t >= WINDOW + tq and t % NSA_TK == 0
    return pl.pallas_call(
        functools.partial(_nsa_prompt_kernel, k_top=k_top, tk=NSA_TK),
        grid=(n_seq, nq),
        in_specs=[pl.BlockSpec((tq, 1024), lambda b, i: (b * nq + i, 0)),
                  pl.BlockSpec((None, 6, t, LANES), lambda b, i: (b, 0, 0, 0)),
                  pl.BlockSpec((None, LANES, 2 * LANES), lambda b, i: (b, 0, 0)),
                  pl.BlockSpec((tq, LANES), lambda b, i: (b * nq + i, 0))],
        out_specs=pl.BlockSpec((tq, 1024), lambda b, i: (b * nq + i, 0)),
        out_shape=jax.ShapeDtypeStruct((n_seq * t, 1024), BF16),
        scratch_shapes=[pltpu.VMEM((NSA_HEADS * tq, LANES), F32), pltpu.VMEM((NSA_HEADS * tq, LANES), F32)],
        compiler_params=_cparams(("parallel", "parallel")),
        name="nsa_prompt",
    )(q_pad, kvt, cmp_perm, gates)


def _permute_cmp(cmp, n_seq, t):
    nc = t // CMP_BLOCK
    c = cmp.reshape(n_seq, nc // 2, 2, 2 * LANES)
    c = jnp.pad(c, ((0, 0), (0, HALF - nc // 2), (0, 0), (0, 0)))
    return jnp.transpose(c, (0, 2, 1, 3)).reshape(n_seq, LANES, 2 * LANES)


def _compress_pool_kernel(x_ref, w_ref, o_ref):
    pb = x_ref.shape[0]
    page = x_ref.shape[-1]
    nb = page // CMP_BLOCK
    blk = lax.broadcasted_iota(jnp.int32, (8, page), 0)
    seg = (lax.broadcasted_iota(jnp.int32, (8, page), 1) // CMP_BLOCK == blk).astype(BF16)
    items = [(p, j) for p in range(pb) for j in range(2)]
    parts = [_split3(x_ref[p, j].reshape(LANES, page) * w_ref[j]) for p, j in items]
    sums = [_dot_nt(seg, k1) + _dot_nt(seg, k2) + _dot_nt(seg, k3) for k1, k2, k3 in parts]
    for (p, j), res in zip(items, sums):
        o_ref[p, j * nb:(j + 1) * nb, :] = res[0:nb]


def _compress_pool(cache_t, w_t, pb):
    L, n_pool = cache_t.shape[:2]
    page = cache_t.shape[-1]
    nb = page // CMP_BLOCK
    return pl.pallas_call(
        _compress_pool_kernel,
        grid=(L, n_pool // pb),
        in_specs=[pl.BlockSpec((None, pb, 2, NSA_KV_HEADS, NSA_HEAD_DIM, page), lambda l, i: (l, i, 0, 0, 0, 0)),
                  pl.BlockSpec((None, 2, LANES, page), lambda l, i: (l, 0, 0, 0))],
        out_specs=pl.BlockSpec((None, pb, 2 * nb, LANES), lambda l, i: (l, i, 0, 0)),
        out_shape=jax.ShapeDtypeStruct((L, n_pool, 2 * nb, LANES), F32),
        compiler_params=_cparams(("parallel", "parallel")),
        name="compress_pool",
    )(cache_t, w_t)


def _nsa_dec_cmp_kernel(pt_ref, pool_ref, q_ref, gate_ref, slope_ref, win_ref, kvn_ref, part_ref, idx_ref, gath_ref,
                        *, n_pages, past_len, k_top):
    b = pl.program_id(0)
    t = past_len
    for p in range(n_pages):
        tile = pool_ref[pt_ref[b, p]]
        for r in range(8):
            gath_ref[r, p:p + 1, :] = tile[r:r + 1, :]
    x = [gath_ref[r] for r in range(8)]
    q = q_ref[...]
    slope = slope_ref[:, 0:1]
    gates = gate_ref[...]
    lane = lax.broadcasted_iota(jnp.int32, (1, LANES), 1)
    lane_p = lax.broadcasted_iota(jnp.int32, (1, 2 * n_pages), 1)
    page_of = lane_p % n_pages
    ss, masks = [], []
    for pair in range(2):
        kmat = jnp.concatenate([x[2 * pair], x[2 * pair + 1]], axis=0).astype(BF16)
        blk = page_of * 4 + 2 * pair + lane_p // n_pages
        dist = t - (blk * CMP_BLOCK + (CMP_BLOCK - 1))
        mask = dist >= 0
        s = _dot_nt(q, kmat) - slope * dist.astype(F32)
        ss.append(jnp.where(mask, s, NEG_INF))
        masks.append(mask)
    mx = jnp.maximum(jnp.max(ss[0], axis=1, keepdims=True), jnp.max(ss[1], axis=1, keepdims=True))
    es = [jnp.where(masks[i], jnp.exp(ss[i] - mx), 0.0) for i in range(2)]
    den = jnp.maximum(jnp.sum(es[0], axis=1, keepdims=True) + jnp.sum(es[1], axis=1, keepdims=True), 1e-30)
    ps = [e / den for e in es]
    o_c = jnp.zeros((NSA_HEADS, LANES), F32)
    for pair in range(2):
        vmat = jnp.concatenate([x[4 + 2 * pair], x[5 + 2 * pair]], axis=0).astype(BF16)
        o_c = o_c + _dot(ps[pair].astype(BF16), vmat)
    n_sel_blk = lane_p // n_pages + 2 * page_of
    cur = t // SEL_BLOCK
    forced = (n_sel_blk == 0) | (n_sel_blk == cur) | (n_sel_blk == cur - 1)
    valid = n_sel_blk * SEL_BLOCK <= t
    idx_rows = []
    for g in range(NSA_KV_HEADS):
        imp_a = jnp.sum(ps[0][NSA_GROUP * g:NSA_GROUP * (g + 1)], axis=0, keepdims=True)
        imp_b = jnp.sum(ps[1][NSA_GROUP * g:NSA_GROUP * (g + 1)], axis=0, keepdims=True)
        ev = imp_a + pltpu.roll(imp_a, n_pages, 1)
        od = imp_b + pltpu.roll(imp_b, n_pages, 1)
        imp = jnp.where(lane_p < n_pages, ev, od)
        score = jnp.where(valid, imp + jnp.where(forced, FORCE_BONUS, 0.0), NEG_INF)
        chosen = jnp.where(lane == 0, cur, 0)
        for it in range(1, k_top):
            mxs = jnp.max(score, axis=1, keepdims=True)
            pick = jnp.min(jnp.where(score == mxs, n_sel_blk, 1 << 30), axis=1, keepdims=True)
            score = jnp.where(n_sel_blk == pick, BELOW_NEG_INF, score)
            chosen = jnp.where(lane == it, pick, chosen)
        idx_rows.append(chosen)
    rowi = lax.broadcasted_iota(jnp.int32, (8, LANES), 0)
    idx_ref[...] = jnp.where(rowi == 0, idx_rows[0], jnp.where(rowi == 1, idx_rows[1], 0))
    wl = win_ref.shape[-1]
    kw = win_ref[0].reshape(LANES, wl).astype(BF16)
    vw = win_ref[1].reshape(LANES, wl).astype(BF16)
    col = lax.broadcasted_iota(jnp.int32, (1, wl), 1)
    dist_w = wl - col
    mask_w = (t - dist_w >= 0) & (dist_w <= WINDOW)
    s_w = jnp.where(mask_w, _dot(q, kw) - slope * dist_w.astype(F32), NEG_INF)
    kvn = kvn_ref[...]
    qf = q.astype(F32)
    s_n = jnp.sum(qf * kvn[4:5, :].astype(BF16).astype(F32), axis=1, keepdims=True)
    mw = jnp.maximum(jnp.max(s_w, axis=1, keepdims=True), s_n)
    e_w = jnp.where(mask_w, jnp.exp(s_w - mw), 0.0)
    e_n = jnp.exp(s_n - mw)
    den_w = jnp.maximum(jnp.sum(e_w, axis=1, keepdims=True) + e_n, 1e-30)
    o_w = (_dot_nt((e_w / den_w).astype(BF16), vw)
           + (e_n / den_w).astype(BF16).astype(F32) * kvn[5:6, :].astype(BF16).astype(F32))
    part_ref[...] = gates[:, 0:1] * o_c + gates[:, 2:3] * o_w


def _nsa_dec_sel_kernel(pt_ref, sel_ref, blk_ref, q_ref, gate_ref, slope_ref, kvn_ref, part_ref, o_ref,
                        stage_ref, *, past_len, k_top):
    b = pl.program_id(0)
    j = pl.program_id(1)
    t = past_len
    stage_ref[j] = blk_ref[...]

    @pl.when(j == pl.num_programs(1) - 1)
    def _():
        q = q_ref[...]
        slope = slope_ref[:, 0:1]
        kvn = kvn_ref[...]
        page = blk_ref.shape[-1]
        halves = page // SEL_BLOCK
        col = lax.broadcasted_iota(jnp.int32, (1, page), 1)
        rowh = lax.broadcasted_iota(jnp.int32, (NSA_HEADS, 1), 0)
        lane = lax.broadcasted_iota(jnp.int32, (1, LANES), 1)
        s_new = jnp.sum(q.astype(F32) * kvn[2:3, :].astype(BF16).astype(F32), axis=1, keepdims=True)
        v_new = kvn[3:4, :].astype(BF16).astype(F32)
        o_groups = []
        for g in range(NSA_KV_HEADS):
            ss, masks, vts = [], [], []
            has_cur = False
            for i in range(k_top):
                jj = g * k_top + i
                n = sel_ref[b, jj]
                kt = stage_ref[jj, 0].reshape(LANES, page).astype(BF16)
                vts.append(stage_ref[jj, 1].reshape(LANES, page).astype(BF16))
                pos = (n // halves) * page + col
                mask = (col // SEL_BLOCK == n % halves) & (pos < t)
                ss.append(jnp.where(mask, _dot(q, kt) - slope * (t - pos).astype(F32), NEG_INF))
                masks.append(mask)
                has_cur = jnp.logical_or(has_cur, n * SEL_BLOCK + SEL_BLOCK > t)
            s_n = jnp.where(has_cur, s_new, NEG_INF)
            mx = s_n
            for s in ss:
                mx = jnp.maximum(mx, jnp.max(s, axis=1, keepdims=True))
            es = [jnp.where(masks[i], jnp.exp(ss[i] - mx), 0.0) for i in range(k_top)]
            e_n = jnp.where(has_cur, jnp.exp(s_n - mx), 0.0)
            den = e_n
            for e in es:
                den = den + jnp.sum(e, axis=1, keepdims=True)
            den = jnp.maximum(den, 1e-30)
            o = (e_n / den).astype(BF16).astype(F32) * v_new
            for i in range(k_top):
                o = o + _dot_nt((es[i] / den).astype(BF16), vts[i])
            o_groups.append(o)
        o_s = jnp.where(rowh // NSA_GROUP == 0, o_groups[0], o_groups[1])
        o = part_ref[...] + gate_ref[:, 1:2] * o_s
        in_half = (lane // HALF) == (rowh // NSA_GROUP)
        o_ref[...] = jnp.where(in_half, o, 0.0).astype(BF16)


def _nsa_decode(page_table, cmp_pool, cache_t, win_t, l, q_pad, gates, kv_new, slopes, past_len):
    bs, n_pages = page_table.shape
    n_pool, page = cache_t.shape[1], cache_t.shape[-1]
    k_top = min(TOP_N, past_len // SEL_BLOCK + 1)
    q3 = q_pad.reshape(bs, NSA_HEADS, LANES)
    g3 = jnp.pad(gates[:, :3 * NSA_HEADS].reshape(bs, NSA_HEADS, 3), ((0, 0), (0, 0), (0, LANES - 3)))
    kvn = kv_new.reshape(bs, 6, LANES)
    wl = win_t.shape[-1]
    part, idx = pl.pallas_call(
        functools.partial(_nsa_dec_cmp_kernel, n_pages=n_pages, past_len=past_len, k_top=k_top),
        grid_spec=pltpu.PrefetchScalarGridSpec(
            num_scalar_prefetch=1, grid=(bs,),
            in_specs=[pl.BlockSpec((None, n_pool, 8, LANES), lambda b, pt: (l, 0, 0, 0)),
                      pl.BlockSpec((None, NSA_HEADS, LANES), lambda b, pt: (b, 0, 0)),
                      pl.BlockSpec((None, NSA_HEADS, LANES), lambda b, pt: (b, 0, 0)),
                      pl.BlockSpec((NSA_HEADS, LANES), lambda b, pt: (0, 0)),
                      pl.BlockSpec((None, None, 2, NSA_KV_HEADS, NSA_HEAD_DIM, wl), lambda b, pt: (l, b, 0, 0, 0, 0)),
                      pl.BlockSpec((None, 6, LANES), lambda b, pt: (b, 0, 0))],
            out_specs=[pl.BlockSpec((None, NSA_HEADS, LANES), lambda b, pt: (b, 0, 0)),
                       pl.BlockSpec((None, 8, LANES), lambda b, pt: (b, 0, 0))],
            scratch_shapes=[pltpu.VMEM((8, n_pages, LANES), F32)]),
        out_shape=[jax.ShapeDtypeStruct((bs, NSA_HEADS, LANES), F32),
                   jax.ShapeDtypeStruct((bs, 8, LANES), jnp.int32)],
        compiler_params=_cparams(("arbitrary",)),
        name="nsa_dec_cmp",
    )(page_table, cmp_pool, q3, g3, slopes, win_t, kvn)
    sel = idx[:, :NSA_KV_HEADS, :k_top].reshape(bs, NSA_KV_HEADS * k_top)
    halves = page // SEL_BLOCK

    def blk_map(b, j, pt, sl):
        pg = pt[b, jnp.minimum(sl[b, j] // halves, n_pages - 1)]
        return (l, pg, 1, 0, 0, 0)

    out = pl.pallas_call(
        functools.partial(_nsa_dec_sel_kernel, past_len=past_len, k_top=k_top),
        grid_spec=pltpu.PrefetchScalarGridSpec(
            num_scalar_prefetch=2, grid=(bs, NSA_KV_HEADS * k_top),
            in_specs=[pl.BlockSpec((None, None, 2, NSA_KV_HEADS, NSA_HEAD_DIM, page), blk_map),
                      pl.BlockSpec((None, NSA_HEADS, LANES), lambda b, j, pt, sl: (b, 0, 0)),
                      pl.BlockSpec((None, NSA_HEADS, LANES), lambda b, j, pt, sl: (b, 0, 0)),
                      pl.BlockSpec((NSA_HEADS, LANES), lambda b, j, pt, sl: (0, 0)),
                      pl.BlockSpec((None, 6, LANES), lambda b, j, pt, sl: (b, 0, 0)),
                      pl.BlockSpec((None, NSA_HEADS, LANES), lambda b, j, pt, sl: (b, 0, 0))],
            out_specs=pl.BlockSpec((None, NSA_HEADS, LANES), lambda b, j, pt, sl: (b, 0, 0)),
            scratch_shapes=[pltpu.VMEM((NSA_KV_HEADS * k_top, 2, NSA_KV_HEADS, NSA_HEAD_DIM, page), F32)]),
        out_shape=jax.ShapeDtypeStruct((bs, NSA_HEADS, LANES), BF16),
        compiler_params=_cparams(("arbitrary", "arbitrary")),
        name="nsa_dec_sel",
    )(page_table, sel, cache_t, q3, g3, slopes, kvn, part)
    return out.reshape(bs, NSA_HEADS * LANES)


def _pool_tail(sums, cur, cnts, pw_ref, scale_ref, o_ref):
    for gi in range(len(POOL_WINDOWS)):
        sl = slice(gi * LANES, (gi + 1) * LANES)
        d = sums[gi] / cnts[gi] - cur[:, sl]
        y = _dot(d.astype(BF16), pw_ref[gi])
        o_ref[:, sl] = (y * scale_ref[:, sl]).astype(BF16)


def _pool_seq_kernel(p_ref, pprev_ref, hist_ref, pw_ref, scale_ref, o_ref, zz_ref, *, seq_tiles, tm, start_pos):
    i = pl.program_id(0)
    first = (i % seq_tiles) == 0
    cur = p_ref[...]
    zz_ref[0:16, :] = jnp.where(first, hist_ref[...], pprev_ref[...])
    zz_ref[16:16 + tm, :] = cur
    pos = start_pos + (i % seq_tiles) * tm + lax.broadcasted_iota(jnp.int32, (tm, 1), 0)
    sums, cnts = [], []
    for gi, w in enumerate(POOL_WINDOWS):
        sl = slice(gi * LANES, (gi + 1) * LANES)
        s = cur[:, sl]
        for k in range(1, w):
            s = s + zz_ref[16 - k:16 - k + tm, sl]
        sums.append(s)
        cnts.append(jnp.minimum(pos + 1, w).astype(F32))
    _pool_tail(sums, cur, cnts, pw_ref, scale_ref, o_ref)


def _pool_step_kernel(z_ref, pw_ref, scale_ref, o_ref, *, start_pos):
    z = z_ref[...]
    cur = z[:, 15, :]
    sums, cnts = [], []
    for gi, w in enumerate(POOL_WINDOWS):
        sl = slice(gi * LANES, (gi + 1) * LANES)
        sums.append(jnp.sum(z[:, 16 - w:16, sl], axis=1))
        cnts.append(float(min(start_pos + 1, w)))
    _pool_tail(sums, cur, cnts, pw_ref, scale_ref, o_ref)


def _pool_seq(p_pool, hist16, pool_w, pool_scale, t, tm, start_pos):
    n = p_pool.shape[0]
    tps = t // tm
    return pl.pallas_call(
        functools.partial(_pool_seq_kernel, seq_tiles=tps, tm=tm, start_pos=start_pos),
        grid=(n // tm,),
        in_specs=[pl.BlockSpec((tm, 512), lambda i: (i, 0)),
                  pl.BlockSpec((16, 512), lambda i: (jnp.maximum(i * (tm // 16) - 1, 0), 0)),
                  pl.BlockSpec((None, 16, 512), lambda i: (i // tps, 0, 0)),
                  pl.BlockSpec((4, LANES, LANES), lambda i: (0, 0, 0)),
                  pl.BlockSpec((1, 512), lambda i: (0, 0))],
        out_specs=pl.BlockSpec((tm, 512), lambda i: (i, 0)),
        out_shape=jax.ShapeDtypeStruct((n, 512), BF16),
        scratch_shapes=[pltpu.VMEM((tm + 16, 512), F32)],
        compiler_params=_cparams(("parallel",)),
        name="pool_seq",
    )(p_pool, p_pool, hist16, pool_w, pool_scale)


def _pool_step(z16, pool_w, pool_scale, start_pos):
    bs = z16.shape[0]
    return pl.pallas_call(
        functools.partial(_pool_step_kernel, start_pos=start_pos),
        grid=(1,),
        in_specs=[pl.BlockSpec((bs, 16, 512), lambda i: (0, 0, 0)),
                  pl.BlockSpec((4, LANES, LANES), lambda i: (0, 0, 0)),
                  pl.BlockSpec((1, 512), lambda i: (0, 0))],
        out_specs=pl.BlockSpec((bs, 512), lambda i: (0, 0)),
        out_shape=jax.ShapeDtypeStruct((bs, 512), BF16),
        compiler_params=_cparams(("arbitrary",)),
        name="pool_step",
    )(z16, pool_w, pool_scale)


def _merge_kernel(x_ref, sc1_ref, sh1_ref, g1_ref, sc2_ref, sh2_ref, orw_ref, onsa_ref, opool_ref,
                  wg_ref, wbr_ref, wbn_ref, wbp_ref, wout_ref, lng_ref, lnb_ref, wr_ref, br_ref,
                  x1_ref, u2_ref, route_ref, *, alpha):
    x = x_ref[...]
    d = x.shape[1]
    u = (x * (1.0 + sc1_ref[...]) + sh1_ref[...]).astype(BF16)
    mixed = jnp.zeros(x.shape, F32)
    for bi, (o_ref, wb_ref) in enumerate(((orw_ref, wbr_ref), (onsa_ref, wbn_ref), (opool_ref, wbp_ref))):
        gate = _sigmoid(_dot(u, wg_ref[:, bi * d:(bi + 1) * d]))
        mixed = mixed + gate * _dot(o_ref[...], wb_ref[...])
    m = _dot(mixed.astype(BF16), wout_ref[...])
    x1 = _layer_norm(alpha * x + (1.0 + g1_ref[...]) * m, lng_ref[...], lnb_ref[...])
    x1_ref[...] = x1
    u2 = x1 * (1.0 + sc2_ref[...]) + sh2_ref[...]
    u2b = u2.astype(BF16)
    u2_ref[...] = u2b
    lg = _dot(u2b, wr_ref[...]) + br_ref[...]
    lane = lax.broadcasted_iota(jnp.int32, (1, LANES), 1)
    is_g = lane < MOE_GROUPS
    mg = jnp.max(jnp.where(is_g, lg, BELOW_NEG_INF), axis=1, keepdims=True)
    gsel = jnp.min(jnp.where(is_g & (lg == mg), lane, LANES), axis=1, keepdims=True)
    wgrp = 1.0 / jnp.sum(jnp.where(is_g, jnp.exp(lg - mg), 0.0), axis=1, keepdims=True)
    e_lane = lane - ROUTE_LANE0
    in_grp = (e_lane >= 0) & (e_lane < MOE_GROUPS * EXPERTS_PER_GROUP) & ((e_lane // EXPERTS_PER_GROUP) == gsel)
    v1 = jnp.max(jnp.where(in_grp, lg, BELOW_NEG_INF), axis=1, keepdims=True)
    i1 = jnp.min(jnp.where(in_grp & (lg == v1), lane, LANES), axis=1, keepdims=True)
    rest = in_grp & (lane != i1)
    v2 = jnp.max(jnp.where(rest, lg, BELOW_NEG_INF), axis=1, keepdims=True)
    i2 = jnp.min(jnp.where(rest & (lg == v2), lane, LANES), axis=1, keepdims=True)
    e2 = jnp.exp(v2 - v1)
    w1 = wgrp / (1.0 + e2)
    w2 = wgrp * e2 / (1.0 + e2)
    for g in range(MOE_GROUPS):
        src = lane + (ROUTE_LANE0 + EXPERTS_PER_GROUP * g)
        route_ref[g] = jnp.where(lane < EXPERTS_PER_GROUP,
                                 jnp.where(src == i1, w1, jnp.where(src == i2, w2, 0.0)), 0.0)


def _merge(x, mods, o_rw, o_nsa, o_pool, wts, rows_per_group, tm, alpha):
    n, d = x.shape
    full = lambda a: pl.BlockSpec(a.shape, lambda i: (0,) * a.ndim)
    row = lambda c: pl.BlockSpec((tm, c), lambda i: (i, 0))
    return pl.pallas_call(
        functools.partial(_merge_kernel, alpha=alpha),
        grid=(n // tm,),
        in_specs=[row(d)] + [_mod_spec(m, rows_per_group, tm) for m in mods]
                 + [row(512), row(1024), row(512)] + [full(w) for w in wts],
        out_specs=[row(d), row(d), pl.BlockSpec((MOE_GROUPS, tm, LANES), lambda i: (0, i, 0))],
        out_shape=[jax.ShapeDtypeStruct((n, d), F32), jax.ShapeDtypeStruct((n, d), BF16),
                   jax.ShapeDtypeStruct((MOE_GROUPS, n, LANES), F32)],
        compiler_params=_cparams(("parallel",), 56),
        name="merge",
    )(x, *mods, o_rw, o_nsa, o_pool, *wts)


def _moe_kernel(x1_ref, u2_ref, route_ref, g2_ref, wg_ref, wu_ref, wd_ref, lng_ref, lnb_ref, o_ref, acc_ref, *, alpha):
    g = pl.program_id(1)

    @pl.when(g == 0)
    def _():
        acc_ref[...] = jnp.zeros(acc_ref.shape, F32)

    u = u2_ref[...]
    hg = _dot(u, wg_ref[...])
    h = (hg * _sigmoid(hg) * _dot(u, wu_ref[...])).astype(BF16)
    route = route_ref[...]
    y = acc_ref[...]
    for e in range(EXPERTS_PER_GROUP):
        w = route[:, e:e + 1]
        ye = _dot(h[:, e * EXPERT_HIDDEN:(e + 1) * EXPERT_HIDDEN], wd_ref[e * EXPERT_HIDDEN:(e + 1) * EXPERT_HIDDEN, :])
        y = y + jnp.where(w != 0.0, w * ye, 0.0)
    acc_ref[...] = y

    @pl.when(g == pl.num_programs(1) - 1)
    def _():
        h2 = alpha * x1_ref[...] + (1.0 + g2_ref[...]) * acc_ref[...]
        o_ref[...] = _layer_norm(h2, lng_ref[...], lnb_ref[...])


def _moe(x1, u2, route, g2, w_gate, w_up, w_down, ln_g, ln_b, rows_per_group, tm, alpha):
    n, d = x1.shape
    hid = EXPERTS_PER_GROUP * EXPERT_HIDDEN
    r = g2.shape[1]
    return pl.pallas_call(
        functools.partial(_moe_kernel, alpha=alpha),
        grid=(n // tm, MOE_GROUPS),
        in_specs=[pl.BlockSpec((tm, d), lambda i, g: (i, 0)),
                  pl.BlockSpec((tm, d), lambda i, g: (i, 0)),
                  pl.BlockSpec((None, tm, LANES), lambda i, g: (g, i, 0)),
                  pl.BlockSpec((None, r, d), lambda i, g: ((i * tm) // rows_per_group, 0, 0)),
                  pl.BlockSpec((None, d, hid), lambda i, g: (g, 0, 0)),
                  pl.BlockSpec((None, d, hid), lambda i, g: (g, 0, 0)),
                  pl.BlockSpec((None, hid, d), lambda i, g: (g, 0, 0)),
                  pl.BlockSpec((1, d), lambda i, g: (0, 0)),
                  pl.BlockSpec((1, d), lambda i, g: (0, 0))],
        out_specs=pl.BlockSpec((tm, d), lambda i, g: (i, 0)),
        out_shape=jax.ShapeDtypeStruct((n, d), F32),
        scratch_shapes=[pltpu.VMEM((tm, d), F32)],
        compiler_params=_cparams(("parallel", "arbitrary"), 56),
        name="moe",
    )(x1, u2, route, g2, w_gate, w_up, w_down, ln_g, ln_b)


def _pad_heads(w, axis):
    shp = w.shape
    w = w.reshape(shp[:axis] + (NSA_KV_HEADS, NSA_GROUP, 1, NSA_HEAD_DIM) + shp[axis + 1:])
    sel = jnp.eye(NSA_KV_HEADS, dtype=w.dtype).reshape((1,) * axis + (NSA_KV_HEADS, 1, NSA_KV_HEADS, 1) + (1,) * (len(shp) - axis - 1))
    w = w * sel
    return w.reshape(shp[:axis] + (NSA_HEADS * LANES,) + shp[axis + 1:])


def _prep_weights(w_in, rw_w_up, rw_a_up, w_branch, moe_router_g, moe_bias_g, moe_router_e, moe_bias_e,
                  moe_w_gate, moe_w_up, moe_w_down, nsa_cmp_wk, nsa_cmp_wv):
    L, D, _ = w_in.shape
    w = {}
    w['rw'] = w_in[:, :, :RW_END].astype(BF16)
    wq = _pad_heads(w_in[:, :, RW_END:NSA_Q_END] * (NSA_HEAD_DIM ** -0.5), 2)
    wg = jnp.pad(w_in[:, :, NSA_KV_END:NSA_END], ((0, 0), (0, 0), (0, LANES - 3 * NSA_HEADS)))
    w['nsa'] = jnp.concatenate([wq, w_in[:, :, NSA_Q_END:NSA_KV_END], wg], axis=-1).astype(BF16)
    w['pool'] = w_in[:, :, NSA_END:POOL_END].astype(BF16)
    w['gate'] = w_in[:, :, POOL_END:].astype(BF16)
    z = jnp.zeros((L, HALF, RW_WIDTH), F32)
    w['wwa'] = jnp.concatenate([jnp.concatenate([rw_w_up, z], axis=2),
                                jnp.concatenate([z, rw_a_up], axis=2)], axis=1).astype(BF16)
    w['b_rw'] = w_branch[:, 0].astype(BF16)
    w['b_nsa'] = _pad_heads(w_branch[:, 1], 1).astype(BF16)
    w['b_pool'] = w_branch[:, 2].astype(BF16)
    pad_r = LANES - MOE_GROUPS - MOE_GROUPS * EXPERTS_PER_GROUP
    w['router'] = jnp.pad(jnp.concatenate([moe_router_g, moe_router_e], axis=2), ((0, 0), (0, 0), (0, pad_r))).astype(BF16)
    w['router_b'] = jnp.pad(jnp.concatenate([moe_bias_g, moe_bias_e], axis=1), ((0, 0), (0, pad_r)))[:, None, :]
    hid = EXPERTS_PER_GROUP * EXPERT_HIDDEN
    grp = lambda a: jnp.transpose(a.reshape(L, MOE_GROUPS, EXPERTS_PER_GROUP, D, EXPERT_HIDDEN),
                                  (0, 1, 3, 2, 4)).reshape(L, MOE_GROUPS, D, hid).astype(BF16)
    w['moe_gate'] = grp(moe_w_gate)
    w['moe_up'] = grp(moe_w_up)
    w['moe_down'] = moe_w_down.reshape(L, MOE_GROUPS, hid, D).astype(BF16)
    w['cmp_wk'] = jnp.tile(nsa_cmp_wk, (1, 1, NSA_KV_HEADS))
    w['cmp_wv'] = jnp.tile(nsa_cmp_wv, (1, 1, NSA_KV_HEADS))
    return w


def kernel(x_prompt, x_sample, cache_nsa_kv, cache_win_kv, state_rwkv, state_rwkv_shift, state_pool, page_table, c_prompt, c_sample, ada_w, ada_b, w_in, rw_mu, rw_w0, rw_w_up, rw_a0, rw_a_up, rw_g_up, rw_k_k, rw_k_a, rw_r_k, rw_gn_g, rw_gn_b, nsa_cmp_wk, nsa_cmp_wv, pool_w, pool_scale, w_branch, w_out, ln1_g, ln1_b, moe_router_g, moe_bias_g, moe_router_e, moe_bias_e, moe_w_gate, moe_w_up, moe_w_down, ln2_g, ln2_b):
    bp, t, d = x_prompt.shape
    bs, ts, _ = x_sample.shape
    L = ada_w.shape[0]
    n_pool, page = cache_nsa_kv.shape[1], cache_nsa_kv.shape[2]
    n_pages = page_table.shape[1]
    past_len = n_pages * page
    assert ts == 1 and d == D_MODEL
    assert t % 256 == 0 and t // CMP_BLOCK <= LANES and past_len % SEL_BLOCK == 0
    assert cache_win_kv.shape[2] == WINDOW and bs % 8 == 0 and 2 * n_pages == LANES
    alpha = (2 * L) ** 0.25
    tm = 256

    w = _prep_weights(w_in, rw_w_up, rw_a_up, w_branch, moe_router_g, moe_bias_g, moe_router_e, moe_bias_e,
                      moe_w_gate, moe_w_up, moe_w_down, nsa_cmp_wk, nsa_cmp_wv)
    w_out_b = w_out.astype(BF16)
    pool_w_b = pool_w.astype(BF16)
    gup_b = rw_g_up.astype(BF16)

    nb = bp + bs
    nb_pad = -(-nb // 8) * 8
    c_all = jnp.pad(jnp.concatenate([c_prompt, c_sample], axis=0), ((0, nb_pad - nb), (0, 0)))
    mods = _ada_mod(c_all, ada_w, ada_b)

    cache_t = jnp.transpose(cache_nsa_kv, (0, 1, 3, 4, 5, 2)).astype(F32)
    win_cache = jnp.transpose(cache_win_kv, (0, 1, 3, 4, 5, 2)).astype(F32)
    pb = 8 if n_pool % 8 == 0 else 1
    cmp_w_t = jnp.tile(jnp.transpose(jnp.stack([nsa_cmp_wk, nsa_cmp_wv], axis=1), (0, 1, 3, 2)),
                       (1, 1, NSA_KV_HEADS, page // CMP_BLOCK))
    cmp_pool = _compress_pool(cache_t, cmp_w_t, pb)
    slopes = jnp.broadcast_to(jnp.asarray(NSA_SLOPES, F32)[:, None], (NSA_HEADS, LANES))

    xp = x_prompt.reshape(bp * t, d)
    xs = x_sample.reshape(bs, d)
    zeros_prev = jnp.zeros((bp, RW_COLS), F32)
    zeros_hist = jnp.zeros((bp, 16, POOL_WIDTH), F32)
    outs = {k: [] for k in ('nsa_p', 'nsa_s', 'win_p', 'win_s', 'rw_p', 'rw_s', 'sh_p', 'sh_s', 'pool_p', 'pool_s')}

    for l in range(L):
        vec = lambda a: a[l].reshape(1, -1)
        rw_wts = (vec(rw_mu), vec(rw_w0), vec(rw_a0), w['wwa'][l], gup_b[l], vec(rw_k_k), vec(rw_k_a), vec(rw_r_k))
        merge_wts = (w['gate'][l], w['b_rw'][l], w['b_nsa'][l], w['b_pool'][l], w_out_b[l], vec(ln1_g), vec(ln1_b),
                     w['router'][l], w['router_b'][l])
        mod_p = [m[:, None, :] for m in jnp.split(mods[l, :bp], 6, axis=-1)]
        mod_s = [m[None] for m in jnp.split(mods[l, bp:bp + bs], 6, axis=-1)]

        p_rw = _proj(xp, mod_p[1], mod_p[0], w['rw'][l], t, tm)
        q_pad, kv, gates, kvt, cmp = _proj_nsa(xp, mod_p[1], mod_p[0], w['nsa'][l], w['cmp_wk'][l], w['cmp_wv'][l], bp, t, tm)
        p_pool = _proj(xp, mod_p[1], mod_p[0], w['pool'][l], t, tm)
        seqs = _rwkv_prep(p_rw, zeros_prev, rw_wts, t, tm)
        o_rw, s_pairs = _rwkv_chunk(seqs, vec(rw_gn_g), vec(rw_gn_b), bp, t)
        o_nsa = _nsa_prompt(q_pad, kvt, _permute_cmp(cmp, bp, t), gates, bp, t)
        o_pool = _pool_seq(p_pool, zeros_hist, pool_w_b[l], vec(pool_scale), t, tm, 0)
        x1, u2, route = _merge(xp, (mod_p[1], mod_p[0], mod_p[2], mod_p[4], mod_p[3]), o_rw, o_nsa, o_pool,
                               merge_wts, t, tm, alpha)
        xp = _moe(x1, u2, route, mod_p[5], w['moe_gate'][l], w['moe_up'][l], w['moe_down'][l],
                  vec(ln2_g), vec(ln2_b), t, 4 * tm, alpha)
        kv3 = kv.reshape(bp, t, 6, NSA_KV_HEADS, NSA_HEAD_DIM)
        outs['nsa_p'].append(kv3[:, :, :4])
        outs['win_p'].append(kv3[:, t - min(WINDOW, t):, 4:])
        outs['rw_p'].append(_state_from_pairs(s_pairs))
        outs['sh_p'].append(p_rw.reshape(bp, t, RW_COLS)[:, -1])
        outs['pool_p'].append(p_pool.reshape(bp, t, POOL_WIDTH)[:, t - POOL_HIST:])

        p_rw = _proj(xs, mod_s[1], mod_s[0], w['rw'][l], bs, bs)
        q_pad, kv, gates = _proj_nsa(xs, mod_s[1], mod_s[0], w['nsa'][l], w['cmp_wk'][l], w['cmp_wv'][l], bs, 1, bs)
        p_pool = _proj(xs, mod_s[1], mod_s[0], w['pool'][l], bs, bs)
        seqs = _rwkv_prep(p_rw, state_rwkv_shift[l], rw_wts, 1, bs)
        o_rw, s_pairs = _rwkv_step(seqs, _state_to_pairs(state_rwkv[l]), vec(rw_gn_g), vec(rw_gn_b))
        o_nsa = _nsa_decode(page_table, cmp_pool, cache_t, win_cache, l, q_pad, gates, kv, slopes, past_len)
        z16 = jnp.concatenate([state_pool[l].astype(F32), p_pool[:, None, :]], axis=1)
        o_pool = _pool_step(z16, pool_w_b[l], vec(pool_scale), past_len)
        x1, u2, route = _merge(xs, (mod_s[1], mod_s[0], mod_s[2], mod_s[4], mod_s[3]), o_rw, o_nsa, o_pool,
                               merge_wts, bs, bs, alpha)
        xs = _moe(x1, u2, route, mod_s[5], w['moe_gate'][l], w['moe_up'][l], w['moe_down'][l],
                  vec(ln2_g), vec(ln2_b), bs, bs, alpha)
        kv3 = kv.reshape(bs, 1, 6, NSA_KV_HEADS, NSA_HEAD_DIM)
        outs['nsa_s'].append(kv3[:, :, :4])
        outs['win_s'].append(jnp.concatenate([cache_win_kv[l].astype(F32), kv3[:, :, 4:]], axis=1)[:, -WINDOW:])
        outs['rw_s'].append(_state_from_pairs(s_pairs))
        outs['sh_s'].append(p_rw)
        outs['pool_s'].append(z16[:, 1:])

    st = lambda k: jnp.stack(outs[k])
    return (xp.reshape(bp, t, d), xs.reshape(bs, 1, d), st('nsa_p'), st('nsa_s'), st('win_p'), st('win_s'),
            st('rw_p'), st('rw_s'), st('sh_p'), st('sh_s'), st('pool_p'), st('pool_s'))
```

```python
import functools

import jax
import jax.numpy as jnp
from jax import lax
from jax.experimental import pallas as pl
from jax.experimental.pallas import tpu as pltpu

F32 = jnp.float32
BF16 = jnp.bfloat16

D_MODEL = 1024
RW_HEADS = 8
RW_HEAD_DIM = 64
RW_WIDTH = 512
RW_COLS = 1792
RW_GN_EPS = 64e-5
RW_CHUNK = 64

NSA_HEADS = 8
NSA_KV_HEADS = 2
NSA_HEAD_DIM = 64
NSA_GROUP = 4
CMP_BLOCK = 32
SEL_BLOCK = 64
TOP_N = 8
WINDOW = 512
NSA_TQ = 128
NSA_TK = 512
NEG_INF = -1e30
BELOW_NEG_INF = -3e38
FORCE_BONUS = 1e9
NSA_SLOPES = tuple(2.0 ** (-8.0 * (h + 1) / NSA_HEADS) for h in range(NSA_HEADS))

POOL_WINDOWS = (2, 4, 8, 16)
POOL_HIST = 15
POOL_WIDTH = 512

RW_END = 1792
NSA_Q_END = RW_END + 512
NSA_KV_END = NSA_Q_END + 768
NSA_END = NSA_KV_END + 24
POOL_END = NSA_END + 512

MOE_GROUPS = 4
EXPERTS_PER_GROUP = 4
EXPERT_HIDDEN = 256
ROUTE_LANE0 = 4

LN_EPS = 1e-5
LANES = 128
HALF = 64


def _cparams(sem, vmem_mb=48):
    return pltpu.CompilerParams(dimension_semantics=sem, vmem_limit_bytes=vmem_mb * 1024 * 1024)


def _dot(a, b):
    return jnp.dot(a, b, preferred_element_type=F32)


def _dot_nt(a, b):
    return lax.dot_general(a, b, (((1,), (1,)), ((), ())), preferred_element_type=F32)


def _dot_tn(a, b):
    return lax.dot_general(a, b, (((0,), (0,)), ((), ())), preferred_element_type=F32)


def _sigmoid(x):
    return 1.0 / (1.0 + jnp.exp(-x))


def _softplus(x):
    return jnp.maximum(x, 0.0) + jnp.log(1.0 + jnp.exp(-jnp.abs(x)))


def _layer_norm(h, g, b):
    mu = jnp.mean(h, axis=-1, keepdims=True)
    d = h - mu
    var = jnp.mean(d * d, axis=-1, keepdims=True)
    return d * lax.rsqrt(var + LN_EPS) * g + b


def _half_sum(x, lo):
    s_lo = jnp.sum(jnp.where(lo, x, 0.0), axis=1, keepdims=True)
    s_hi = jnp.sum(jnp.where(lo, 0.0, x), axis=1, keepdims=True)
    return jnp.where(lo, s_lo, s_hi)


def _ada_kernel(c_ref, w_ref, b_ref, o_ref):
    c = c_ref[...]
    s = (c * _sigmoid(c)).astype(BF16)
    o_ref[...] = _dot(s, w_ref[...].astype(BF16)) + b_ref[...]


def _ada_mod(c_all, ada_w, ada_b):
    L, D, D6 = ada_w.shape
    nb = c_all.shape[0]
    return pl.pallas_call(
        _ada_kernel,
        grid=(L, D6 // D),
        in_specs=[pl.BlockSpec((nb, D), lambda l, j: (0, 0)),
                  pl.BlockSpec((None, D, D), lambda l, j: (l, 0, j)),
                  pl.BlockSpec((None, 1, D), lambda l, j: (l, 0, j))],
        out_specs=pl.BlockSpec((None, nb, D), lambda l, j: (l, 0, j)),
        out_shape=jax.ShapeDtypeStruct((L, nb, D6), F32),
        compiler_params=_cparams(("parallel", "parallel")),
        name="ada_mod",
    )(c_all, ada_w, ada_b.reshape(L, 1, D6))


def _mod_spec(mod, rows_per_group, tm):
    r = mod.shape[1]
    return pl.BlockSpec((None, r, mod.shape[2]), lambda i: ((i * tm) // rows_per_group, 0, 0))


def _proj_kernel(x_ref, sc_ref, sh_ref, w_ref, o_ref):
    u = (x_ref[...] * (1.0 + sc_ref[...]) + sh_ref[...]).astype(BF16)
    o_ref[...] = _dot(u, w_ref[...])


def _proj(x, sc, sh, w, rows_per_group, tm):
    n, d = x.shape
    nc = w.shape[1]
    return pl.pallas_call(
        _proj_kernel,
        grid=(n // tm,),
        in_specs=[pl.BlockSpec((tm, d), lambda i: (i, 0)),
                  _mod_spec(sc, rows_per_group, tm), _mod_spec(sh, rows_per_group, tm),
                  pl.BlockSpec((d, nc), lambda i: (0, 0))],
        out_specs=pl.BlockSpec((tm, nc), lambda i: (i, 0)),
        out_shape=jax.ShapeDtypeStruct((n, nc), F32),
        compiler_params=_cparams(("parallel",)),
        name="proj",
    )(x, sc, sh, w)


def _proj_nsa_kernel(x_ref, sc_ref, sh_ref, w_ref, wk_ref, wv_ref, q_ref, kv_ref, gate_ref, kvt_ref=None, cmp_ref=None):
    u = (x_ref[...] * (1.0 + sc_ref[...]) + sh_ref[...]).astype(BF16)
    res = _dot(u, w_ref[...])
    q_ref[...] = res[:, :1024].astype(BF16)
    kv = res[:, 1024:1792]
    kv_ref[...] = kv
    gate_ref[...] = _sigmoid(res[:, 1792:1920])
    if kvt_ref is None:
        return
    for j in range(6):
        kvt_ref[j] = kv[:, j * LANES:(j + 1) * LANES].astype(BF16)
    nb = kv.shape[0] // CMP_BLOCK
    kc = jnp.sum(kv[:, 0:LANES].reshape(nb, CMP_BLOCK, LANES) * wk_ref[...][None], axis=1)
    vc = jnp.sum(kv[:, LANES:2 * LANES].reshape(nb, CMP_BLOCK, LANES) * wv_ref[...][None], axis=1)
    cmp_ref[:, 0:LANES] = kc
    cmp_ref[:, LANES:2 * LANES] = vc


def _proj_nsa(x, sc, sh, w, wk, wv, n_seq, t, tm):
    n, d = x.shape
    out_specs = [pl.BlockSpec((tm, 1024), lambda i: (i, 0)),
                 pl.BlockSpec((tm, 768), lambda i: (i, 0)),
                 pl.BlockSpec((tm, LANES), lambda i: (i, 0))]
    out_shape = [jax.ShapeDtypeStruct((n, 1024), BF16),
                 jax.ShapeDtypeStruct((n, 768), F32),
                 jax.ShapeDtypeStruct((n, LANES), F32)]
    if t > 1:
        tps = t // tm
        nb = tm // CMP_BLOCK
        out_specs += [pl.BlockSpec((None, 6, tm, LANES), lambda i: (i // tps, 0, i % tps, 0)),
                      pl.BlockSpec((nb, 2 * LANES), lambda i: (i, 0))]
        out_shape += [jax.ShapeDtypeStruct((n_seq, 6, t, LANES), BF16),
                      jax.ShapeDtypeStruct((n // CMP_BLOCK, 2 * LANES), F32)]
    rows_per_group = t if sc.shape[1] == 1 else n
    return pl.pallas_call(
        _proj_nsa_kernel,
        grid=(n // tm,),
        in_specs=[pl.BlockSpec((tm, d), lambda i: (i, 0)),
                  _mod_spec(sc, rows_per_group, tm), _mod_spec(sh, rows_per_group, tm),
                  pl.BlockSpec((d, 1920), lambda i: (0, 0)),
                  pl.BlockSpec((CMP_BLOCK, LANES), lambda i: (0, 0)),
                  pl.BlockSpec((CMP_BLOCK, LANES), lambda i: (0, 0))],
        out_specs=out_specs,
        out_shape=out_shape,
        compiler_params=_cparams(("parallel",)),
        name="proj_nsa",
    )(x, sc, sh, w, wk, wv)


def _rwkv_prep_body(p, prev, mu_ref, w0_ref, a0_ref, wwa_ref, gup_ref, kkw_ref, kaw_ref, rk_ref, outs):
    r_ref, lw_ref, k_ref, v_ref, kk_ref, kka_ref, g_ref, bonus_ref = outs
    xs = p + (prev - p) * mu_ref[...]
    r = xs[:, 0:512]
    xk = xs[:, 512:1024]
    v = xs[:, 1024:1536]
    t12 = xs[:, 1536:1664]
    gd = xs[:, 1664:1792]
    lane = lax.broadcasted_iota(jnp.int32, (1, LANES), 1)
    lo = lane < HALF
    z = jnp.where(lo, jnp.tanh(t12), t12).astype(BF16)
    dwa = _dot(z, wwa_ref[...])
    w_log = -_softplus(-(w0_ref[...] + dwa[:, :512])) - 0.5
    a = _sigmoid(a0_ref[...] + dwa[:, 512:])
    g_ref[...] = _dot(_sigmoid(gd).astype(BF16), gup_ref[...])
    kmod = xk * (1.0 + (a - 1.0) * kaw_ref[...])
    kkr = xk * kkw_ref[...]
    rkr = r * kmod * rk_ref[...]
    for m in range(4):
        sl = slice(m * LANES, (m + 1) * LANES)
        x = kkr[:, sl]
        nrm = jnp.sqrt(_half_sum(x * x, lo))
        kk = x / jnp.maximum(nrm, 1e-12)
        kk_ref[:, sl] = kk
        kka_ref[:, sl] = kk * a[:, sl]
        bonus_ref[:, sl] = _half_sum(rkr[:, sl], lo) * v[:, sl]
    r_ref[...] = r
    lw_ref[...] = -jnp.exp(w_log)
    k_ref[...] = kmod
    v_ref[...] = v


def _rwkv_prep_seq_kernel(p_ref, pprev_ref, prow_ref, *rest, seq_tiles):
    i = pl.program_id(0)
    p = p_ref[...]
    first = (i % seq_tiles) == 0
    prev_row = jnp.where(first, prow_ref[...], pprev_ref[7:8, :])
    rolled = pltpu.roll(p, 1, 0)
    rowid = lax.broadcasted_iota(jnp.int32, (p.shape[0], 1), 0)
    prev = jnp.where(rowid == 0, prev_row, rolled)
    _rwkv_prep_body(p, prev, *rest[:8], rest[8:])


def _rwkv_prep_step_kernel(p_ref, prev_ref, *rest):
    _rwkv_prep_body(p_ref[...], prev_ref[...], *rest[:8], rest[8:])


def _rwkv_prep(p_rw, prev_rows, wts, t, tm):
    n = p_rw.shape[0]
    vec = lambda c: pl.BlockSpec((1, c), lambda i: (0, 0))
    w_specs = [vec(RW_COLS), vec(512), vec(512), pl.BlockSpec((LANES, 1024), lambda i: (0, 0)),
               pl.BlockSpec((LANES, 512), lambda i: (0, 0)), vec(512), vec(512), vec(512)]
    out_specs = [pl.BlockSpec((tm, 512), lambda i: (i, 0))] * 8
    out_shape = [jax.ShapeDtypeStruct((n, 512), F32)] * 8
    if t == 1:
        kern = _rwkv_prep_step_kernel
        in_specs = [pl.BlockSpec((tm, RW_COLS), lambda i: (i, 0)), pl.BlockSpec((tm, RW_COLS), lambda i: (i, 0))]
        args = (p_rw, prev_rows)
    else:
        tps = t // tm
        kern = functools.partial(_rwkv_prep_seq_kernel, seq_tiles=tps)
        in_specs = [pl.BlockSpec((tm, RW_COLS), lambda i: (i, 0)),
                    pl.BlockSpec((8, RW_COLS), lambda i: (jnp.maximum(i * (tm // 8) - 1, 0), 0)),
                    pl.BlockSpec((None, 1, RW_COLS), lambda i: (i // tps, 0, 0))]
        args = (p_rw, p_rw, prev_rows[:, None, :])
    return pl.pallas_call(
        kern, grid=(n // tm,), in_specs=in_specs + w_specs, out_specs=out_specs, out_shape=out_shape,
        compiler_params=_cparams(("parallel",)), name="rwkv_prep",
    )(*args, *wts)


def _split3(x):
    x1 = x.astype(BF16)
    r1 = x - x1.astype(F32)
    x2 = r1.astype(BF16)
    x3 = (r1 - x2.astype(F32)).astype(BF16)
    return x1, x2, x3


def _rwkv_chunk_kernel(r_ref, lw_ref, k_ref, v_ref, kk_ref, kka_ref, g_ref, bonus_ref, gng_ref, gnb_ref,
                       o_ref, sout_ref, s_ref):
    c = pl.program_id(1)
    C = RW_CHUNK

    @pl.when(c == 0)
    def _():
        s_ref[...] = jnp.zeros(s_ref.shape, F32)

    ns = lw_ref.shape[0]
    flat = lambda ref: ref[...].reshape(ns * C, ref.shape[-1])
    lw = flat(lw_ref)
    row = lax.broadcasted_iota(jnp.int32, (ns * C, ns * C), 0)
    col = lax.broadcasted_iota(jnp.int32, (ns * C, ns * C), 1)
    tri = ((row >= col) & (row // C == col // C)).astype(BF16)
    l1, l2, l3 = _split3(lw)
    cl = _dot(tri, l1) + _dot(tri, l2) + _dot(tri, l3)
    cl_last = jnp.concatenate([jnp.broadcast_to(cl[(q + 1) * C - 1:(q + 1) * C, :], (C, cl.shape[1]))
                               for q in range(ns)], axis=0)
    g_in = jnp.exp(cl)
    g_ex = jnp.exp(cl - lw)
    g_inv = jnp.exp(-cl)
    g_rem = jnp.exp(cl_last - cl)
    gc = jnp.exp(cl_last)
    k = flat(k_ref)
    kka = flat(kka_ref)
    qk_all = flat(kk_ref) * g_ex
    r_all = flat(r_ref) * g_in
    kt_all = k * g_inv
    at_all = kka * g_inv
    kd_all = k * g_rem
    ad_all = kka * g_rem
    v_all = flat(v_ref)

    lane = lax.broadcasted_iota(jnp.int32, (1, LANES), 1)
    lo = lane < HALF

    def st(x):
        return jnp.concatenate([jnp.where(lo, x, 0.0), jnp.where(lo, 0.0, x)], axis=0)

    r2 = lax.broadcasted_iota(jnp.int32, (4 * C, 4 * C), 0)
    c2 = lax.broadcasted_iota(jnp.int32, (4 * C, 4 * C), 1)
    rt = r2 % C
    ct = c2 % C
    tmask = (ct < rt) | ((r2 >= 2 * C) & (ct == rt))
    ri = lax.broadcasted_iota(jnp.int32, (2 * C, 2 * C), 0)
    ci = lax.broadcasted_iota(jnp.int32, (2 * C, 2 * C), 1)
    eye = (ri == ci).astype(F32)

    units = [(q, m) for q in range(ns) for m in range(4)]
    pairs = range(len(units))
    cut = lambda x, u: x[units[u][0] * C:(units[u][0] + 1) * C, units[u][1] * LANES:(units[u][1] + 1) * LANES]
    sls = [slice(m * LANES, (m + 1) * LANES) for _, m in units]
    bf = lambda x: x.astype(BF16)
    qk_st = [st(cut(qk_all, u)) for u in pairs]
    r_st = [st(cut(r_all, u)) for u in pairs]
    v_st = [st(cut(v_all, u)) for u in pairs]
    ad_st = [st(cut(ad_all, u)) for u in pairs]
    kd_st = [st(cut(kd_all, u)) for u in pairs]
    xm = [jnp.where(tmask, _dot_nt(bf(jnp.concatenate([qk_st[m], r_st[m]], axis=0)),
                                   bf(jnp.concatenate([st(cut(at_all, m)), st(cut(kt_all, m))], axis=0))), 0.0)
          for m in pairs]
    av = [_dot(bf(xm[m][:, 2 * C:]), bf(v_st[m])) for m in pairs]
    p = [xm[m][:2 * C, :2 * C] for m in pairs]
    tm_ = [eye - p[m] for m in pairs]
    for _ in range(5):
        p = [_dot(bf(p[m]), bf(p[m])) for m in pairs]
        tm_ = [_dot(bf(tm_[m]), bf(eye + p[m])) for m in pairs]
    wu = [_dot(bf(tm_[m]), bf(jnp.concatenate([qk_st[m], av[m][:2 * C]], axis=1))) for m in pairs]
    ry = [_dot(bf(xm[m][2 * C:, :2 * C]), bf(wu[m])) for m in pairs]
    rq = [r_st[m] - ry[m][:, :LANES] for m in pairs]
    y0 = [av[m][2 * C:] - ry[m][:, LANES:] for m in pairs]
    gt = [eye * cut(gc, m)[0:1] - _dot_tn(bf(wu[m][:, :LANES]), bf(ad_st[m])) for m in pairs]
    ht = [_dot_tn(bf(jnp.concatenate([v_st[m], wu[m][:, LANES:]], axis=0)),
                  bf(jnp.concatenate([kd_st[m], -ad_st[m]], axis=0))) for m in pairs]
    sb = [bf(s_ref[q, m]) for q, m in units]
    y_st = [_dot_nt(bf(rq[m]), sb[m]) + y0[m] for m in pairs]
    s_new = [_dot(sb[m], bf(gt[m])) + ht[m] for m in pairs]
    for u, (q, m) in enumerate(units):
        sl = sls[u]
        s_ref[q, m] = s_new[u]
        sout_ref[q, m] = s_new[u][:C] + s_new[u][C:]
        y = y_st[u][:C] + y_st[u][C:]
        o_ref[q, :, sl] = _rwkv_out(y, lo, gng_ref[:, sl], gnb_ref[:, sl], bonus_ref[q, :, sl], g_ref[q, :, sl])


def _rwkv_out(y, lo, gn_g, gn_b, bonus, g):
    mu = _half_sum(y, lo) * (1.0 / RW_HEAD_DIM)
    d = y - mu
    var = _half_sum(d * d, lo) * (1.0 / RW_HEAD_DIM)
    yn = d * lax.rsqrt(var + RW_GN_EPS) * gn_g + gn_b
    return ((yn + bonus) * g).astype(BF16)


def _rwkv_chunk(seqs, gn_g, gn_b, n_seq, t):
    C = RW_CHUNK
    nch = t // C
    ns = 4 if n_seq % 4 == 0 else (2 if n_seq % 2 == 0 else 1)
    row_spec = pl.BlockSpec((ns, C, 512), lambda b, c: (b, c, 0))
    vec = pl.BlockSpec((1, 512), lambda b, c: (0, 0))
    o, s = pl.pallas_call(
        _rwkv_chunk_kernel,
        grid=(n_seq // ns, nch),
        in_specs=[row_spec] * 8 + [vec, vec],
        out_specs=[row_spec, pl.BlockSpec((ns, 4, RW_HEAD_DIM, LANES), lambda b, c: (b, 0, 0, 0))],
        out_shape=[jax.ShapeDtypeStruct((n_seq, t, 512), BF16),
                   jax.ShapeDtypeStruct((n_seq, 4, RW_HEAD_DIM, LANES), F32)],
        scratch_shapes=[pltpu.VMEM((ns, 4, LANES, LANES), F32)],
        compiler_params=_cparams(("parallel", "arbitrary")),
        name="rwkv_chunk",
    )(*[a.reshape(n_seq, t, 512) for a in seqs], gn_g, gn_b)
    return o.reshape(n_seq * t, 512), s


def _rwkv_step_kernel(r_ref, lw_ref, k_ref, v_ref, kk_ref, kka_ref, g_ref, bonus_ref, s_ref, gng_ref, gnb_ref,
                      o_ref, sout_ref):
    lane = lax.broadcasted_iota(jnp.int32, (1, LANES), 1)
    lo = lane < HALF
    rowi = lax.broadcasted_iota(jnp.int32, (RW_HEAD_DIM, LANES), 0)
    diag = (rowi == lax.broadcasted_iota(jnp.int32, (RW_HEAD_DIM, LANES), 1) % HALF).astype(F32)
    for m in range(4):
        sl = slice(m * LANES, (m + 1) * LANES)
        s = s_ref[m]
        sk = _half_sum(s * kk_ref[:, sl], lo)
        v_col = _half_sum(diag * v_ref[:, sl], lo)
        s_new = s * jnp.exp(lw_ref[:, sl]) - sk * kka_ref[:, sl] + v_col * k_ref[:, sl]
        sout_ref[m] = s_new
        y_col = _half_sum(s_new * r_ref[:, sl], lo)
        y = jnp.sum(diag * y_col, axis=0, keepdims=True)
        o_ref[:, sl] = _rwkv_out(y, lo, gng_ref[:, sl], gnb_ref[:, sl], bonus_ref[:, sl], g_ref[:, sl])


def _rwkv_step(seqs, s_pairs, gn_g, gn_b):
    bs = s_pairs.shape[0]
    row_spec = pl.BlockSpec((None, 1, 512), lambda b: (b, 0, 0))
    st_spec = pl.BlockSpec((None, 4, RW_HEAD_DIM, LANES), lambda b: (b, 0, 0, 0))
    vec = pl.BlockSpec((1, 512), lambda b: (0, 0))
    o, s = pl.pallas_call(
        _rwkv_step_kernel,
        grid=(bs,),
        in_specs=[row_spec] * 8 + [st_spec, vec, vec],
        out_specs=[row_spec, st_spec],
        out_shape=[jax.ShapeDtypeStruct((bs, 1, 512), BF16),
                   jax.ShapeDtypeStruct((bs, 4, RW_HEAD_DIM, LANES), F32)],
        compiler_params=_cparams(("parallel",)),
        name="rwkv_step",
    )(*[a[:, None, :] for a in seqs], s_pairs, gn_g, gn_b)
    return o[:, 0], s


def _state_to_pairs(s):
    b = s.shape[0]
    return jnp.transpose(s.reshape(b, 4, 2, 64, 64).astype(F32), (0, 1, 3, 2, 4)).reshape(b, 4, 64, LANES)


def _state_from_pairs(sp):
    b = sp.shape[0]
    return jnp.transpose(sp.reshape(b, 4, 64, 2, 64), (0, 1, 3, 2, 4)).reshape(b, 8, 64, 64)


def _nsa_prompt_kernel(q_ref, kvt_ref, cmp_ref, gate_ref, o_ref, m_ref, acc_ref, *, k_top, tk):
    qi = pl.program_id(1)
    tq = NSA_TQ
    q0 = qi * tq
    lane = lax.broadcasted_iota(jnp.int32, (1, LANES), 1)
    lo = lane < HALF
    tok = q0 + lax.broadcasted_iota(jnp.int32, (tq, 1), 0)
    cmpv = cmp_ref[...]
    kc = cmpv[:, :LANES].astype(BF16)
    vc = cmpv[:, LANES:].astype(BF16)
    gates = gate_ref[...]
    blk_c = 2 * (lane % HALF) + lane // HALF
    dist_c = tok - (blk_c * CMP_BLOCK + (CMP_BLOCK - 1))
    mask_c = dist_c >= 0
    dist_cf = dist_c.astype(F32)
    cur = tok // SEL_BLOCK
    valid = (lane * SEL_BLOCK <= tok) & lo
    forced = (lane == 0) | (lane == cur) | (lane == cur - 1)
    in_grp = (lo, jnp.logical_not(lo))
    den_lane = (HALF, 0)
    hrows = lambda h: slice(h * tq, (h + 1) * tq)
    grows = lambda g: slice(g * NSA_GROUP * tq, (g + 1) * NSA_GROUP * tq)
    q_all = jnp.concatenate([q_ref[:, h * LANES:(h + 1) * LANES] for h in range(NSA_HEADS)], axis=0)

    s = _dot_nt(q_all, kc)
    ps = []
    imp = [jnp.zeros((tq, LANES), F32) for _ in range(NSA_KV_HEADS)]
    for h in range(NSA_HEADS):
        sr = jnp.where(mask_c, s[hrows(h)] - NSA_SLOPES[h] * dist_cf, NEG_INF)
        mx = jnp.max(sr, axis=1, keepdims=True)
        e = jnp.where(mask_c, jnp.exp(sr - mx), 0.0)
        p = e / jnp.maximum(jnp.sum(e, axis=1, keepdims=True), 1e-30)
        ps.append(p.astype(BF16))
        imp[h // NSA_GROUP] = imp[h // NSA_GROUP] + p
    o_c = _dot(jnp.concatenate(ps, axis=0), vc)

    wk = WINDOW + tq
    kstart = pl.multiple_of(jnp.maximum(q0 - WINDOW, 0), tq)
    kt_w = kvt_ref[4, pl.ds(kstart, wk), :]
    vt_w = kvt_ref[5, pl.ds(kstart, wk), :]
    s_w = _dot_nt(q_all, kt_w)
    dist_w = tok - (kstart + lax.broadcasted_iota(jnp.int32, (1, wk), 1))
    dist_wf = dist_w.astype(F32)
    bias_w = jnp.where((dist_w >= 0) & (dist_w <= WINDOW), 0.0, NEG_INF)
    o_w = []
    for g in range(NSA_KV_HEADS):
        es = []
        for r in range(NSA_GROUP):
            h = NSA_GROUP * g + r
            x = s_w[hrows(h)] - NSA_SLOPES[h] * dist_wf + bias_w
            e = jnp.exp(x - jnp.max(x, axis=1, keepdims=True))
            es.append((e / jnp.sum(e, axis=1, keepdims=True)).astype(BF16))
        o_w.append(_dot(jnp.concatenate(es, axis=0), vt_w))

    score, sel = [], []
    for g in range(NSA_KV_HEADS):
        imp_sel = imp[g] + pltpu.roll(imp[g], HALF, 1)
        sc = jnp.where(valid, imp_sel + jnp.where(forced, FORCE_BONUS, 0.0), NEG_INF)
        score.append(jnp.where(lo, sc, BELOW_NEG_INF))
        sel.append(jnp.zeros((tq, LANES), F32))
    for _ in range(k_top):
        for g in range(NSA_KV_HEADS):
            mx = jnp.max(score[g], axis=1, keepdims=True)
            idx = jnp.min(jnp.where(score[g] == mx, lane, 4 * LANES), axis=1, keepdims=True)
            hit = lane == idx
            sel[g] = jnp.where(hit, 1.0, sel[g])
            score[g] = jnp.where(hit, BELOW_NEG_INF, score[g])
    sel_b = [x.astype(BF16) for x in sel]

    n_sub = tk // LANES
    blk_row = lax.broadcasted_iota(jnp.int32, (LANES, tk), 0)
    key_blk = lax.broadcasted_iota(jnp.int32, (LANES, tk), 1) // SEL_BLOCK
    col_k = lax.broadcasted_iota(jnp.int32, (1, tk), 1)
    m_ref[...] = jnp.full(m_ref.shape, NEG_INF, F32)
    acc_ref[...] = jnp.zeros(acc_ref.shape, F32)

    def sel_tile(j, carry):
        start = pl.multiple_of(j * tk, tk)
        kt = kvt_ref[2, pl.ds(start, tk), :]
        vt = kvt_ref[3, pl.ds(start, tk), :]
        s = _dot_nt(q_all, kt)
        dist = tok - (start + col_k)
        dist_f = dist.astype(F32)
        causal = jnp.where(dist >= 0, 0.0, NEG_INF)
        expand = (blk_row == (tk // SEL_BLOCK) * j + key_blk).astype(BF16)
        for g in range(NSA_KV_HEADS):
            bias = (_dot(sel_b[g], expand) - 1.0) * (-NEG_INF) + causal
            es = []
            for r in range(NSA_GROUP):
                h = NSA_GROUP * g + r
                x = s[hrows(h)] - NSA_SLOPES[h] * dist_f + bias
                xm = x[:, 0:LANES]
                for i in range(1, n_sub):
                    xm = jnp.maximum(xm, x[:, i * LANES:(i + 1) * LANES])
                m_old = m_ref[hrows(h)]
                m_new = jnp.maximum(m_old, jnp.max(xm, axis=1, keepdims=True))
                m_ref[hrows(h)] = m_new
                acc_ref[hrows(h)] = jnp.exp(m_old - m_new) * acc_ref[hrows(h)]
                es.append(jnp.exp(x - jnp.concatenate([m_new] * n_sub, axis=1)).astype(BF16))
            vg = jnp.where(in_grp[g], vt, jnp.ones_like(vt))
            acc_ref[grows(g)] += _dot(jnp.concatenate(es, axis=0), vg)
        return carry

    lax.fori_loop(0, (q0 + tq - 1) // tk + 1, sel_tile, 0)
    acc = acc_ref[...]
    o_s = [acc[grows(g)] / jnp.maximum(acc[grows(g), den_lane[g]:den_lane[g] + 1], 1e-30) for g in range(NSA_KV_HEADS)]

    for h in range(NSA_HEADS):
        g, r = divmod(h, NSA_GROUP)
        rows = slice(r * tq, (r + 1) * tq)
        o = (gates[:, 3 * h:3 * h + 1] * o_c[hrows(h)] + gates[:, 3 * h + 1:3 * h + 2] * o_s[g][rows]
             + gates[:, 3 * h + 2:3 * h + 3] * o_w[g][rows])
        o_ref[:, h * LANES:(h + 1) * LANES] = jnp.where(in_grp[g], o, 0.0).astype(BF16)


def _nsa_prompt(q_pad, kvt, cmp_perm, gates, n_seq, t):
    tq = NSA_TQ
    nq = t // tq
    k_top = min(TOP_N, t // SEL_BLOCK)
    assert k_top >= 3 and t >= WINDOW + tq and t % NSA_TK == 0
    return pl.pallas_call(
        functools.partial(_nsa_prompt_kernel, k_top=k_top, tk=NSA_TK),
        grid=(n_seq, nq),
        in_specs=[pl.BlockSpec((tq, 1024), lambda b, i: (b * nq + i, 0)),
                  pl.BlockSpec((None, 6, t, LANES), lambda b, i: (b, 0, 0, 0)),
                  pl.BlockSpec((None, LANES, 2 * LANES), lambda b, i: (b, 0, 0)),
                  pl.BlockSpec((tq, LANES), lambda b, i: (b * nq + i, 0))],
        out_specs=pl.BlockSpec((tq, 1024), lambda b, i: (b * nq + i, 0)),
        out_shape=jax.ShapeDtypeStruct((n_seq * t, 1024), BF16),
        scratch_shapes=[pltpu.VMEM((NSA_HEADS * tq, LANES), F32), pltpu.VMEM((NSA_HEADS * tq, LANES), F32)],
        compiler_params=_cparams(("parallel", "parallel")),
        name="nsa_prompt",
    )(q_pad, kvt, cmp_perm, gates)


def _permute_cmp(cmp, n_seq, t):
    nc = t // CMP_BLOCK
    c = cmp.reshape(n_seq, nc // 2, 2, 2 * LANES)
    c = jnp.pad(c, ((0, 0), (0, HALF - nc // 2), (0, 0), (0, 0)))
    return jnp.transpose(c, (0, 2, 1, 3)).reshape(n_seq, LANES, 2 * LANES)


def _compress_pool_kernel(x_ref, w_ref, o_ref):
    pb = x_ref.shape[0]
    page = x_ref.shape[-1]
    nb = page // CMP_BLOCK
    blk = lax.broadcasted_iota(jnp.int32, (8, page), 0)
    seg = (lax.broadcasted_iota(jnp.int32, (8, page), 1) // CMP_BLOCK == blk).astype(BF16)
    items = [(p, j) for p in range(pb) for j in range(2)]
    parts = [_split3(x_ref[p, j].reshape(LANES, page) * w_ref[j]) for p, j in items]
    sums = [_dot_nt(seg, k1) + _dot_nt(seg, k2) + _dot_nt(seg, k3) for k1, k2, k3 in parts]
    for (p, j), res in zip(items, sums):
        o_ref[p, j * nb:(j + 1) * nb, :] = res[0:nb]


def _compress_pool(cache_t, w_t, pb):
    L, n_pool = cache_t.shape[:2]
    page = cache_t.shape[-1]
    nb = page // CMP_BLOCK
    return pl.pallas_call(
        _compress_pool_kernel,
        grid=(L, n_pool // pb),
        in_specs=[pl.BlockSpec((None, pb, 2, NSA_KV_HEADS, NSA_HEAD_DIM, page), lambda l, i: (l, i, 0, 0, 0, 0)),
                  pl.BlockSpec((None, 2, LANES, page), lambda l, i: (l, 0, 0, 0))],
        out_specs=pl.BlockSpec((None, pb, 2 * nb, LANES), lambda l, i: (l, i, 0, 0)),
        out_shape=jax.ShapeDtypeStruct((L, n_pool, 2 * nb, LANES), F32),
        compiler_params=_cparams(("parallel", "parallel")),
        name="compress_pool",
    )(cache_t, w_t)


def _nsa_dec_cmp_kernel(pt_ref, pool_ref, q_ref, gate_ref, slope_ref, win_ref, kvn_ref, part_ref, idx_ref, gath_ref,
                        *, n_pages, past_len, k_top):
    b = pl.program_id(0)
    t = past_len
    for p in range(n_pages):
        tile = pool_ref[pt_ref[b, p]]
        for r in range(8):
            gath_ref[r, p:p + 1, :] = tile[r:r + 1, :]
    x = [gath_ref[r] for r in range(8)]
    q = q_ref[...]
    slope = slope_ref[:, 0:1]
    gates = gate_ref[...]
    lane = lax.broadcasted_iota(jnp.int32, (1, LANES), 1)
    lane_p = lax.broadcasted_iota(jnp.int32, (1, 2 * n_pages), 1)
    page_of = lane_p % n_pages
    ss, masks = [], []
    for pair in range(2):
        kmat = jnp.concatenate([x[2 * pair], x[2 * pair + 1]], axis=0).astype(BF16)
        blk = page_of * 4 + 2 * pair + lane_p // n_pages
        dist = t - (blk * CMP_BLOCK + (CMP_BLOCK - 1))
        mask = dist >= 0
        s = _dot_nt(q, kmat) - slope * dist.astype(F32)
        ss.append(jnp.where(mask, s, NEG_INF))
        masks.append(mask)
    mx = jnp.maximum(jnp.max(ss[0], axis=1, keepdims=True), jnp.max(ss[1], axis=1, keepdims=True))
    es = [jnp.where(masks[i], jnp.exp(ss[i] - mx), 0.0) for i in range(2)]
    den = jnp.maximum(jnp.sum(es[0], axis=1, keepdims=True) + jnp.sum(es[1], axis=1, keepdims=True), 1e-30)
    ps = [e / den for e in es]
    o_c = jnp.zeros((NSA_HEADS, LANES), F32)
    for pair in range(2):
        vmat = jnp.concatenate([x[4 + 2 * pair], x[5 + 2 * pair]], axis=0).astype(BF16)
        o_c = o_c + _dot(ps[pair].astype(BF16), vmat)
    n_sel_blk = lane_p // n_pages + 2 * page_of
    cur = t // SEL_BLOCK
    forced = (n_sel_blk == 0) | (n_sel_blk == cur) | (n_sel_blk == cur - 1)
    valid = n_sel_blk * SEL_BLOCK <= t
    idx_rows = []
    for g in range(NSA_KV_HEADS):
        imp_a = jnp.sum(ps[0][NSA_GROUP * g:NSA_GROUP * (g + 1)], axis=0, keepdims=True)
        imp_b = jnp.sum(ps[1][NSA_GROUP * g:NSA_GROUP * (g + 1)], axis=0, keepdims=True)
        ev = imp_a + pltpu.roll(imp_a, n_pages, 1)
        od = imp_b + pltpu.roll(imp_b, n_pages, 1)
        imp = jnp.where(lane_p < n_pages, ev, od)
        score = jnp.where(valid, imp + jnp.where(forced, FORCE_BONUS, 0.0), NEG_INF)
        chosen = jnp.where(lane == 0, cur, 0)
        for it in range(1, k_top):
            mxs = jnp.max(score, axis=1, keepdims=True)
            pick = jnp.min(jnp.where(score == mxs, n_sel_blk, 1 << 30), axis=1, keepdims=True)
            score = jnp.where(n_sel_blk == pick, BELOW_NEG_INF, score)
            chosen = jnp.where(lane == it, pick, chosen)
        idx_rows.append(chosen)
    rowi = lax.broadcasted_iota(jnp.int32, (8, LANES), 0)
    idx_ref[...] = jnp.where(rowi == 0, idx_rows[0], jnp.where(rowi == 1, idx_rows[1], 0))
    wl = win_ref.shape[-1]
    kw = win_ref[0].reshape(LANES, wl).astype(BF16)
    vw = win_ref[1].reshape(LANES, wl).astype(BF16)
    col = lax.broadcasted_iota(jnp.int32, (1, wl), 1)
    dist_w = wl - col
    mask_w = (t - dist_w >= 0) & (dist_w <= WINDOW)
    s_w = jnp.where(mask_w, _dot(q, kw) - slope * dist_w.astype(F32), NEG_INF)
    kvn = kvn_ref[...]
    qf = q.astype(F32)
    s_n = jnp.sum(qf * kvn[4:5, :].astype(BF16).astype(F32), axis=1, keepdims=True)
    mw = jnp.maximum(jnp.max(s_w, axis=1, keepdims=True), s_n)
    e_w = jnp.where(mask_w, jnp.exp(s_w - mw), 0.0)
    e_n = jnp.exp(s_n - mw)
    den_w = jnp.maximum(jnp.sum(e_w, axis=1, keepdims=True) + e_n, 1e-30)
    o_w = (_dot_nt((e_w / den_w).astype(BF16), vw)
           + (e_n / den_w).astype(BF16).astype(F32) * kvn[5:6, :].astype(BF16).astype(F32))
    part_ref[...] = gates[:, 0:1] * o_c + gates[:, 2:3] * o_w


def _nsa_dec_sel_kernel(pt_ref, sel_ref, blk_ref, q_ref, gate_ref, slope_ref, kvn_ref, part_ref, o_ref,
                        stage_ref, *, past_len, k_top, cur_selected):
    b = pl.program_id(0)
    j = pl.program_id(1)
    t = past_len
    stage_ref[j] = blk_ref[...]

    @pl.when(j == pl.num_programs(1) - 1)
    def _():
        q = q_ref[...]
        slope = slope_ref[:, 0:1]
        kvn = kvn_ref[...]
        page = blk_ref.shape[-1]
        halves = page // SEL_BLOCK
        col = lax.broadcasted_iota(jnp.int32, (1, page), 1)
        rowh = lax.broadcasted_iota(jnp.int32, (NSA_HEADS, 1), 0)
        lane = lax.broadcasted_iota(jnp.int32, (1, LANES), 1)
        s_new = jnp.sum(q.astype(F32) * kvn[2:3, :].astype(BF16).astype(F32), axis=1, keepdims=True)
        v_new = kvn[3:4, :].astype(BF16).astype(F32)
        o_groups = []
        for g in range(NSA_KV_HEADS):
            ss, masks, vts = [], [], []
            has_cur = cur_selected
            for i in range(k_top):
                jj = g * k_top + i
                n = sel_ref[b, jj]
                kt = stage_ref[jj, 0].reshape(LANES, page).astype(BF16)
                vts.append(stage_ref[jj, 1].reshape(LANES, page).astype(BF16))
                pos = (n // halves) * page + col
                mask = (col // SEL_BLOCK == n % halves) & (pos < t)
                ss.append(jnp.where(mask, _dot(q, kt) - slope * (t - pos).astype(F32), NEG_INF))
                masks.append(mask)
                has_cur = jnp.logical_or(has_cur, n * SEL_BLOCK + SEL_BLOCK > t)
            s_n = jnp.where(has_cur, s_new, NEG_INF)
            mx = s_n
            for s in ss:
                mx = jnp.maximum(mx, jnp.max(s, axis=1, keepdims=True))
            es = [jnp.where(masks[i], jnp.exp(ss[i] - mx), 0.0) for i in range(k_top)]
            e_n = jnp.where(has_cur, jnp.exp(s_n - mx), 0.0)
            den = e_n
            for e in es:
                den = den + jnp.sum(e, axis=1, keepdims=True)
            den = jnp.maximum(den, 1e-30)
            o = (e_n / den).astype(BF16).astype(F32) * v_new
            for i in range(k_top):
                o = o + _dot_nt((es[i] / den).astype(BF16), vts[i])
            o_groups.append(o)
        o_s = jnp.where(rowh // NSA_GROUP == 0, o_groups[0], o_groups[1])
        o = part_ref[...] + gate_ref[:, 1:2] * o_s
        in_half = (lane // HALF) == (rowh // NSA_GROUP)
        o_ref[...] = jnp.where(in_half, o, 0.0).astype(BF16)


def _nsa_decode(page_table, cmp_pool, cache_t, win_t, l, q_pad, gates, kv_new, slopes, past_len):
    bs, n_pages = page_table.shape
    n_pool, page = cache_t.shape[1], cache_t.shape[-1]
    k_top = min(TOP_N, past_len // SEL_BLOCK + 1)
    q3 = q_pad.reshape(bs, NSA_HEADS, LANES)
    g3 = jnp.pad(gates[:, :3 * NSA_HEADS].reshape(bs, NSA_HEADS, 3), ((0, 0), (0, 0), (0, LANES - 3)))
    kvn = kv_new.reshape(bs, 6, LANES)
    wl = win_t.shape[-1]
    part, idx = pl.pallas_call(
        functools.partial(_nsa_dec_cmp_kernel, n_pages=n_pages, past_len=past_len, k_top=k_top),
        grid_spec=pltpu.PrefetchScalarGridSpec(
            num_scalar_prefetch=1, grid=(bs,),
            in_specs=[pl.BlockSpec((None, n_pool, 8, LANES), lambda b, pt: (l, 0, 0, 0)),
                      pl.BlockSpec((None, NSA_HEADS, LANES), lambda b, pt: (b, 0, 0)),
                      pl.BlockSpec((None, NSA_HEADS, LANES), lambda b, pt: (b, 0, 0)),
                      pl.BlockSpec((NSA_HEADS, LANES), lambda b, pt: (0, 0)),
                      pl.BlockSpec((None, None, 2, NSA_KV_HEADS, NSA_HEAD_DIM, wl), lambda b, pt: (l, b, 0, 0, 0, 0)),
                      pl.BlockSpec((None, 6, LANES), lambda b, pt: (b, 0, 0))],
            out_specs=[pl.BlockSpec((None, NSA_HEADS, LANES), lambda b, pt: (b, 0, 0)),
                       pl.BlockSpec((None, 8, LANES), lambda b, pt: (b, 0, 0))],
            scratch_shapes=[pltpu.VMEM((8, n_pages, LANES), F32)]),
        out_shape=[jax.ShapeDtypeStruct((bs, NSA_HEADS, LANES), F32),
                   jax.ShapeDtypeStruct((bs, 8, LANES), jnp.int32)],
        compiler_params=_cparams(("arbitrary",)),
        name="nsa_dec_cmp",
    )(page_table, cmp_pool, q3, g3, slopes, win_t, kvn)
    k_pg = k_top - 1
    sel = idx[:, :NSA_KV_HEADS, 1:k_top].reshape(bs, NSA_KV_HEADS * k_pg)
    halves = page // SEL_BLOCK

    def blk_map(b, j, pt, sl):
        pg = pt[b, jnp.minimum(sl[b, j] // halves, n_pages - 1)]
        return (l, pg, 1, 0, 0, 0)

    out = pl.pallas_call(
        functools.partial(_nsa_dec_sel_kernel, past_len=past_len, k_top=k_pg, cur_selected=True),
        grid_spec=pltpu.PrefetchScalarGridSpec(
            num_scalar_prefetch=2, grid=(bs, NSA_KV_HEADS * k_pg),
            in_specs=[pl.BlockSpec((None, None, 2, NSA_KV_HEADS, NSA_HEAD_DIM, page), blk_map),
                      pl.BlockSpec((None, NSA_HEADS, LANES), lambda b, j, pt, sl: (b, 0, 0)),
                      pl.BlockSpec((None, NSA_HEADS, LANES), lambda b, j, pt, sl: (b, 0, 0)),
                      pl.BlockSpec((NSA_HEADS, LANES), lambda b, j, pt, sl: (0, 0)),
                      pl.BlockSpec((None, 6, LANES), lambda b, j, pt, sl: (b, 0, 0)),
                      pl.BlockSpec((None, NSA_HEADS, LANES), lambda b, j, pt, sl: (b, 0, 0))],
            out_specs=pl.BlockSpec((None, NSA_HEADS, LANES), lambda b, j, pt, sl: (b, 0, 0)),
            scratch_shapes=[pltpu.VMEM((NSA_KV_HEADS * k_pg, 2, NSA_KV_HEADS, NSA_HEAD_DIM, page), F32)]),
        out_shape=jax.ShapeDtypeStruct((bs, NSA_HEADS, LANES), BF16),
        compiler_params=_cparams(("arbitrary", "arbitrary")),
        name="nsa_dec_sel",
    )(page_table, sel, cache_t, q3, g3, slopes, kvn, part)
    return out.reshape(bs, NSA_HEADS * LANES)


def _pool_tail(sums, cur, cnts, pw_ref, scale_ref, o_ref):
    for gi in range(len(POOL_WINDOWS)):
        sl = slice(gi * LANES, (gi + 1) * LANES)
        d = sums[gi] / cnts[gi] - cur[:, sl]
        y = _dot(d.astype(BF16), pw_ref[gi])
        o_ref[:, sl] = (y * scale_ref[:, sl]).astype(BF16)


def _pool_seq_kernel(p_ref, pprev_ref, hist_ref, pw_ref, scale_ref, o_ref, zz_ref, *, seq_tiles, tm, start_pos):
    i = pl.program_id(0)
    first = (i % seq_tiles) == 0
    cur = p_ref[...]
    zz_ref[0:16, :] = jnp.where(first, hist_ref[...], pprev_ref[...])
    zz_ref[16:16 + tm, :] = cur
    pos = start_pos + (i % seq_tiles) * tm + lax.broadcasted_iota(jnp.int32, (tm, 1), 0)
    sums, cnts = [], []
    for gi, w in enumerate(POOL_WINDOWS):
        sl = slice(gi * LANES, (gi + 1) * LANES)
        s = cur[:, sl]
        for k in range(1, w):
            s = s + zz_ref[16 - k:16 - k + tm, sl]
        sums.append(s)
        cnts.append(jnp.minimum(pos + 1, w).astype(F32))
    _pool_tail(sums, cur, cnts, pw_ref, scale_ref, o_ref)


def _pool_step_kernel(z_ref, pw_ref, scale_ref, o_ref, *, start_pos):
    z = z_ref[...]
    cur = z[:, 15, :]
    sums, cnts = [], []
    for gi, w in enumerate(POOL_WINDOWS):
        sl = slice(gi * LANES, (gi + 1) * LANES)
        sums.append(jnp.sum(z[:, 16 - w:16, sl], axis=1))
        cnts.append(float(min(start_pos + 1, w)))
    _pool_tail(sums, cur, cnts, pw_ref, scale_ref, o_ref)


def _pool_seq(p_pool, hist16, pool_w, pool_scale, t, tm, start_pos):
    n = p_pool.shape[0]
    tps = t // tm
    return pl.pallas_call(
        functools.partial(_pool_seq_kernel, seq_tiles=tps, tm=tm, start_pos=start_pos),
        grid=(n // tm,),
        in_specs=[pl.BlockSpec((tm, 512), lambda i: (i, 0)),
                  pl.BlockSpec((16, 512), lambda i: (jnp.maximum(i * (tm // 16) - 1, 0), 0)),
                  pl.BlockSpec((None, 16, 512), lambda i: (i // tps, 0, 0)),
                  pl.BlockSpec((4, LANES, LANES), lambda i: (0, 0, 0)),
                  pl.BlockSpec((1, 512), lambda i: (0, 0))],
        out_specs=pl.BlockSpec((tm, 512), lambda i: (i, 0)),
        out_shape=jax.ShapeDtypeStruct((n, 512), BF16),
        scratch_shapes=[pltpu.VMEM((tm + 16, 512), F32)],
        compiler_params=_cparams(("parallel",)),
        name="pool_seq",
    )(p_pool, p_pool, hist16, pool_w, pool_scale)


def _pool_step(z16, pool_w, pool_scale, start_pos):
    bs = z16.shape[0]
    return pl.pallas_call(
        functools.partial(_pool_step_kernel, start_pos=start_pos),
        grid=(1,),
        in_specs=[pl.BlockSpec((bs, 16, 512), lambda i: (0, 0, 0)),
                  pl.BlockSpec((4, LANES, LANES), lambda i: (0, 0, 0)),
                  pl.BlockSpec((1, 512), lambda i: (0, 0))],
        out_specs=pl.BlockSpec((bs, 512), lambda i: (0, 0)),
        out_shape=jax.ShapeDtypeStruct((bs, 512), BF16),
        compiler_params=_cparams(("arbitrary",)),
        name="pool_step",
    )(z16, pool_w, pool_scale)


def _merge_kernel(x_ref, sc1_ref, sh1_ref, g1_ref, sc2_ref, sh2_ref, orw_ref, onsa_ref, opool_ref,
                  wg_ref, wbr_ref, wbn_ref, wbp_ref, wout_ref, lng_ref, lnb_ref, wr_ref, br_ref,
                  x1_ref, u2_ref, route_ref, *, alpha):
    x = x_ref[...]
    d = x.shape[1]
    u = (x * (1.0 + sc1_ref[...]) + sh1_ref[...]).astype(BF16)
    mixed = jnp.zeros(x.shape, F32)
    for bi, (o_ref, wb_ref) in enumerate(((orw_ref, wbr_ref), (onsa_ref, wbn_ref), (opool_ref, wbp_ref))):
        gate = _sigmoid(_dot(u, wg_ref[:, bi * d:(bi + 1) * d]))
        mixed = mixed + gate * _dot(o_ref[...], wb_ref[...])
    m = _dot(mixed.astype(BF16), wout_ref[...])
    x1 = _layer_norm(alpha * x + (1.0 + g1_ref[...]) * m, lng_ref[...], lnb_ref[...])
    x1_ref[...] = x1
    u2 = x1 * (1.0 + sc2_ref[...]) + sh2_ref[...]
    u2b = u2.astype(BF16)
    u2_ref[...] = u2b
    lg = _dot(u2b, wr_ref[...]) + br_ref[...]
    lane = lax.broadcasted_iota(jnp.int32, (1, LANES), 1)
    is_g = lane < MOE_GROUPS
    mg = jnp.max(jnp.where(is_g, lg, BELOW_NEG_INF), axis=1, keepdims=True)
    gsel = jnp.min(jnp.where(is_g & (lg == mg), lane, LANES), axis=1, keepdims=True)
    wgrp = 1.0 / jnp.sum(jnp.where(is_g, jnp.exp(lg - mg), 0.0), axis=1, keepdims=True)
    e_lane = lane - ROUTE_LANE0
    in_grp = (e_lane >= 0) & (e_lane < MOE_GROUPS * EXPERTS_PER_GROUP) & ((e_lane // EXPERTS_PER_GROUP) == gsel)
    v1 = jnp.max(jnp.where(in_grp, lg, BELOW_NEG_INF), axis=1, keepdims=True)
    i1 = jnp.min(jnp.where(in_grp & (lg == v1), lane, LANES), axis=1, keepdims=True)
    rest = in_grp & (lane != i1)
    v2 = jnp.max(jnp.where(rest, lg, BELOW_NEG_INF), axis=1, keepdims=True)
    i2 = jnp.min(jnp.where(rest & (lg == v2), lane, LANES), axis=1, keepdims=True)
    e2 = jnp.exp(v2 - v1)
    w1 = wgrp / (1.0 + e2)
    w2 = wgrp * e2 / (1.0 + e2)
    for g in range(MOE_GROUPS):
        src = lane + (ROUTE_LANE0 + EXPERTS_PER_GROUP * g)
        route_ref[g] = jnp.where(lane < EXPERTS_PER_GROUP,
                                 jnp.where(src == i1, w1, jnp.where(src == i2, w2, 0.0)), 0.0)


def _merge(x, mods, o_rw, o_nsa, o_pool, wts, rows_per_group, tm, alpha):
    n, d = x.shape
    full = lambda a: pl.BlockSpec(a.shape, lambda i: (0,) * a.ndim)
    row = lambda c: pl.BlockSpec((tm, c), lambda i: (i, 0))
    return pl.pallas_call(
        functools.partial(_merge_kernel, alpha=alpha),
        grid=(n // tm,),
        in_specs=[row(d)] + [_mod_spec(m, rows_per_group, tm) for m in mods]
                 + [row(512), row(1024), row(512)] + [full(w) for w in wts],
        out_specs=[row(d), row(d), pl.BlockSpec((MOE_GROUPS, tm, LANES), lambda i: (0, i, 0))],
        out_shape=[jax.ShapeDtypeStruct((n, d), F32), jax.ShapeDtypeStruct((n, d), BF16),
                   jax.ShapeDtypeStruct((MOE_GROUPS, n, LANES), F32)],
        compiler_params=_cparams(("parallel",), 56),
        name="merge",
    )(x, *mods, o_rw, o_nsa, o_pool, *wts)


def _moe_kernel(x1_ref, u2_ref, route_ref, g2_ref, wg_ref, wu_ref, wd_ref, lng_ref, lnb_ref, o_ref, acc_ref, *, alpha):
    g = pl.program_id(1)

    @pl.when(g == 0)
    def _():
        acc_ref[...] = jnp.zeros(acc_ref.shape, F32)

    u = u2_ref[...]
    hg = _dot(u, wg_ref[...])
    h = (hg * _sigmoid(hg) * _dot(u, wu_ref[...])).astype(BF16)
    route = route_ref[...]
    y = acc_ref[...]
    for e in range(EXPERTS_PER_GROUP):
        w = route[:, e:e + 1]
        ye = _dot(h[:, e * EXPERT_HIDDEN:(e + 1) * EXPERT_HIDDEN], wd_ref[e * EXPERT_HIDDEN:(e + 1) * EXPERT_HIDDEN, :])
        y = y + jnp.where(w != 0.0, w * ye, 0.0)
    acc_ref[...] = y

    @pl.when(g == pl.num_programs(1) - 1)
    def _():
        h2 = alpha * x1_ref[...] + (1.0 + g2_ref[...]) * acc_ref[...]
        o_ref[...] = _layer_norm(h2, lng_ref[...], lnb_ref[...])


def _moe(x1, u2, route, g2, w_gate, w_up, w_down, ln_g, ln_b, rows_per_group, tm, alpha):
    n, d = x1.shape
    hid = EXPERTS_PER_GROUP * EXPERT_HIDDEN
    r = g2.shape[1]
    return pl.pallas_call(
        functools.partial(_moe_kernel, alpha=alpha),
        grid=(n // tm, MOE_GROUPS),
        in_specs=[pl.BlockSpec((tm, d), lambda i, g: (i, 0)),
                  pl.BlockSpec((tm, d), lambda i, g: (i, 0)),
                  pl.BlockSpec((None, tm, LANES), lambda i, g: (g, i, 0)),
                  pl.BlockSpec((None, r, d), lambda i, g: ((i * tm) // rows_per_group, 0, 0)),
                  pl.BlockSpec((None, d, hid), lambda i, g: (g, 0, 0)),
                  pl.BlockSpec((None, d, hid), lambda i, g: (g, 0, 0)),
                  pl.BlockSpec((None, hid, d), lambda i, g: (g, 0, 0)),
                  pl.BlockSpec((1, d), lambda i, g: (0, 0)),
                  pl.BlockSpec((1, d), lambda i, g: (0, 0))],
        out_specs=pl.BlockSpec((tm, d), lambda i, g: (i, 0)),
        out_shape=jax.ShapeDtypeStruct((n, d), F32),
        scratch_shapes=[pltpu.VMEM((tm, d), F32)],
        compiler_params=_cparams(("parallel", "arbitrary"), 56),
        name="moe",
    )(x1, u2, route, g2, w_gate, w_up, w_down, ln_g, ln_b)


def _pad_heads(w, axis):
    shp = w.shape
    w = w.reshape(shp[:axis] + (NSA_KV_HEADS, NSA_GROUP, 1, NSA_HEAD_DIM) + shp[axis + 1:])
    sel = jnp.eye(NSA_KV_HEADS, dtype=w.dtype).reshape((1,) * axis + (NSA_KV_HEADS, 1, NSA_KV_HEADS, 1) + (1,) * (len(shp) - axis - 1))
    w = w * sel
    return w.reshape(shp[:axis] + (NSA_HEADS * LANES,) + shp[axis + 1:])


def _prep_weights(w_in, rw_w_up, rw_a_up, w_branch, moe_router_g, moe_bias_g, moe_router_e, moe_bias_e,
                  moe_w_gate, moe_w_up, moe_w_down, nsa_cmp_wk, nsa_cmp_wv):
    L, D, _ = w_in.shape
    w = {}
    w['rw'] = w_in[:, :, :RW_END].astype(BF16)
    wq = _pad_heads(w_in[:, :, RW_END:NSA_Q_END] * (NSA_HEAD_DIM ** -0.5), 2)
    wg = jnp.pad(w_in[:, :, NSA_KV_END:NSA_END], ((0, 0), (0, 0), (0, LANES - 3 * NSA_HEADS)))
    w['nsa'] = jnp.concatenate([wq, w_in[:, :, NSA_Q_END:NSA_KV_END], wg], axis=-1).astype(BF16)
    w['pool'] = w_in[:, :, NSA_END:POOL_END].astype(BF16)
    w['gate'] = w_in[:, :, POOL_END:].astype(BF16)
    z = jnp.zeros((L, HALF, RW_WIDTH), F32)
    w['wwa'] = jnp.concatenate([jnp.concatenate([rw_w_up, z], axis=2),
                                jnp.concatenate([z, rw_a_up], axis=2)], axis=1).astype(BF16)
    w['b_rw'] = w_branch[:, 0].astype(BF16)
    w['b_nsa'] = _pad_heads(w_branch[:, 1], 1).astype(BF16)
    w['b_pool'] = w_branch[:, 2].astype(BF16)
    pad_r = LANES - MOE_GROUPS - MOE_GROUPS * EXPERTS_PER_GROUP
    w['router'] = jnp.pad(jnp.concatenate([moe_router_g, moe_router_e], axis=2), ((0, 0), (0, 0), (0, pad_r))).astype(BF16)
    w['router_b'] = jnp.pad(jnp.concatenate([moe_bias_g, moe_bias_e], axis=1), ((0, 0), (0, pad_r)))[:, None, :]
    hid = EXPERTS_PER_GROUP * EXPERT_HIDDEN
    grp = lambda a: jnp.transpose(a.reshape(L, MOE_GROUPS, EXPERTS_PER_GROUP, D, EXPERT_HIDDEN),
                                  (0, 1, 3, 2, 4)).reshape(L, MOE_GROUPS, D, hid).astype(BF16)
    w['moe_gate'] = grp(moe_w_gate)
    w['moe_up'] = grp(moe_w_up)
    w['moe_down'] = moe_w_down.reshape(L, MOE_GROUPS, hid, D).astype(BF16)
    w['cmp_wk'] = jnp.tile(nsa_cmp_wk, (1, 1, NSA_KV_HEADS))
    w['cmp_wv'] = jnp.tile(nsa_cmp_wv, (1, 1, NSA_KV_HEADS))
    return w


def kernel(x_prompt, x_sample, cache_nsa_kv, cache_win_kv, state_rwkv, state_rwkv_shift, state_pool, page_table, c_prompt, c_sample, ada_w, ada_b, w_in, rw_mu, rw_w0, rw_w_up, rw_a0, rw_a_up, rw_g_up, rw_k_k, rw_k_a, rw_r_k, rw_gn_g, rw_gn_b, nsa_cmp_wk, nsa_cmp_wv, pool_w, pool_scale, w_branch, w_out, ln1_g, ln1_b, moe_router_g, moe_bias_g, moe_router_e, moe_bias_e, moe_w_gate, moe_w_up, moe_w_down, ln2_g, ln2_b):
    bp, t, d = x_prompt.shape
    bs, ts, _ = x_sample.shape
    L = ada_w.shape[0]
    n_pool, page = cache_nsa_kv.shape[1], cache_nsa_kv.shape[2]
    n_pages = page_table.shape[1]
    past_len = n_pages * page
    assert ts == 1 and d == D_MODEL
    assert t % 1024 == 0 and t // CMP_BLOCK <= LANES and past_len % SEL_BLOCK == 0
    assert cache_win_kv.shape[2] == WINDOW and bs % 8 == 0 and 2 * n_pages == LANES
    alpha = (2 * L) ** 0.25
    tm = 512

    w = _prep_weights(w_in, rw_w_up, rw_a_up, w_branch, moe_router_g, moe_bias_g, moe_router_e, moe_bias_e,
                      moe_w_gate, moe_w_up, moe_w_down, nsa_cmp_wk, nsa_cmp_wv)
    w_out_b = w_out.astype(BF16)
    pool_w_b = pool_w.astype(BF16)
    gup_b = rw_g_up.astype(BF16)

    nb = bp + bs
    nb_pad = -(-nb // 8) * 8
    c_all = jnp.pad(jnp.concatenate([c_prompt, c_sample], axis=0), ((0, nb_pad - nb), (0, 0)))
    mods = _ada_mod(c_all, ada_w, ada_b)

    cache_t = jnp.transpose(cache_nsa_kv, (0, 1, 3, 4, 5, 2)).astype(F32)
    win_cache = jnp.transpose(cache_win_kv, (0, 1, 3, 4, 5, 2)).astype(F32)
    pb = 8 if n_pool % 8 == 0 else 1
    cmp_w_t = jnp.tile(jnp.transpose(jnp.stack([nsa_cmp_wk, nsa_cmp_wv], axis=1), (0, 1, 3, 2)),
                       (1, 1, NSA_KV_HEADS, page // CMP_BLOCK))
    cmp_pool = _compress_pool(cache_t, cmp_w_t, pb)
    slopes = jnp.broadcast_to(jnp.asarray(NSA_SLOPES, F32)[:, None], (NSA_HEADS, LANES))

    xp = x_prompt.reshape(bp * t, d)
    xs = x_sample.reshape(bs, d)
    zeros_prev = jnp.zeros((bp, RW_COLS), F32)
    zeros_hist = jnp.zeros((bp, 16, POOL_WIDTH), F32)
    outs = {k: [] for k in ('nsa_p', 'nsa_s', 'win_p', 'win_s', 'rw_p', 'rw_s', 'sh_p', 'sh_s', 'pool_p', 'pool_s')}

    for l in range(L):
        vec = lambda a: a[l].reshape(1, -1)
        rw_wts = (vec(rw_mu), vec(rw_w0), vec(rw_a0), w['wwa'][l], gup_b[l], vec(rw_k_k), vec(rw_k_a), vec(rw_r_k))
        merge_wts = (w['gate'][l], w['b_rw'][l], w['b_nsa'][l], w['b_pool'][l], w_out_b[l], vec(ln1_g), vec(ln1_b),
                     w['router'][l], w['router_b'][l])
        mod_p = [m[:, None, :] for m in jnp.split(mods[l, :bp], 6, axis=-1)]
        mod_s = [m[None] for m in jnp.split(mods[l, bp:bp + bs], 6, axis=-1)]

        p_rw = _proj(xp, mod_p[1], mod_p[0], w['rw'][l], t, tm)
        q_pad, kv, gates, kvt, cmp = _proj_nsa(xp, mod_p[1], mod_p[0], w['nsa'][l], w['cmp_wk'][l], w['cmp_wv'][l], bp, t, tm)
        p_pool = _proj(xp, mod_p[1], mod_p[0], w['pool'][l], t, tm)
        seqs = _rwkv_prep(p_rw, zeros_prev, rw_wts, t, tm)
        o_rw, s_pairs = _rwkv_chunk(seqs, vec(rw_gn_g), vec(rw_gn_b), bp, t)
        o_nsa = _nsa_prompt(q_pad, kvt, _permute_cmp(cmp, bp, t), gates, bp, t)
        o_pool = _pool_seq(p_pool, zeros_hist, pool_w_b[l], vec(pool_scale), t, tm, 0)
        x1, u2, route = _merge(xp, (mod_p[1], mod_p[0], mod_p[2], mod_p[4], mod_p[3]), o_rw, o_nsa, o_pool,
                               merge_wts, t, tm, alpha)
        xp = _moe(x1, u2, route, mod_p[5], w['moe_gate'][l], w['moe_up'][l], w['moe_down'][l],
                  vec(ln2_g), vec(ln2_b), t, 2 * tm, alpha)
        kv3 = kv.reshape(bp, t, 6, NSA_KV_HEADS, NSA_HEAD_DIM)
        outs['nsa_p'].append(kv3[:, :, :4])
        outs['win_p'].append(kv3[:, t - min(WINDOW, t):, 4:])
        outs['rw_p'].append(_state_from_pairs(s_pairs))
        outs['sh_p'].append(p_rw.reshape(bp, t, RW_COLS)[:, -1])
        outs['pool_p'].append(p_pool.reshape(bp, t, POOL_WIDTH)[:, t - POOL_HIST:])

        p_rw = _proj(xs, mod_s[1], mod_s[0], w['rw'][l], bs, bs)
        q_pad, kv, gates = _proj_nsa(xs, mod_s[1], mod_s[0], w['nsa'][l], w['cmp_wk'][l], w['cmp_wv'][l], bs, 1, bs)
        p_pool = _proj(xs, mod_s[1], mod_s[0], w['pool'][l], bs, bs)
        seqs = _rwkv_prep(p_rw, state_rwkv_shift[l], rw_wts, 1, bs)
        o_rw, s_pairs = _rwkv_step(seqs, _state_to_pairs(state_rwkv[l]), vec(rw_gn_g), vec(rw_gn_b))
        o_nsa = _nsa_decode(page_table, cmp_pool, cache_t, win_cache, l, q_pad, gates, kv, slopes, past_len)
        z16 = jnp.concatenate([state_pool[l].astype(F32), p_pool[:, None, :]], axis=1)
        o_pool = _pool_step(z16, pool_w_b[l], vec(pool_scale), past_len)
        x1, u2, route = _merge(xs, (mod_s[1], mod_s[0], mod_s[2], mod_s[4], mod_s[3]), o_rw, o_nsa, o_pool,
                               merge_wts, bs, bs, alpha)
        xs = _moe(x1, u2, route, mod_s[5], w['moe_gate'][l], w['moe_up'][l], w['moe_down'][l],
                  vec(ln2_g), vec(ln2_b), bs, bs, alpha)
        kv3 = kv.reshape(bs, 1, 6, NSA_KV_HEADS, NSA_HEAD_DIM)
        outs['nsa_s'].append(kv3[:, :, :4])
        outs['win_s'].append(jnp.concatenate([cache_win_kv[l].astype(F32), kv3[:, :, 4:]], axis=1)[:, -WINDOW:])
        outs['rw_s'].append(_state_from_pairs(s_pairs))
        outs['sh_s'].append(p_rw)
        outs['pool_s'].append(z16[:, 1:])

    st = lambda k: jnp.stack(outs[k])
    return (xp.reshape(bp, t, d), xs.reshape(bs, 1, d), st('nsa_p'), st('nsa_s'), st('win_p'), st('win_s'),
            st('rw_p'), st('rw_s'), st('sh_p'), st('sh_s'), st('pool_p'), st('pool_s'))
```

```python
import functools

import jax
import jax.numpy as jnp
from jax import lax
from jax.experimental import pallas as pl
from jax.experimental.pallas import tpu as pltpu

F32 = jnp.float32
BF16 = jnp.bfloat16

D_MODEL = 1024
RW_HEADS = 8
RW_HEAD_DIM = 64
RW_WIDTH = 512
RW_COLS = 1792
RW_GN_EPS = 64e-5
RW_CHUNK = 64

NSA_HEADS = 8
NSA_KV_HEADS = 2
NSA_HEAD_DIM = 64
NSA_GROUP = 4
CMP_BLOCK = 32
SEL_BLOCK = 64
TOP_N = 8
WINDOW = 512
NSA_TQ = 128
NSA_TK = 512
NEG_INF = -1e30
BELOW_NEG_INF = -3e38
FORCE_BONUS = 1e9
NSA_SLOPES = tuple(2.0 ** (-8.0 * (h + 1) / NSA_HEADS) for h in range(NSA_HEADS))

POOL_WINDOWS = (2, 4, 8, 16)
POOL_HIST = 15
POOL_WIDTH = 512

RW_END = 1792
NSA_Q_END = RW_END + 512
NSA_KV_END = NSA_Q_END + 768
NSA_END = NSA_KV_END + 24
POOL_END = NSA_END + 512

MOE_GROUPS = 4
EXPERTS_PER_GROUP = 4
EXPERT_HIDDEN = 256
ROUTE_LANE0 = 4

LN_EPS = 1e-5
LANES = 128
HALF = 64


def _cparams(sem, vmem_mb=48):
    return pltpu.CompilerParams(dimension_semantics=sem, vmem_limit_bytes=vmem_mb * 1024 * 1024)


def _dot(a, b):
    return jnp.dot(a, b, preferred_element_type=F32)


def _dot_nt(a, b):
    return lax.dot_general(a, b, (((1,), (1,)), ((), ())), preferred_element_type=F32)


def _dot_tn(a, b):
    return lax.dot_general(a, b, (((0,), (0,)), ((), ())), preferred_element_type=F32)


def _sigmoid(x):
    return 1.0 / (1.0 + jnp.exp(-x))


def _softplus(x):
    return jnp.maximum(x, 0.0) + jnp.log(1.0 + jnp.exp(-jnp.abs(x)))


def _layer_norm(h, g, b):
    mu = jnp.mean(h, axis=-1, keepdims=True)
    d = h - mu
    var = jnp.mean(d * d, axis=-1, keepdims=True)
    return d * lax.rsqrt(var + LN_EPS) * g + b


def _half_sum(x, lo):
    s_lo = jnp.sum(jnp.where(lo, x, 0.0), axis=1, keepdims=True)
    s_hi = jnp.sum(jnp.where(lo, 0.0, x), axis=1, keepdims=True)
    return jnp.where(lo, s_lo, s_hi)


def _ada_kernel(c_ref, w_ref, b_ref, o_ref):
    c = c_ref[...]
    s = (c * _sigmoid(c)).astype(BF16)
    o_ref[...] = _dot(s, w_ref[...].astype(BF16)) + b_ref[...]


def _ada_mod(c_all, ada_w, ada_b):
    L, D, D6 = ada_w.shape
    nb = c_all.shape[0]
    return pl.pallas_call(
        _ada_kernel,
        grid=(L, D6 // D),
        in_specs=[pl.BlockSpec((nb, D), lambda l, j: (0, 0)),
                  pl.BlockSpec((None, D, D), lambda l, j: (l, 0, j)),
                  pl.BlockSpec((None, 1, D), lambda l, j: (l, 0, j))],
        out_specs=pl.BlockSpec((None, nb, D), lambda l, j: (l, 0, j)),
        out_shape=jax.ShapeDtypeStruct((L, nb, D6), F32),
        compiler_params=_cparams(("parallel", "parallel")),
        name="ada_mod",
    )(c_all, ada_w, ada_b.reshape(L, 1, D6))


def _mod_spec(mod, rows_per_group, tm):
    r = mod.shape[1]
    return pl.BlockSpec((None, r, mod.shape[2]), lambda i: ((i * tm) // rows_per_group, 0, 0))


def _proj_kernel(x_ref, sc_ref, sh_ref, w_ref, o_ref):
    u = (x_ref[...] * (1.0 + sc_ref[...]) + sh_ref[...]).astype(BF16)
    o_ref[...] = _dot(u, w_ref[...])


def _proj(x, sc, sh, w, rows_per_group, tm):
    n, d = x.shape
    nc = w.shape[1]
    return pl.pallas_call(
        _proj_kernel,
        grid=(n // tm,),
        in_specs=[pl.BlockSpec((tm, d), lambda i: (i, 0)),
                  _mod_spec(sc, rows_per_group, tm), _mod_spec(sh, rows_per_group, tm),
                  pl.BlockSpec((d, nc), lambda i: (0, 0))],
        out_specs=pl.BlockSpec((tm, nc), lambda i: (i, 0)),
        out_shape=jax.ShapeDtypeStruct((n, nc), F32),
        compiler_params=_cparams(("parallel",)),
        name="proj",
    )(x, sc, sh, w)


def _proj_nsa_kernel(x_ref, sc_ref, sh_ref, w_ref, wk_ref, wv_ref, q_ref, kv_ref, gate_ref, kvt_ref=None, cmp_ref=None):
    u = (x_ref[...] * (1.0 + sc_ref[...]) + sh_ref[...]).astype(BF16)
    res = _dot(u, w_ref[...])
    q_ref[...] = res[:, :1024].astype(BF16)
    kv = res[:, 1024:1792]
    kv_ref[...] = kv
    gate_ref[...] = _sigmoid(res[:, 1792:1920])
    if kvt_ref is None:
        return
    for j in range(6):
        kvt_ref[j] = kv[:, j * LANES:(j + 1) * LANES].astype(BF16)
    nb = kv.shape[0] // CMP_BLOCK
    kc = jnp.sum(kv[:, 0:LANES].reshape(nb, CMP_BLOCK, LANES) * wk_ref[...][None], axis=1)
    vc = jnp.sum(kv[:, LANES:2 * LANES].reshape(nb, CMP_BLOCK, LANES) * wv_ref[...][None], axis=1)
    cmp_ref[:, 0:LANES] = kc
    cmp_ref[:, LANES:2 * LANES] = vc


def _proj_nsa(x, sc, sh, w, wk, wv, n_seq, t, tm):
    n, d = x.shape
    out_specs = [pl.BlockSpec((tm, 1024), lambda i: (i, 0)),
                 pl.BlockSpec((tm, 768), lambda i: (i, 0)),
                 pl.BlockSpec((tm, LANES), lambda i: (i, 0))]
    out_shape = [jax.ShapeDtypeStruct((n, 1024), BF16),
                 jax.ShapeDtypeStruct((n, 768), F32),
                 jax.ShapeDtypeStruct((n, LANES), F32)]
    if t > 1:
        tps = t // tm
        nb = tm // CMP_BLOCK
        out_specs += [pl.BlockSpec((None, 6, tm, LANES), lambda i: (i // tps, 0, i % tps, 0)),
                      pl.BlockSpec((nb, 2 * LANES), lambda i: (i, 0))]
        out_shape += [jax.ShapeDtypeStruct((n_seq, 6, t, LANES), BF16),
                      jax.ShapeDtypeStruct((n // CMP_BLOCK, 2 * LANES), F32)]
    rows_per_group = t if sc.shape[1] == 1 else n
    return pl.pallas_call(
        _proj_nsa_kernel,
        grid=(n // tm,),
        in_specs=[pl.BlockSpec((tm, d), lambda i: (i, 0)),
                  _mod_spec(sc, rows_per_group, tm), _mod_spec(sh, rows_per_group, tm),
                  pl.BlockSpec((d, 1920), lambda i: (0, 0)),
                  pl.BlockSpec((CMP_BLOCK, LANES), lambda i: (0, 0)),
                  pl.BlockSpec((CMP_BLOCK, LANES), lambda i: (0, 0))],
        out_specs=out_specs,
        out_shape=out_shape,
        compiler_params=_cparams(("parallel",)),
        name="proj_nsa",
    )(x, sc, sh, w, wk, wv)


def _rwkv_prep_body(p, prev, mu_ref, w0_ref, a0_ref, wwa_ref, gup_ref, kkw_ref, kaw_ref, rk_ref, outs):
    r_ref, lw_ref, k_ref, v_ref, kk_ref, kka_ref, g_ref, bonus_ref = outs
    xs = p + (prev - p) * mu_ref[...]
    r = xs[:, 0:512]
    xk = xs[:, 512:1024]
    v = xs[:, 1024:1536]
    t12 = xs[:, 1536:1664]
    gd = xs[:, 1664:1792]
    lane = lax.broadcasted_iota(jnp.int32, (1, LANES), 1)
    lo = lane < HALF
    z = jnp.where(lo, jnp.tanh(t12), t12).astype(BF16)
    dwa = _dot(z, wwa_ref[...])
    w_log = -_softplus(-(w0_ref[...] + dwa[:, :512])) - 0.5
    a = _sigmoid(a0_ref[...] + dwa[:, 512:])
    g_ref[...] = _dot(_sigmoid(gd).astype(BF16), gup_ref[...])
    kmod = xk * (1.0 + (a - 1.0) * kaw_ref[...])
    kkr = xk * kkw_ref[...]
    rkr = r * kmod * rk_ref[...]
    for m in range(4):
        sl = slice(m * LANES, (m + 1) * LANES)
        x = kkr[:, sl]
        nrm = jnp.sqrt(_half_sum(x * x, lo))
        kk = x / jnp.maximum(nrm, 1e-12)
        kk_ref[:, sl] = kk
        kka_ref[:, sl] = kk * a[:, sl]
        bonus_ref[:, sl] = _half_sum(rkr[:, sl], lo) * v[:, sl]
    r_ref[...] = r
    lw_ref[...] = -jnp.exp(w_log)
    k_ref[...] = kmod
    v_ref[...] = v


def _rwkv_prep_seq_kernel(p_ref, pprev_ref, prow_ref, *rest, seq_tiles):
    i = pl.program_id(0)
    p = p_ref[...]
    first = (i % seq_tiles) == 0
    prev_row = jnp.where(first, prow_ref[...], pprev_ref[7:8, :])
    rolled = pltpu.roll(p, 1, 0)
    rowid = lax.broadcasted_iota(jnp.int32, (p.shape[0], 1), 0)
    prev = jnp.where(rowid == 0, prev_row, rolled)
    _rwkv_prep_body(p, prev, *rest[:8], rest[8:])


def _rwkv_prep_step_kernel(p_ref, prev_ref, *rest):
    _rwkv_prep_body(p_ref[...], prev_ref[...], *rest[:8], rest[8:])


def _rwkv_prep(p_rw, prev_rows, wts, t, tm):
    n = p_rw.shape[0]
    vec = lambda c: pl.BlockSpec((1, c), lambda i: (0, 0))
    w_specs = [vec(RW_COLS), vec(512), vec(512), pl.BlockSpec((LANES, 1024), lambda i: (0, 0)),
               pl.BlockSpec((LANES, 512), lambda i: (0, 0)), vec(512), vec(512), vec(512)]
    out_specs = [pl.BlockSpec((tm, 512), lambda i: (i, 0))] * 8
    out_shape = [jax.ShapeDtypeStruct((n, 512), F32)] * 8
    if t == 1:
        kern = _rwkv_prep_step_kernel
        in_specs = [pl.BlockSpec((tm, RW_COLS), lambda i: (i, 0)), pl.BlockSpec((tm, RW_COLS), lambda i: (i, 0))]
        args = (p_rw, prev_rows)
    else:
        tps = t // tm
        kern = functools.partial(_rwkv_prep_seq_kernel, seq_tiles=tps)
        in_specs = [pl.BlockSpec((tm, RW_COLS), lambda i: (i, 0)),
                    pl.BlockSpec((8, RW_COLS), lambda i: (jnp.maximum(i * (tm // 8) - 1, 0), 0)),
                    pl.BlockSpec((None, 1, RW_COLS), lambda i: (i // tps, 0, 0))]
        args = (p_rw, p_rw, prev_rows[:, None, :])
    return pl.pallas_call(
        kern, grid=(n // tm,), in_specs=in_specs + w_specs, out_specs=out_specs, out_shape=out_shape,
        compiler_params=_cparams(("parallel",)), name="rwkv_prep",
    )(*args, *wts)


def _split3(x):
    x1 = x.astype(BF16)
    r1 = x - x1.astype(F32)
    x2 = r1.astype(BF16)
    x3 = (r1 - x2.astype(F32)).astype(BF16)
    return x1, x2, x3


def _rwkv_chunk_kernel(r_ref, lw_ref, k_ref, v_ref, kk_ref, kka_ref, g_ref, bonus_ref, gng_ref, gnb_ref,
                       o_ref, sout_ref, s_ref):
    c = pl.program_id(1)
    C = RW_CHUNK

    @pl.when(c == 0)
    def _():
        s_ref[...] = jnp.zeros(s_ref.shape, F32)

    ns = lw_ref.shape[0]
    flat = lambda ref: ref[...].reshape(ns * C, ref.shape[-1])
    lw = flat(lw_ref)
    row = lax.broadcasted_iota(jnp.int32, (ns * C, ns * C), 0)
    col = lax.broadcasted_iota(jnp.int32, (ns * C, ns * C), 1)
    tri = ((row >= col) & (row // C == col // C)).astype(BF16)
    l1, l2, l3 = _split3(lw)
    cl = _dot(tri, l1) + _dot(tri, l2) + _dot(tri, l3)
    cl_last = jnp.concatenate([jnp.broadcast_to(cl[(q + 1) * C - 1:(q + 1) * C, :], (C, cl.shape[1]))
                               for q in range(ns)], axis=0)
    g_in = jnp.exp(cl)
    g_ex = jnp.exp(cl - lw)
    g_inv = jnp.exp(-cl)
    g_rem = jnp.exp(cl_last - cl)
    gc = jnp.exp(cl_last)
    k = flat(k_ref)
    kka = flat(kka_ref)
    qk_all = flat(kk_ref) * g_ex
    r_all = flat(r_ref) * g_in
    kt_all = k * g_inv
    at_all = kka * g_inv
    kd_all = k * g_rem
    ad_all = kka * g_rem
    v_all = flat(v_ref)

    lane = lax.broadcasted_iota(jnp.int32, (1, LANES), 1)
    lo = lane < HALF

    def st(x):
        return jnp.concatenate([jnp.where(lo, x, 0.0), jnp.where(lo, 0.0, x)], axis=0)

    r2 = lax.broadcasted_iota(jnp.int32, (4 * C, 4 * C), 0)
    c2 = lax.broadcasted_iota(jnp.int32, (4 * C, 4 * C), 1)
    rt = r2 % C
    ct = c2 % C
    tmask = (ct < rt) | ((r2 >= 2 * C) & (ct == rt))
    ri = lax.broadcasted_iota(jnp.int32, (2 * C, 2 * C), 0)
    ci = lax.broadcasted_iota(jnp.int32, (2 * C, 2 * C), 1)
    eye = (ri == ci).astype(F32)

    units = [(q, m) for q in range(ns) for m in range(4)]
    pairs = range(len(units))
    cut = lambda x, u: x[units[u][0] * C:(units[u][0] + 1) * C, units[u][1] * LANES:(units[u][1] + 1) * LANES]
    sls = [slice(m * LANES, (m + 1) * LANES) for _, m in units]
    bf = lambda x: x.astype(BF16)
    qk_st = [st(cut(qk_all, u)) for u in pairs]
    r_st = [st(cut(r_all, u)) for u in pairs]
    v_st = [st(cut(v_all, u)) for u in pairs]
    ad_st = [st(cut(ad_all, u)) for u in pairs]
    kd_st = [st(cut(kd_all, u)) for u in pairs]
    xm = [jnp.where(tmask, _dot_nt(bf(jnp.concatenate([qk_st[m], r_st[m]], axis=0)),
                                   bf(jnp.concatenate([st(cut(at_all, m)), st(cut(kt_all, m))], axis=0))), 0.0)
          for m in pairs]
    av = [_dot(bf(xm[m][:, 2 * C:]), bf(v_st[m])) for m in pairs]
    p = [xm[m][:2 * C, :2 * C] for m in pairs]
    tm_ = [eye - p[m] for m in pairs]
    for _ in range(5):
        p = [_dot(bf(p[m]), bf(p[m])) for m in pairs]
        tm_ = [_dot(bf(tm_[m]), bf(eye + p[m])) for m in pairs]
    wu = [_dot(bf(tm_[m]), bf(jnp.concatenate([qk_st[m], av[m][:2 * C]], axis=1))) for m in pairs]
    ry = [_dot(bf(xm[m][2 * C:, :2 * C]), bf(wu[m])) for m in pairs]
    rq = [r_st[m] - ry[m][:, :LANES] for m in pairs]
    y0 = [av[m][2 * C:] - ry[m][:, LANES:] for m in pairs]
    gt = [eye * cut(gc, m)[0:1] - _dot_tn(bf(wu[m][:, :LANES]), bf(ad_st[m])) for m in pairs]
    ht = [_dot_tn(bf(jnp.concatenate([v_st[m], wu[m][:, LANES:]], axis=0)),
                  bf(jnp.concatenate([kd_st[m], -ad_st[m]], axis=0))) for m in pairs]
    sb = [bf(s_ref[q, m]) for q, m in units]
    y_st = [_dot_nt(bf(rq[m]), sb[m]) + y0[m] for m in pairs]
    s_new = [_dot(sb[m], bf(gt[m])) + ht[m] for m in pairs]
    for u, (q, m) in enumerate(units):
        sl = sls[u]
        s_ref[q, m] = s_new[u]
        sout_ref[q, m] = s_new[u][:C] + s_new[u][C:]
        y = y_st[u][:C] + y_st[u][C:]
        o_ref[q, :, sl] = _rwkv_out(y, lo, gng_ref[:, sl], gnb_ref[:, sl], bonus_ref[q, :, sl], g_ref[q, :, sl])


def _rwkv_out(y, lo, gn_g, gn_b, bonus, g):
    mu = _half_sum(y, lo) * (1.0 / RW_HEAD_DIM)
    d = y - mu
    var = _half_sum(d * d, lo) * (1.0 / RW_HEAD_DIM)
    yn = d * lax.rsqrt(var + RW_GN_EPS) * gn_g + gn_b
    return ((yn + bonus) * g).astype(BF16)


def _rwkv_chunk(seqs, gn_g, gn_b, n_seq, t):
    C = RW_CHUNK
    nch = t // C
    ns = 4 if n_seq % 4 == 0 else (2 if n_seq % 2 == 0 else 1)
    row_spec = pl.BlockSpec((ns, C, 512), lambda b, c: (b, c, 0))
    vec = pl.BlockSpec((1, 512), lambda b, c: (0, 0))
    o, s = pl.pallas_call(
        _rwkv_chunk_kernel,
        grid=(n_seq // ns, nch),
        in_specs=[row_spec] * 8 + [vec, vec],
        out_specs=[row_spec, pl.BlockSpec((ns, 4, RW_HEAD_DIM, LANES), lambda b, c: (b, 0, 0, 0))],
        out_shape=[jax.ShapeDtypeStruct((n_seq, t, 512), BF16),
                   jax.ShapeDtypeStruct((n_seq, 4, RW_HEAD_DIM, LANES), F32)],
        scratch_shapes=[pltpu.VMEM((ns, 4, LANES, LANES), F32)],
        compiler_params=_cparams(("parallel", "arbitrary")),
        name="rwkv_chunk",
    )(*[a.reshape(n_seq, t, 512) for a in seqs], gn_g, gn_b)
    return o.reshape(n_seq * t, 512), s


def _rwkv_step_kernel(r_ref, lw_ref, k_ref, v_ref, kk_ref, kka_ref, g_ref, bonus_ref, s_ref, gng_ref, gnb_ref,
                      o_ref, sout_ref):
    lane = lax.broadcasted_iota(jnp.int32, (1, LANES), 1)
    lo = lane < HALF
    rowi = lax.broadcasted_iota(jnp.int32, (RW_HEAD_DIM, LANES), 0)
    diag = (rowi == lax.broadcasted_iota(jnp.int32, (RW_HEAD_DIM, LANES), 1) % HALF).astype(F32)
    for m in range(4):
        sl = slice(m * LANES, (m + 1) * LANES)
        s = s_ref[m]
        sk = _half_sum(s * kk_ref[:, sl], lo)
        v_col = _half_sum(diag * v_ref[:, sl], lo)
        s_new = s * jnp.exp(lw_ref[:, sl]) - sk * kka_ref[:, sl] + v_col * k_ref[:, sl]
        sout_ref[m] = s_new
        y_col = _half_sum(s_new * r_ref[:, sl], lo)
        y = jnp.sum(diag * y_col, axis=0, keepdims=True)
        o_ref[:, sl] = _rwkv_out(y, lo, gng_ref[:, sl], gnb_ref[:, sl], bonus_ref[:, sl], g_ref[:, sl])


def _rwkv_step(seqs, s_pairs, gn_g, gn_b):
    bs = s_pairs.shape[0]
    row_spec = pl.BlockSpec((None, 1, 512), lambda b: (b, 0, 0))
    st_spec = pl.BlockSpec((None, 4, RW_HEAD_DIM, LANES), lambda b: (b, 0, 0, 0))
    vec = pl.BlockSpec((1, 512), lambda b: (0, 0))
    o, s = pl.pallas_call(
        _rwkv_step_kernel,
        grid=(bs,),
        in_specs=[row_spec] * 8 + [st_spec, vec, vec],
        out_specs=[row_spec, st_spec],
        out_shape=[jax.ShapeDtypeStruct((bs, 1, 512), BF16),
                   jax.ShapeDtypeStruct((bs, 4, RW_HEAD_DIM, LANES), F32)],
        compiler_params=_cparams(("parallel",)),
        name="rwkv_step",
    )(*[a[:, None, :] for a in seqs], s_pairs, gn_g, gn_b)
    return o[:, 0], s


def _state_to_pairs(s):
    b = s.shape[0]
    return jnp.transpose(s.reshape(b, 4, 2, 64, 64).astype(F32), (0, 1, 3, 2, 4)).reshape(b, 4, 64, LANES)


def _state_from_pairs(sp):
    b = sp.shape[0]
    return jnp.transpose(sp.reshape(b, 4, 64, 2, 64), (0, 1, 3, 2, 4)).reshape(b, 8, 64, 64)


def _nsa_prompt_kernel(q_ref, kvt_ref, cmp_ref, gate_ref, o_ref, m_ref, acc_ref, *, k_top, tk):
    qi = pl.program_id(1)
    tq = NSA_TQ
    q0 = qi * tq
    lane = lax.broadcasted_iota(jnp.int32, (1, LANES), 1)
    lo = lane < HALF
    tok = q0 + lax.broadcasted_iota(jnp.int32, (tq, 1), 0)
    cmpv = cmp_ref[...]
    kc = cmpv[:, :LANES].astype(BF16)
    vc = cmpv[:, LANES:].astype(BF16)
    gates = gate_ref[...]
    blk_c = 2 * (lane % HALF) + lane // HALF
    dist_c = tok - (blk_c * CMP_BLOCK + (CMP_BLOCK - 1))
    mask_c = dist_c >= 0
    dist_cf = dist_c.astype(F32)
    cur = tok // SEL_BLOCK
    valid = (lane * SEL_BLOCK <= tok) & lo
    forced = (lane == 0) | (lane == cur) | (lane == cur - 1)
    in_grp = (lo, jnp.logical_not(lo))
    den_lane = (HALF, 0)
    hrows = lambda h: slice(h * tq, (h + 1) * tq)
    grows = lambda g: slice(g * NSA_GROUP * tq, (g + 1) * NSA_GROUP * tq)
    q_all = jnp.concatenate([q_ref[:, h * LANES:(h + 1) * LANES] for h in range(NSA_HEADS)], axis=0)

    s = _dot_nt(q_all, kc)
    ps = []
    imp = [jnp.zeros((tq, LANES), F32) for _ in range(NSA_KV_HEADS)]
    for h in range(NSA_HEADS):
        sr = jnp.where(mask_c, s[hrows(h)] - NSA_SLOPES[h] * dist_cf, NEG_INF)
        mx = jnp.max(sr, axis=1, keepdims=True)
        e = jnp.where(mask_c, jnp.exp(sr - mx), 0.0)
        p = e / jnp.maximum(jnp.sum(e, axis=1, keepdims=True), 1e-30)
        ps.append(p.astype(BF16))
        imp[h // NSA_GROUP] = imp[h // NSA_GROUP] + p
    o_c = _dot(jnp.concatenate(ps, axis=0), vc)

    wk = WINDOW + tq
    kstart = pl.multiple_of(jnp.maximum(q0 - WINDOW, 0), tq)
    kt_w = kvt_ref[4, pl.ds(kstart, wk), :]
    vt_w = kvt_ref[5, pl.ds(kstart, wk), :]
    s_w = _dot_nt(q_all, kt_w)
    dist_w = tok - (kstart + lax.broadcasted_iota(jnp.int32, (1, wk), 1))
    dist_wf = dist_w.astype(F32)
    bias_w = jnp.where((dist_w >= 0) & (dist_w <= WINDOW), 0.0, NEG_INF)
    o_w = []
    for g in range(NSA_KV_HEADS):
        es = []
        for r in range(NSA_GROUP):
            h = NSA_GROUP * g + r
            x = s_w[hrows(h)] - NSA_SLOPES[h] * dist_wf + bias_w
            e = jnp.exp(x - jnp.max(x, axis=1, keepdims=True))
            es.append((e / jnp.sum(e, axis=1, keepdims=True)).astype(BF16))
        o_w.append(_dot(jnp.concatenate(es, axis=0), vt_w))

    score, sel = [], []
    for g in range(NSA_KV_HEADS):
        imp_sel = imp[g] + pltpu.roll(imp[g], HALF, 1)
        sc = jnp.where(valid, imp_sel + jnp.where(forced, FORCE_BONUS, 0.0), NEG_INF)
        score.append(jnp.where(lo, sc, BELOW_NEG_INF))
        sel.append(jnp.zeros((tq, LANES), F32))
    for _ in range(k_top):
        for g in range(NSA_KV_HEADS):
            mx = jnp.max(score[g], axis=1, keepdims=True)
            idx = jnp.min(jnp.where(score[g] == mx, lane, 4 * LANES), axis=1, keepdims=True)
            hit = lane == idx
            sel[g] = jnp.where(hit, 1.0, sel[g])
            score[g] = jnp.where(hit, BELOW_NEG_INF, score[g])
    sel_b = [x.astype(BF16) for x in sel]

    n_sub = tk // LANES
    blk_row = lax.broadcasted_iota(jnp.int32, (LANES, tk), 0)
    key_blk = lax.broadcasted_iota(jnp.int32, (LANES, tk), 1) // SEL_BLOCK
    col_k = lax.broadcasted_iota(jnp.int32, (1, tk), 1)
    m_ref[...] = jnp.full(m_ref.shape, NEG_INF, F32)
    acc_ref[...] = jnp.zeros(acc_ref.shape, F32)

    def sel_tile(j, carry):
        start = pl.multiple_of(j * tk, tk)
        kt = kvt_ref[2, pl.ds(start, tk), :]
        vt = kvt_ref[3, pl.ds(start, tk), :]
        s = _dot_nt(q_all, kt)
        dist = tok - (start + col_k)
        dist_f = dist.astype(F32)
        causal = jnp.where(dist >= 0, 0.0, NEG_INF)
        expand = (blk_row == (tk // SEL_BLOCK) * j + key_blk).astype(BF16)
        for g in range(NSA_KV_HEADS):
            bias = (_dot(sel_b[g], expand) - 1.0) * (-NEG_INF) + causal
            es = []
            for r in range(NSA_GROUP):
                h = NSA_GROUP * g + r
                x = s[hrows(h)] - NSA_SLOPES[h] * dist_f + bias
                xm = x[:, 0:LANES]
                for i in range(1, n_sub):
                    xm = jnp.maximum(xm, x[:, i * LANES:(i + 1) * LANES])
                m_old = m_ref[hrows(h)]
                m_new = jnp.maximum(m_old, jnp.max(xm, axis=1, keepdims=True))
                m_ref[hrows(h)] = m_new
                acc_ref[hrows(h)] = jnp.exp(m_old - m_new) * acc_ref[hrows(h)]
                es.append(jnp.exp(x - jnp.concatenate([m_new] * n_sub, axis=1)).astype(BF16))
            vg = jnp.where(in_grp[g], vt, jnp.ones_like(vt))
            acc_ref[grows(g)] += _dot(jnp.concatenate(es, axis=0), vg)
        return carry

    lax.fori_loop(0, (q0 + tq - 1) // tk + 1, sel_tile, 0)
    acc = acc_ref[...]
    o_s = [acc[grows(g)] / jnp.maximum(acc[grows(g), den_lane[g]:den_lane[g] + 1], 1e-30) for g in range(NSA_KV_HEADS)]

    for h in range(NSA_HEADS):
        g, r = divmod(h, NSA_GROUP)
        rows = slice(r * tq, (r + 1) * tq)
        o = (gates[:, 3 * h:3 * h + 1] * o_c[hrows(h)] + gates[:, 3 * h + 1:3 * h + 2] * o_s[g][rows]
             + gates[:, 3 * h + 2:3 * h + 3] * o_w[g][rows])
        o_ref[:, h * LANES:(h + 1) * LANES] = jnp.where(in_grp[g], o, 0.0).astype(BF16)


def _nsa_prompt(q_pad, kvt, cmp_perm, gates, n_seq, t):
    tq = NSA_TQ
    nq = t // tq
    k_top = min(TOP_N, t // SEL_BLOCK)
    assert k_top >= 3 and t >= WINDOW + tq and t % NSA_TK == 0
    return pl.pallas_call(
        functools.partial(_nsa_prompt_kernel, k_top=k_top, tk=NSA_TK),
        grid=(n_seq, nq),
        in_specs=[pl.BlockSpec((tq, 1024), lambda b, i: (b * nq + i, 0)),
                  pl.BlockSpec((None, 6, t, LANES), lambda b, i: (b, 0, 0, 0)),
                  pl.BlockSpec((None, LANES, 2 * LANES), lambda b, i: (b, 0, 0)),
                  pl.BlockSpec((tq, LANES), lambda b, i: (b * nq + i, 0))],
        out_specs=pl.BlockSpec((tq, 1024), lambda b, i: (b * nq + i, 0)),
        out_shape=jax.ShapeDtypeStruct((n_seq * t, 1024), BF16),
        scratch_shapes=[pltpu.VMEM((NSA_HEADS * tq, LANES), F32), pltpu.VMEM((NSA_HEADS * tq, LANES), F32)],
        compiler_params=_cparams(("parallel", "parallel")),
        name="nsa_prompt",
    )(q_pad, kvt, cmp_perm, gates)


def _permute_cmp(cmp, n_seq, t):
    nc = t // CMP_BLOCK
    c = cmp.reshape(n_seq, nc // 2, 2, 2 * LANES)
    c = jnp.pad(c, ((0, 0), (0, HALF - nc // 2), (0, 0), (0, 0)))
    return jnp.transpose(c, (0, 2, 1, 3)).reshape(n_seq, LANES, 2 * LANES)


def _compress_pool_kernel(x_ref, w_ref, o_ref):
    pb = x_ref.shape[0]
    page = x_ref.shape[-1]
    nb = page // CMP_BLOCK
    blk = lax.broadcasted_iota(jnp.int32, (8, page), 0)
    seg = (lax.broadcasted_iota(jnp.int32, (8, page), 1) // CMP_BLOCK == blk).astype(BF16)
    items = [(p, j) for p in range(pb) for j in range(2)]
    parts = [_split3(x_ref[p, j].reshape(LANES, page) * w_ref[j]) for p, j in items]
    sums = [_dot_nt(seg, k1) + _dot_nt(seg, k2) + _dot_nt(seg, k3) for k1, k2, k3 in parts]
    for (p, j), res in zip(items, sums):
        o_ref[p, j * nb:(j + 1) * nb, :] = res[0:nb]


def _compress_pool(cache_t, w_t, pb):
    L, n_pool = cache_t.shape[:2]
    page = cache_t.shape[-1]
    nb = page // CMP_BLOCK
    return pl.pallas_call(
        _compress_pool_kernel,
        grid=(L, n_pool // pb),
        in_specs=[pl.BlockSpec((None, pb, 2, NSA_KV_HEADS, NSA_HEAD_DIM, page), lambda l, i: (l, i, 0, 0, 0, 0)),
                  pl.BlockSpec((None, 2, LANES, page), lambda l, i: (l, 0, 0, 0))],
        out_specs=pl.BlockSpec((None, pb, 2 * nb, LANES), lambda l, i: (l, i, 0, 0)),
        out_shape=jax.ShapeDtypeStruct((L, n_pool, 2 * nb, LANES), F32),
        compiler_params=_cparams(("parallel", "parallel")),
        name="compress_pool",
    )(cache_t, w_t)


def _nsa_dec_cmp_kernel(pt_ref, pool_ref, q_ref, gate_ref, slope_ref, win_ref, kvn_ref, part_ref, idx_ref, gath_ref,
                        *, n_pages, past_len, k_top):
    b = pl.program_id(0)
    t = past_len
    for p in range(n_pages):
        tile = pool_ref[pt_ref[b, p]]
        for r in range(8):
            gath_ref[r, p:p + 1, :] = tile[r:r + 1, :]
    x = [gath_ref[r] for r in range(8)]
    q = q_ref[...]
    slope = slope_ref[:, 0:1]
    gates = gate_ref[...]
    lane = lax.broadcasted_iota(jnp.int32, (1, LANES), 1)
    lane_p = lax.broadcasted_iota(jnp.int32, (1, 2 * n_pages), 1)
    page_of = lane_p % n_pages
    ss, masks = [], []
    for pair in range(2):
        kmat = jnp.concatenate([x[2 * pair], x[2 * pair + 1]], axis=0).astype(BF16)
        blk = page_of * 4 + 2 * pair + lane_p // n_pages
        dist = t - (blk * CMP_BLOCK + (CMP_BLOCK - 1))
        mask = dist >= 0
        s = _dot_nt(q, kmat) - slope * dist.astype(F32)
        ss.append(jnp.where(mask, s, NEG_INF))
        masks.append(mask)
    mx = jnp.maximum(jnp.max(ss[0], axis=1, keepdims=True), jnp.max(ss[1], axis=1, keepdims=True))
    es = [jnp.where(masks[i], jnp.exp(ss[i] - mx), 0.0) for i in range(2)]
    den = jnp.maximum(jnp.sum(es[0], axis=1, keepdims=True) + jnp.sum(es[1], axis=1, keepdims=True), 1e-30)
    ps = [e / den for e in es]
    o_c = jnp.zeros((NSA_HEADS, LANES), F32)
    for pair in range(2):
        vmat = jnp.concatenate([x[4 + 2 * pair], x[5 + 2 * pair]], axis=0).astype(BF16)
        o_c = o_c + _dot(ps[pair].astype(BF16), vmat)
    n_sel_blk = lane_p // n_pages + 2 * page_of
    cur = t // SEL_BLOCK
    forced = (n_sel_blk == 0) | (n_sel_blk == cur) | (n_sel_blk == cur - 1)
    valid = n_sel_blk * SEL_BLOCK <= t
    idx_rows = []
    for g in range(NSA_KV_HEADS):
        imp_a = jnp.sum(ps[0][NSA_GROUP * g:NSA_GROUP * (g + 1)], axis=0, keepdims=True)
        imp_b = jnp.sum(ps[1][NSA_GROUP * g:NSA_GROUP * (g + 1)], axis=0, keepdims=True)
        ev = imp_a + pltpu.roll(imp_a, n_pages, 1)
        od = imp_b + pltpu.roll(imp_b, n_pages, 1)
        imp = jnp.where(lane_p < n_pages, ev, od)
        score = jnp.where(valid, imp + jnp.where(forced, FORCE_BONUS, 0.0), NEG_INF)
        chosen = jnp.where(lane == 0, cur, 0)
        for it in range(1, k_top):
            mxs = jnp.max(score, axis=1, keepdims=True)
            pick = jnp.min(jnp.where(score == mxs, n_sel_blk, 1 << 30), axis=1, keepdims=True)
            score = jnp.where(n_sel_blk == pick, BELOW_NEG_INF, score)
            chosen = jnp.where(lane == it, pick, chosen)
        idx_rows.append(chosen)
    rowi = lax.broadcasted_iota(jnp.int32, (8, LANES), 0)
    idx_ref[...] = jnp.where(rowi == 0, idx_rows[0], jnp.where(rowi == 1, idx_rows[1], 0))
    wl = win_ref.shape[-1]
    kw = win_ref[0].reshape(LANES, wl).astype(BF16)
    vw = win_ref[1].reshape(LANES, wl).astype(BF16)
    col = lax.broadcasted_iota(jnp.int32, (1, wl), 1)
    dist_w = wl - col
    mask_w = (t - dist_w >= 0) & (dist_w <= WINDOW)
    s_w = jnp.where(mask_w, _dot(q, kw) - slope * dist_w.astype(F32), NEG_INF)
    kvn = kvn_ref[...]
    qf = q.astype(F32)
    s_n = jnp.sum(qf * kvn[4:5, :].astype(BF16).astype(F32), axis=1, keepdims=True)
    mw = jnp.maximum(jnp.max(s_w, axis=1, keepdims=True), s_n)
    e_w = jnp.where(mask_w, jnp.exp(s_w - mw), 0.0)
    e_n = jnp.exp(s_n - mw)
    den_w = jnp.maximum(jnp.sum(e_w, axis=1, keepdims=True) + e_n, 1e-30)
    o_w = (_dot_nt((e_w / den_w).astype(BF16), vw)
           + (e_n / den_w).astype(BF16).astype(F32) * kvn[5:6, :].astype(BF16).astype(F32))
    part_ref[...] = gates[:, 0:1] * o_c + gates[:, 2:3] * o_w


def _nsa_dec_sel_kernel(pt_ref, sel_ref, blk_ref, q_ref, gate_ref, slope_ref, kvn_ref, part_ref, o_ref,
                        stage_ref, *, past_len, k_top, cur_selected):
    b = pl.program_id(0)
    j = pl.program_id(1)
    t = past_len
    stage_ref[j] = blk_ref[...]

    @pl.when(j == pl.num_programs(1) - 1)
    def _():
        q = q_ref[...]
        slope = slope_ref[:, 0:1]
        kvn = kvn_ref[...]
        page = blk_ref.shape[-1]
        halves = page // SEL_BLOCK
        col = lax.broadcasted_iota(jnp.int32, (1, page), 1)
        rowh = lax.broadcasted_iota(jnp.int32, (NSA_HEADS, 1), 0)
        lane = lax.broadcasted_iota(jnp.int32, (1, LANES), 1)
        s_new = jnp.sum(q.astype(F32) * kvn[2:3, :].astype(BF16).astype(F32), axis=1, keepdims=True)
        v_new = kvn[3:4, :].astype(BF16).astype(F32)
        o_groups = []
        for g in range(NSA_KV_HEADS):
            ss, masks, vts = [], [], []
            has_cur = cur_selected
            for i in range(k_top):
                jj = g * k_top + i
                n = sel_ref[b, jj]
                kt = stage_ref[jj, 0].reshape(LANES, page).astype(BF16)
                vts.append(stage_ref[jj, 1].reshape(LANES, page).astype(BF16))
                pos = (n // halves) * page + col
                mask = (col // SEL_BLOCK == n % halves) & (pos < t)
                ss.append(jnp.where(mask, _dot(q, kt) - slope * (t - pos).astype(F32), NEG_INF))
                masks.append(mask)
                has_cur = jnp.logical_or(has_cur, n * SEL_BLOCK + SEL_BLOCK > t)
            s_n = jnp.where(has_cur, s_new, NEG_INF)
            mx = s_n
            for s in ss:
                mx = jnp.maximum(mx, jnp.max(s, axis=1, keepdims=True))
            es = [jnp.where(masks[i], jnp.exp(ss[i] - mx), 0.0) for i in range(k_top)]
            e_n = jnp.where(has_cur, jnp.exp(s_n - mx), 0.0)
            den = e_n
            for e in es:
                den = den + jnp.sum(e, axis=1, keepdims=True)
            den = jnp.maximum(den, 1e-30)
            o = (e_n / den).astype(BF16).astype(F32) * v_new
            for i in range(k_top):
                o = o + _dot_nt((es[i] / den).astype(BF16), vts[i])
            o_groups.append(o)
        o_s = jnp.where(rowh // NSA_GROUP == 0, o_groups[0], o_groups[1])
        o = part_ref[...] + gate_ref[:, 1:2] * o_s
        in_half = (lane // HALF) == (rowh // NSA_GROUP)
        o_ref[...] = jnp.where(in_half, o, 0.0).astype(BF16)


def _nsa_decode(page_table, cmp_pool, cache_t, win_t, l, q_pad, gates, kv_new, slopes, past_len):
    bs, n_pages = page_table.shape
    n_pool, page = cache_t.shape[1], cache_t.shape[-1]
    k_top = min(TOP_N, past_len // SEL_BLOCK + 1)
    q3 = q_pad.reshape(bs, NSA_HEADS, LANES)
    g3 = jnp.pad(gates[:, :3 * NSA_HEADS].reshape(bs, NSA_HEADS, 3), ((0, 0), (0, 0), (0, LANES - 3)))
    kvn = kv_new.reshape(bs, 6, LANES)
    wl = win_t.shape[-1]
    part, idx = pl.pallas_call(
        functools.partial(_nsa_dec_cmp_kernel, n_pages=n_pages, past_len=past_len, k_top=k_top),
        grid_spec=pltpu.PrefetchScalarGridSpec(
            num_scalar_prefetch=1, grid=(bs,),
            in_specs=[pl.BlockSpec((None, n_pool, 8, LANES), lambda b, pt: (l, 0, 0, 0)),
                      pl.BlockSpec((None, NSA_HEADS, LANES), lambda b, pt: (b, 0, 0)),
                      pl.BlockSpec((None, NSA_HEADS, LANES), lambda b, pt: (b, 0, 0)),
                      pl.BlockSpec((NSA_HEADS, LANES), lambda b, pt: (0, 0)),
                      pl.BlockSpec((None, None, 2, NSA_KV_HEADS, NSA_HEAD_DIM, wl), lambda b, pt: (l, b, 0, 0, 0, 0)),
                      pl.BlockSpec((None, 6, LANES), lambda b, pt: (b, 0, 0))],
            out_specs=[pl.BlockSpec((None, NSA_HEADS, LANES), lambda b, pt: (b, 0, 0)),
                       pl.BlockSpec((None, 8, LANES), lambda b, pt: (b, 0, 0))],
            scratch_shapes=[pltpu.VMEM((8, n_pages, LANES), F32)]),
        out_shape=[jax.ShapeDtypeStruct((bs, NSA_HEADS, LANES), F32),
                   jax.ShapeDtypeStruct((bs, 8, LANES), jnp.int32)],
        compiler_params=_cparams(("arbitrary",)),
        name="nsa_dec_cmp",
    )(page_table, cmp_pool, q3, g3, slopes, win_t, kvn)
    k_pg = k_top - 1
    sel = idx[:, :NSA_KV_HEADS, 1:k_top].reshape(bs, NSA_KV_HEADS * k_pg)
    halves = page // SEL_BLOCK

    def blk_map(b, j, pt, sl):
        pg = pt[b, jnp.minimum(sl[b, j] // halves, n_pages - 1)]
        return (l, pg, 1, 0, 0, 0)

    out = pl.pallas_call(
        functools.partial(_nsa_dec_sel_kernel, past_len=past_len, k_top=k_pg, cur_selected=True),
        grid_spec=pltpu.PrefetchScalarGridSpec(
            num_scalar_prefetch=2, grid=(bs, NSA_KV_HEADS * k_pg),
            in_specs=[pl.BlockSpec((None, None, 2, NSA_KV_HEADS, NSA_HEAD_DIM, page), blk_map),
                      pl.BlockSpec((None, NSA_HEADS, LANES), lambda b, j, pt, sl: (b, 0, 0)),
                      pl.BlockSpec((None, NSA_HEADS, LANES), lambda b, j, pt, sl: (b, 0, 0)),
                      pl.BlockSpec((NSA_HEADS, LANES), lambda b, j, pt, sl: (0, 0)),
                      pl.BlockSpec((None, 6, LANES), lambda b, j, pt, sl: (b, 0, 0)),
                      pl.BlockSpec((None, NSA_HEADS, LANES), lambda b, j, pt, sl: (b, 0, 0))],
            out_specs=pl.BlockSpec((None, NSA_HEADS, LANES), lambda b, j, pt, sl: (b, 0, 0)),
            scratch_shapes=[pltpu.VMEM((NSA_KV_HEADS * k_pg, 2, NSA_KV_HEADS, NSA_HEAD_DIM, page), F32)]),
        out_shape=jax.ShapeDtypeStruct((bs, NSA_HEADS, LANES), BF16),
        compiler_params=_cparams(("arbitrary", "arbitrary")),
        name="nsa_dec_sel",
    )(page_table, sel, cache_t, q3, g3, slopes, kvn, part)
    return out.reshape(bs, NSA_HEADS * LANES)


def _pool_tail(sums, cur, cnts, pw_ref, scale_ref, o_ref):
    for gi in range(len(POOL_WINDOWS)):
        sl = slice(gi * LANES, (gi + 1) * LANES)
        d = sums[gi] / cnts[gi] - cur[:, sl]
        y = _dot(d.astype(BF16), pw_ref[gi])
        o_ref[:, sl] = (y * scale_ref[:, sl]).astype(BF16)


def _pool_seq_kernel(p_ref, pprev_ref, hist_ref, pw_ref, scale_ref, o_ref, zz_ref, *, seq_tiles, tm, start_pos):
    i = pl.program_id(0)
    first = (i % seq_tiles) == 0
    cur = p_ref[...]
    zz_ref[0:16, :] = jnp.where(first, hist_ref[...], pprev_ref[...])
    zz_ref[16:16 + tm, :] = cur
    pos = start_pos + (i % seq_tiles) * tm + lax.broadcasted_iota(jnp.int32, (tm, 1), 0)
    sums, cnts = [], []
    for gi, w in enumerate(POOL_WINDOWS):
        sl = slice(gi * LANES, (gi + 1) * LANES)
        s = cur[:, sl]
        for k in range(1, w):
            s = s + zz_ref[16 - k:16 - k + tm, sl]
        sums.append(s)
        cnts.append(jnp.minimum(pos + 1, w).astype(F32))
    _pool_tail(sums, cur, cnts, pw_ref, scale_ref, o_ref)


def _pool_step_kernel(z_ref, pw_ref, scale_ref, o_ref, *, start_pos):
    z = z_ref[...]
    cur = z[:, 15, :]
    sums, cnts = [], []
    for gi, w in enumerate(POOL_WINDOWS):
        sl = slice(gi * LANES, (gi + 1) * LANES)
        sums.append(jnp.sum(z[:, 16 - w:16, sl], axis=1))
        cnts.append(float(min(start_pos + 1, w)))
    _pool_tail(sums, cur, cnts, pw_ref, scale_ref, o_ref)


def _pool_seq(p_pool, hist16, pool_w, pool_scale, t, tm, start_pos):
    n = p_pool.shape[0]
    tps = t // tm
    return pl.pallas_call(
        functools.partial(_pool_seq_kernel, seq_tiles=tps, tm=tm, start_pos=start_pos),
        grid=(n // tm,),
        in_specs=[pl.BlockSpec((tm, 512), lambda i: (i, 0)),
                  pl.BlockSpec((16, 512), lambda i: (jnp.maximum(i * (tm // 16) - 1, 0), 0)),
                  pl.BlockSpec((None, 16, 512), lambda i: (i // tps, 0, 0)),
                  pl.BlockSpec((4, LANES, LANES), lambda i: (0, 0, 0)),
                  pl.BlockSpec((1, 512), lambda i: (0, 0))],
        out_specs=pl.BlockSpec((tm, 512), lambda i: (i, 0)),
        out_shape=jax.ShapeDtypeStruct((n, 512), BF16),
        scratch_shapes=[pltpu.VMEM((tm + 16, 512), F32)],
        compiler_params=_cparams(("parallel",)),
        name="pool_seq",
    )(p_pool, p_pool, hist16, pool_w, pool_scale)


def _pool_step(z16, pool_w, pool_scale, start_pos):
    bs = z16.shape[0]
    return pl.pallas_call(
        functools.partial(_pool_step_kernel, start_pos=start_pos),
        grid=(1,),
        in_specs=[pl.BlockSpec((bs, 16, 512), lambda i: (0, 0, 0)),
                  pl.BlockSpec((4, LANES, LANES), lambda i: (0, 0, 0)),
                  pl.BlockSpec((1, 512), lambda i: (0, 0))],
        out_specs=pl.BlockSpec((bs, 512), lambda i: (0, 0)),
        out_shape=jax.ShapeDtypeStruct((bs, 512), BF16),
        compiler_params=_cparams(("arbitrary",)),
        name="pool_step",
    )(z16, pool_w, pool_scale)


def _merge_kernel(x_ref, sc1_ref, sh1_ref, g1_ref, sc2_ref, sh2_ref, orw_ref, onsa_ref, opool_ref,
                  wg_ref, wbr_ref, wbn_ref, wbp_ref, wout_ref, lng_ref, lnb_ref, wr_ref, br_ref,
                  x1_ref, u2_ref, route_ref, *, alpha):
    x = x_ref[...]
    d = x.shape[1]
    u = (x * (1.0 + sc1_ref[...]) + sh1_ref[...]).astype(BF16)
    mixed = jnp.zeros(x.shape, F32)
    for bi, (o_ref, wb_ref) in enumerate(((orw_ref, wbr_ref), (onsa_ref, wbn_ref), (opool_ref, wbp_ref))):
        gate = _sigmoid(_dot(u, wg_ref[:, bi * d:(bi + 1) * d]))
        mixed = mixed + gate * _dot(o_ref[...], wb_ref[...])
    m = _dot(mixed.astype(BF16), wout_ref[...])
    x1 = _layer_norm(alpha * x + (1.0 + g1_ref[...]) * m, lng_ref[...], lnb_ref[...])
    x1_ref[...] = x1
    u2 = x1 * (1.0 + sc2_ref[...]) + sh2_ref[...]
    u2b = u2.astype(BF16)
    u2_ref[...] = u2b
    lg = _dot(u2b, wr_ref[...]) + br_ref[...]
    lane = lax.broadcasted_iota(jnp.int32, (1, LANES), 1)
    is_g = lane < MOE_GROUPS
    mg = jnp.max(jnp.where(is_g, lg, BELOW_NEG_INF), axis=1, keepdims=True)
    gsel = jnp.min(jnp.where(is_g & (lg == mg), lane, LANES), axis=1, keepdims=True)
    wgrp = 1.0 / jnp.sum(jnp.where(is_g, jnp.exp(lg - mg), 0.0), axis=1, keepdims=True)
    e_lane = lane - ROUTE_LANE0
    in_grp = (e_lane >= 0) & (e_lane < MOE_GROUPS * EXPERTS_PER_GROUP) & ((e_lane // EXPERTS_PER_GROUP) == gsel)
    v1 = jnp.max(jnp.where(in_grp, lg, BELOW_NEG_INF), axis=1, keepdims=True)
    i1 = jnp.min(jnp.where(in_grp & (lg == v1), lane, LANES), axis=1, keepdims=True)
    rest = in_grp & (lane != i1)
    v2 = jnp.max(jnp.where(rest, lg, BELOW_NEG_INF), axis=1, keepdims=True)
    i2 = jnp.min(jnp.where(rest & (lg == v2), lane, LANES), axis=1, keepdims=True)
    e2 = jnp.exp(v2 - v1)
    w1 = wgrp / (1.0 + e2)
    w2 = wgrp * e2 / (1.0 + e2)
    for g in range(MOE_GROUPS):
        src = lane + (ROUTE_LANE0 + EXPERTS_PER_GROUP * g)
        route_ref[g] = jnp.where(lane < EXPERTS_PER_GROUP,
                                 jnp.where(src == i1, w1, jnp.where(src == i2, w2, 0.0)), 0.0)


def _merge(x, mods, o_rw, o_nsa, o_pool, wts, rows_per_group, tm, alpha):
    n, d = x.shape
    full = lambda a: pl.BlockSpec(a.shape, lambda i: (0,) * a.ndim)
    row = lambda c: pl.BlockSpec((tm, c), lambda i: (i, 0))
    return pl.pallas_call(
        functools.partial(_merge_kernel, alpha=alpha),
        grid=(n // tm,),
        in_specs=[row(d)] + [_mod_spec(m, rows_per_group, tm) for m in mods]
                 + [row(512), row(1024), row(512)] + [full(w) for w in wts],
        out_specs=[row(d), row(d), pl.BlockSpec((MOE_GROUPS, tm, LANES), lambda i: (0, i, 0))],
        out_shape=[jax.ShapeDtypeStruct((n, d), F32), jax.ShapeDtypeStruct((n, d), BF16),
                   jax.ShapeDtypeStruct((MOE_GROUPS, n, LANES), F32)],
        compiler_params=_cparams(("parallel",), 56),
        name="merge",
    )(x, *mods, o_rw, o_nsa, o_pool, *wts)


def _moe_kernel(x1_ref, u2_ref, route_ref, g2_ref, wg_ref, wu_ref, wd_ref, lng_ref, lnb_ref, o_ref, acc_ref, *, alpha):
    g = pl.program_id(1)

    @pl.when(g == 0)
    def _():
        acc_ref[...] = jnp.zeros(acc_ref.shape, F32)

    u = u2_ref[...]
    hg = _dot(u, wg_ref[...])
    h = (hg * _sigmoid(hg) * _dot(u, wu_ref[...])).astype(BF16)
    route = route_ref[...]
    y = acc_ref[...]
    for e in range(EXPERTS_PER_GROUP):
        w = route[:, e:e + 1]
        ye = _dot(h[:, e * EXPERT_HIDDEN:(e + 1) * EXPERT_HIDDEN], wd_ref[e * EXPERT_HIDDEN:(e + 1) * EXPERT_HIDDEN, :])
        y = y + jnp.where(w != 0.0, w * ye, 0.0)
    acc_ref[...] = y

    @pl.when(g == pl.num_programs(1) - 1)
    def _():
        h2 = alpha * x1_ref[...] + (1.0 + g2_ref[...]) * acc_ref[...]
        o_ref[...] = _layer_norm(h2, lng_ref[...], lnb_ref[...])


def _moe(x1, u2, route, g2, w_gate, w_up, w_down, ln_g, ln_b, rows_per_group, tm, alpha):
    n, d = x1.shape
    hid = EXPERTS_PER_GROUP * EXPERT_HIDDEN
    r = g2.shape[1]
    return pl.pallas_call(
        functools.partial(_moe_kernel, alpha=alpha),
        grid=(n // tm, MOE_GROUPS),
        in_specs=[pl.BlockSpec((tm, d), lambda i, g: (i, 0)),
                  pl.BlockSpec((tm, d), lambda i, g: (i, 0)),
                  pl.BlockSpec((None, tm, LANES), lambda i, g: (g, i, 0)),
                  pl.BlockSpec((None, r, d), lambda i, g: ((i * tm) // rows_per_group, 0, 0)),
                  pl.BlockSpec((None, d, hid), lambda i, g: (g, 0, 0)),
                  pl.BlockSpec((None, d, hid), lambda i, g: (g, 0, 0)),
                  pl.BlockSpec((None, hid, d), lambda i, g: (g, 0, 0)),
                  pl.BlockSpec((1, d), lambda i, g: (0, 0)),
                  pl.BlockSpec((1, d), lambda i, g: (0, 0))],
        out_specs=pl.BlockSpec((tm, d), lambda i, g: (i, 0)),
        out_shape=jax.ShapeDtypeStruct((n, d), F32),
        scratch_shapes=[pltpu.VMEM((tm, d), F32)],
        compiler_params=_cparams(("parallel", "arbitrary"), 56),
        name="moe",
    )(x1, u2, route, g2, w_gate, w_up, w_down, ln_g, ln_b)


def _pad_heads(w, axis):
    shp = w.shape
    w = w.reshape(shp[:axis] + (NSA_KV_HEADS, NSA_GROUP, 1, NSA_HEAD_DIM) + shp[axis + 1:])
    sel = jnp.eye(NSA_KV_HEADS, dtype=w.dtype).reshape((1,) * axis + (NSA_KV_HEADS, 1, NSA_KV_HEADS, 1) + (1,) * (len(shp) - axis - 1))
    w = w * sel
    return w.reshape(shp[:axis] + (NSA_HEADS * LANES,) + shp[axis + 1:])


def _prep_weights(w_in, rw_w_up, rw_a_up, w_branch, moe_router_g, moe_bias_g, moe_router_e, moe_bias_e,
                  moe_w_gate, moe_w_up, moe_w_down, nsa_cmp_wk, nsa_cmp_wv):
    L, D, _ = w_in.shape
    w = {}
    w['rw'] = w_in[:, :, :RW_END].astype(BF16)
    wq = _pad_heads(w_in[:, :, RW_END:NSA_Q_END] * (NSA_HEAD_DIM ** -0.5), 2)
    wg = jnp.pad(w_in[:, :, NSA_KV_END:NSA_END], ((0, 0), (0, 0), (0, LANES - 3 * NSA_HEADS)))
    w['nsa'] = jnp.concatenate([wq, w_in[:, :, NSA_Q_END:NSA_KV_END], wg], axis=-1).astype(BF16)
    w['pool'] = w_in[:, :, NSA_END:POOL_END].astype(BF16)
    w['gate'] = w_in[:, :, POOL_END:].astype(BF16)
    z = jnp.zeros((L, HALF, RW_WIDTH), F32)
    w['wwa'] = jnp.concatenate([jnp.concatenate([rw_w_up, z], axis=2),
                                jnp.concatenate([z, rw_a_up], axis=2)], axis=1).astype(BF16)
    w['b_rw'] = w_branch[:, 0].astype(BF16)
    w['b_nsa'] = _pad_heads(w_branch[:, 1], 1).astype(BF16)
    w['b_pool'] = w_branch[:, 2].astype(BF16)
    pad_r = LANES - MOE_GROUPS - MOE_GROUPS * EXPERTS_PER_GROUP
    w['router'] = jnp.pad(jnp.concatenate([moe_router_g, moe_router_e], axis=2), ((0, 0), (0, 0), (0, pad_r))).astype(BF16)
    w['router_b'] = jnp.pad(jnp.concatenate([moe_bias_g, moe_bias_e], axis=1), ((0, 0), (0, pad_r)))[:, None, :]
    hid = EXPERTS_PER_GROUP * EXPERT_HIDDEN
    grp = lambda a: jnp.transpose(a.reshape(L, MOE_GROUPS, EXPERTS_PER_GROUP, D, EXPERT_HIDDEN),
                                  (0, 1, 3, 2, 4)).reshape(L, MOE_GROUPS, D, hid).astype(BF16)
    w['moe_gate'] = grp(moe_w_gate)
    w['moe_up'] = grp(moe_w_up)
    w['moe_down'] = moe_w_down.reshape(L, MOE_GROUPS, hid, D).astype(BF16)
    w['cmp_wk'] = jnp.tile(nsa_cmp_wk, (1, 1, NSA_KV_HEADS))
    w['cmp_wv'] = jnp.tile(nsa_cmp_wv, (1, 1, NSA_KV_HEADS))
    return w


def kernel(x_prompt, x_sample, cache_nsa_kv, cache_win_kv, state_rwkv, state_rwkv_shift, state_pool, page_table, c_prompt, c_sample, ada_w, ada_b, w_in, rw_mu, rw_w0, rw_w_up, rw_a0, rw_a_up, rw_g_up, rw_k_k, rw_k_a, rw_r_k, rw_gn_g, rw_gn_b, nsa_cmp_wk, nsa_cmp_wv, pool_w, pool_scale, w_branch, w_out, ln1_g, ln1_b, moe_router_g, moe_bias_g, moe_router_e, moe_bias_e, moe_w_gate, moe_w_up, moe_w_down, ln2_g, ln2_b):
    bp, t, d = x_prompt.shape
    bs, ts, _ = x_sample.shape
    L = ada_w.shape[0]
    n_pool, page = cache_nsa_kv.shape[1], cache_nsa_kv.shape[2]
    n_pages = page_table.shape[1]
    past_len = n_pages * page
    assert ts == 1 and d == D_MODEL
    assert t % 1024 == 0 and t // CMP_BLOCK <= LANES and past_len % SEL_BLOCK == 0
    assert cache_win_kv.shape[2] == WINDOW and bs % 8 == 0 and 2 * n_pages == LANES
    alpha = (2 * L) ** 0.25
    tm = 512

    w = _prep_weights(w_in, rw_w_up, rw_a_up, w_branch, moe_router_g, moe_bias_g, moe_router_e, moe_bias_e,
                      moe_w_gate, moe_w_up, moe_w_down, nsa_cmp_wk, nsa_cmp_wv)
    w_out_b = w_out.astype(BF16)
    pool_w_b = pool_w.astype(BF16)
    gup_b = rw_g_up.astype(BF16)

    nb = bp + bs
    nb_pad = -(-nb // 8) * 8
    c_all = jnp.pad(jnp.concatenate([c_prompt, c_sample], axis=0), ((0, nb_pad - nb), (0, 0)))
    mods = _ada_mod(c_all, ada_w, ada_b)

    cache_t = jnp.transpose(cache_nsa_kv, (0, 1, 3, 4, 5, 2)).astype(F32)
    win_cache = jnp.transpose(cache_win_kv, (0, 1, 3, 4, 5, 2)).astype(F32)
    pb = 16 if n_pool % 16 == 0 else (8 if n_pool % 8 == 0 else 1)
    cmp_w_t = jnp.tile(jnp.transpose(jnp.stack([nsa_cmp_wk, nsa_cmp_wv], axis=1), (0, 1, 3, 2)),
                       (1, 1, NSA_KV_HEADS, page // CMP_BLOCK))
    cmp_pool = _compress_pool(cache_t, cmp_w_t, pb)
    slopes = jnp.broadcast_to(jnp.asarray(NSA_SLOPES, F32)[:, None], (NSA_HEADS, LANES))

    xp = x_prompt.reshape(bp * t, d)
    xs = x_sample.reshape(bs, d)
    zeros_prev = jnp.zeros((bp, RW_COLS), F32)
    zeros_hist = jnp.zeros((bp, 16, POOL_WIDTH), F32)
    outs = {k: [] for k in ('nsa_p', 'nsa_s', 'win_p', 'win_s', 'rw_p', 'rw_s', 'sh_p', 'sh_s', 'pool_p', 'pool_s')}

    for l in range(L):
        vec = lambda a: a[l].reshape(1, -1)
        rw_wts = (vec(rw_mu), vec(rw_w0), vec(rw_a0), w['wwa'][l], gup_b[l], vec(rw_k_k), vec(rw_k_a), vec(rw_r_k))
        merge_wts = (w['gate'][l], w['b_rw'][l], w['b_nsa'][l], w['b_pool'][l], w_out_b[l], vec(ln1_g), vec(ln1_b),
                     w['router'][l], w['router_b'][l])
        mod_p = [m[:, None, :] for m in jnp.split(mods[l, :bp], 6, axis=-1)]
        mod_s = [m[None] for m in jnp.split(mods[l, bp:bp + bs], 6, axis=-1)]

        p_rw = _proj(xp, mod_p[1], mod_p[0], w['rw'][l], t, tm)
        q_pad, kv, gates, kvt, cmp = _proj_nsa(xp, mod_p[1], mod_p[0], w['nsa'][l], w['cmp_wk'][l], w['cmp_wv'][l], bp, t, tm)
        p_pool = _proj(xp, mod_p[1], mod_p[0], w['pool'][l], t, tm)
        seqs = _rwkv_prep(p_rw, zeros_prev, rw_wts, t, tm)
        o_rw, s_pairs = _rwkv_chunk(seqs, vec(rw_gn_g), vec(rw_gn_b), bp, t)
        o_nsa = _nsa_prompt(q_pad, kvt, _permute_cmp(cmp, bp, t), gates, bp, t)
        o_pool = _pool_seq(p_pool, zeros_hist, pool_w_b[l], vec(pool_scale), t, tm, 0)
        x1, u2, route = _merge(xp, (mod_p[1], mod_p[0], mod_p[2], mod_p[4], mod_p[3]), o_rw, o_nsa, o_pool,
                               merge_wts, t, tm, alpha)
        xp = _moe(x1, u2, route, mod_p[5], w['moe_gate'][l], w['moe_up'][l], w['moe_down'][l],
                  vec(ln2_g), vec(ln2_b), t, 2 * tm, alpha)
        kv3 = kv.reshape(bp, t, 6, NSA_KV_HEADS, NSA_HEAD_DIM)
        outs['nsa_p'].append(kv3[:, :, :4])
        outs['win_p'].append(kv3[:, t - min(WINDOW, t):, 4:])
        outs['rw_p'].append(_state_from_pairs(s_pairs))
        outs['sh_p'].append(p_rw.reshape(bp, t, RW_COLS)[:, -1])
        outs['pool_p'].append(p_pool.reshape(bp, t, POOL_WIDTH)[:, t - POOL_HIST:])

        p_rw = _proj(xs, mod_s[1], mod_s[0], w['rw'][l], bs, bs)
        q_pad, kv, gates = _proj_nsa(xs, mod_s[1], mod_s[0], w['nsa'][l], w['cmp_wk'][l], w['cmp_wv'][l], bs, 1, bs)
        p_pool = _proj(xs, mod_s[1], mod_s[0], w['pool'][l], bs, bs)
        seqs = _rwkv_prep(p_rw, state_rwkv_shift[l], rw_wts, 1, bs)
        o_rw, s_pairs = _rwkv_step(seqs, _state_to_pairs(state_rwkv[l]), vec(rw_gn_g), vec(rw_gn_b))
        o_nsa = _nsa_decode(page_table, cmp_pool, cache_t, win_cache, l, q_pad, gates, kv, slopes, past_len)
        z16 = jnp.concatenate([state_pool[l].astype(F32), p_pool[:, None, :]], axis=1)
        o_pool = _pool_step(z16, pool_w_b[l], vec(pool_scale), past_len)
        x1, u2, route = _merge(xs, (mod_s[1], mod_s[0], mod_s[2], mod_s[4], mod_s[3]), o_rw, o_nsa, o_pool,
                               merge_wts, bs, bs, alpha)
        xs = _moe(x1, u2, route, mod_s[5], w['moe_gate'][l], w['moe_up'][l], w['moe_down'][l],
                  vec(ln2_g), vec(ln2_b), bs, bs, alpha)
        kv3 = kv.reshape(bs, 1, 6, NSA_KV_HEADS, NSA_HEAD_DIM)
        outs['nsa_s'].append(kv3[:, :, :4])
        outs['win_s'].append(jnp.concatenate([cache_win_kv[l].astype(F32), kv3[:, :, 4:]], axis=1)[:, -WINDOW:])
        outs['rw_s'].append(_state_from_pairs(s_pairs))
        outs['sh_s'].append(p_rw)
        outs['pool_s'].append(z16[:, 1:])

    st = lambda k: jnp.stack(outs[k])
    return (xp.reshape(bp, t, d), xs.reshape(bs, 1, d), st('nsa_p'), st('nsa_s'), st('win_p'), st('win_s'),
            st('rw_p'), st('rw_s'), st('sh_p'), st('sh_s'), st('pool_p'), st('pool_s'))
```
